```python
import jax, jax.numpy as jnp
from jax import lax
import numpy as np

D_MODEL = 1024
BATCH = 16
SEQ = 4096
DEPTH = 4
DEC_BATCH = 8
DEC_SEQ = 32
PAST_LEN = 4096

CHUNK = 64
HEAD_DIM = 64
N_HEADS_A = 8
BAND_A = 9
REL_CLIP = 128
N_HEADS_B = 8
N_KV_B = 2
WINDOW_B = 128
BAND_B = 1 + WINDOW_B // CHUNK
N_EXPERTS = 16
N_GROUPS = 4
EXPERTS_PER_GROUP = N_EXPERTS // N_GROUPS
TOP_K = 2
D_EXPERT = 512
LN_EPS = 1e-5
RMS_EPS = 1e-6
ALPHA = (2.0 * DEPTH) ** 0.25
BETA = (8.0 * DEPTH) ** -0.25

WIDTH_A = N_HEADS_A * HEAD_DIM
WIDTH_B = N_HEADS_B * HEAD_DIM
KV_WIDTH_B = N_KV_B * HEAD_DIM
D_IN = 3 * WIDTH_A + WIDTH_B + 2 * KV_WIDTH_B
SPLITS = (WIDTH_A, 2 * WIDTH_A, 3 * WIDTH_A, 3 * WIDTH_A + WIDTH_B, 3 * WIDTH_A + WIDTH_B + KV_WIDTH_B)

kernel_name = "hymba_chunk_stream_encoder_step"


def layer_norm(x, g, b):
    xf = x.astype(jnp.float32)
    mu = jnp.mean(xf, -1, keepdims=True)
    var = jnp.mean(jnp.square(xf - mu), -1, keepdims=True)
    return ((xf - mu) * lax.rsqrt(var + LN_EPS) * g.astype(jnp.float32) + b.astype(jnp.float32)).astype(x.dtype)


def rms_norm(x, g):
    xf = x.astype(jnp.float32)
    return (xf * lax.rsqrt(jnp.mean(xf * xf, -1, keepdims=True) + RMS_EPS) * g.astype(jnp.float32)).astype(x.dtype)


def alibi_slopes():
    return 2.0 ** (-8.0 * jnp.arange(1, N_HEADS_B + 1, dtype=jnp.float32) / N_HEADS_B)


def project(x, w_in):
    B, S, _ = x.shape
    qa, ka, va, qb, kb, vb = jnp.split(x @ w_in, SPLITS, axis=-1)
    heads = lambda t, n: t.reshape(B, S, n, HEAD_DIM)
    return (heads(qa, N_HEADS_A), heads(ka, N_HEADS_A), heads(va, N_HEADS_A),
            heads(qb, N_HEADS_B), heads(kb, N_KV_B), heads(vb, N_KV_B))


def band_positions(n_chunks, band):
    return (jnp.arange(n_chunks, dtype=jnp.int32)[:, None] * CHUNK
            + jnp.arange(band * CHUNK, dtype=jnp.int32)[None, :] - (band - 1) * CHUNK)


def gather_band(t, pos, band):
    pad = (band - 1) * CHUNK
    tp = jnp.pad(t, ((0, 0), (pad, 0), (0, 0), (0, 0)))
    return jnp.take(tp, pos + pad, axis=1)


def band_rel_attention(q, k, v, q_pos, k_pos, rel_bias):
    s = jnp.einsum('bnqhd,bnshd->bnhqs', q, k).astype(jnp.float32) * (HEAD_DIM ** -0.5)
    dist = q_pos[:, :, None] - k_pos[:, None, :]
    bias = jnp.take(rel_bias.astype(jnp.float32), jnp.clip(dist, -REL_CLIP, REL_CLIP) + REL_CLIP, axis=1)
    s = s + jnp.transpose(bias, (1, 0, 2, 3))[None]
    s = jnp.where((k_pos >= 0)[None, :, None, None, :], s, -jnp.inf)
    p = jax.nn.softmax(s, axis=-1).astype(v.dtype)
    return jnp.einsum('bnhqs,bnshd->bnqhd', p, v)


def band_sink_attention(q, k, v, q_pos, k_pos, sinks):
    B, N, Cq = q.shape[:3]
    G = N_HEADS_B // N_KV_B
    qg = q.reshape(B, N, Cq, N_KV_B, G, HEAD_DIM)
    s = jnp.einsum('bnqkgd,bnskd->bnkgqs', qg, k).astype(jnp.float32) * (HEAD_DIM ** -0.5)
    dist = jnp.abs(q_pos[:, :, None] - k_pos[:, None, :]).astype(jnp.float32)
    slopes = alibi_slopes().reshape(N_KV_B, G)
    s = s - slopes[None, None, :, :, None, None] * dist[None, :, None, None]
    s = jnp.where((k_pos >= 0)[None, :, None, None, None, :], s, -jnp.inf)
    sink = sinks.astype(jnp.float32).reshape(N_KV_B, G)[None, None, :, :, None, None]
    m = jnp.maximum(jnp.max(s, -1, keepdims=True), sink)
    e = jnp.exp(s - m)
    p = (e / (jnp.sum(e, -1, keepdims=True) + jnp.exp(sink - m))).astype(v.dtype)
    o = jnp.einsum('bnkgqs,bnskd->bnqkgd', p, v)
    return o.reshape(B, N, Cq, N_HEADS_B, HEAD_DIM)


def merge_heads(oa, ob, gn_a, gn_b, w_out):
    return jnp.concatenate([rms_norm(oa, gn_a), rms_norm(ob, gn_b)], axis=-1) @ w_out


def mixer_prompt(x, w_in, rel_bias, sinks, gn_a, gn_b, w_out):
    B, S, _ = x.shape
    n = S // CHUNK
    qa, ka, va, qb, kb, vb = project(x, w_in)
    q_pos = jnp.arange(S, dtype=jnp.int32).reshape(n, CHUNK)
    pos_a = band_positions(n, BAND_A)
    pos_b = band_positions(n, BAND_B)
    oa = band_rel_attention(qa.reshape(B, n, CHUNK, N_HEADS_A, HEAD_DIM),
                            gather_band(ka, pos_a, BAND_A), gather_band(va, pos_a, BAND_A),
                            q_pos, pos_a, rel_bias)
    ob = band_sink_attention(qb.reshape(B, n, CHUNK, N_HEADS_B, HEAD_DIM),
                             gather_band(kb, pos_b, BAND_B), gather_band(vb, pos_b, BAND_B),
                             q_pos, pos_b, sinks)
    y = merge_heads(oa.reshape(B, S, WIDTH_A), ob.reshape(B, S, WIDTH_B), gn_a, gn_b, w_out)
    ra = min((BAND_A - 1) * CHUNK, S)
    rb = min(WINDOW_B, S)
    return y, ka[:, S - ra:], va[:, S - ra:], kb[:, S - rb:], vb[:, S - rb:]


def mixer_sample(x, ck_a, cv_a, ck_b, cv_b, w_in, rel_bias, sinks, gn_a, gn_b, w_out):
    B, S, _ = x.shape
    qa, ka, va, qb, kb, vb = project(x, w_in)
    ra, rb = ck_a.shape[1], ck_b.shape[1]
    q_pos = (PAST_LEN + jnp.arange(S, dtype=jnp.int32))[None]
    pos_a = (PAST_LEN - ra + jnp.arange(ra + S, dtype=jnp.int32))[None]
    pos_b = (PAST_LEN - rb + jnp.arange(rb + S, dtype=jnp.int32))[None]
    band = lambda c, t: jnp.concatenate([c, t], axis=1)[:, None]
    oa = band_rel_attention(qa[:, None], band(ck_a, ka), band(cv_a, va), q_pos, pos_a, rel_bias)
    ob = band_sink_attention(qb[:, None], band(ck_b, kb), band(cv_b, vb), q_pos, pos_b, sinks)
    y = merge_heads(oa.reshape(B, S, WIDTH_A), ob.reshape(B, S, WIDTH_B), gn_a, gn_b, w_out)
    return y, ka, va, kb, vb


def route(xt, w_router, b_router):
    T = xt.shape[0]
    logits = (xt @ w_router).astype(jnp.float32) + b_router.astype(jnp.float32)
    grouped = logits.reshape(T, N_GROUPS, EXPERTS_PER_GROUP)
    group_score = jnp.sum(lax.top_k(grouped, TOP_K)[0], axis=-1)
    g_sel = jnp.argmax(group_score, axis=-1)
    in_group = jnp.sum(grouped * jax.nn.one_hot(g_sel, N_GROUPS, dtype=jnp.float32)[..., None], axis=1)
    top_val, top_idx = lax.top_k(in_group, TOP_K)
    w = jax.nn.softmax(top_val, axis=-1)
    expert_idx = g_sel[:, None] * EXPERTS_PER_GROUP + top_idx
    return jnp.sum(jax.nn.one_hot(expert_idx, N_EXPERTS, dtype=jnp.float32) * w[..., None], axis=1)


def moe(x, w_router, b_router, w_gate, w_up, w_down):
    B, S, D = x.shape
    xt = x.reshape(B * S, D)
    gate = route(xt, w_router, b_router).astype(x.dtype)
    y = jnp.zeros_like(xt)
    for e in range(N_EXPERTS):
        h = jax.nn.silu(xt @ w_gate[e]) * (xt @ w_up[e])
        y = y + gate[:, e:e + 1] * (h @ w_down[e])
    return y.reshape(B, S, D)


def setup_inputs(seed: int = 0) -> dict:
    key = jax.random.key(seed)
    ks = jax.random.split(key, 22)
    nrm = lambda k, shape, scale: scale * jax.random.normal(k, shape, jnp.float32)
    rows_a = min((BAND_A - 1) * CHUNK, PAST_LEN)
    rows_b = min(WINDOW_B, PAST_LEN)
    return {
        "x_prompt": nrm(ks[0], (BATCH, SEQ, D_MODEL), 1.0),
        "x_sample": nrm(ks[1], (DEC_BATCH, DEC_SEQ, D_MODEL), 1.0),
        "cache_a_k": nrm(ks[2], (DEPTH, DEC_BATCH, rows_a, N_HEADS_A, HEAD_DIM), 1.0),
        "cache_a_v": nrm(ks[3], (DEPTH, DEC_BATCH, rows_a, N_HEADS_A, HEAD_DIM), 1.0),
        "cache_b_k": nrm(ks[4], (DEPTH, DEC_BATCH, rows_b, N_KV_B, HEAD_DIM), 1.0),
        "cache_b_v": nrm(ks[5], (DEPTH, DEC_BATCH, rows_b, N_KV_B, HEAD_DIM), 1.0),
        "w_in": nrm(ks[6], (DEPTH, D_MODEL, D_IN), D_MODEL ** -0.5),
        "rel_bias": nrm(ks[7], (DEPTH, N_HEADS_A, 2 * REL_CLIP + 1), 0.5),
        "attn_sinks": nrm(ks[8], (DEPTH, N_HEADS_B), 1.0),
        "gn_a": 1.0 + nrm(ks[9], (DEPTH, WIDTH_A), 0.02),
        "gn_b": 1.0 + nrm(ks[10], (DEPTH, WIDTH_B), 0.02),
        "w_out": nrm(ks[11], (DEPTH, D_MODEL, D_MODEL), BETA * D_MODEL ** -0.5),
        "ln1_g": 1.0 + nrm(ks[12], (DEPTH, D_MODEL), 0.02),
        "ln1_b": nrm(ks[13], (DEPTH, D_MODEL), 0.02),
        "w_router": nrm(ks[14], (D_MODEL, N_EXPERTS), D_MODEL ** -0.5),
        "b_router": nrm(ks[15], (N_EXPERTS,), 0.01),
        "w_gate": nrm(ks[16], (DEPTH, N_EXPERTS, D_MODEL, D_EXPERT), D_MODEL ** -0.5),
        "w_up": nrm(ks[17], (DEPTH, N_EXPERTS, D_MODEL, D_EXPERT), D_MODEL ** -0.5),
        "w_down": nrm(ks[18], (DEPTH, N_EXPERTS, D_EXPERT, D_MODEL), BETA * D_EXPERT ** -0.5),
        "ln2_g": 1.0 + nrm(ks[19], (DEPTH, D_MODEL), 0.02),
        "ln2_b": nrm(ks[20], (DEPTH, D_MODEL), 0.02),
    }


def reference(x_prompt, x_sample, cache_a_k, cache_a_v, cache_b_k, cache_b_v,
              w_in, rel_bias, attn_sinks, gn_a, gn_b, w_out, ln1_g, ln1_b,
              w_router, b_router, w_gate, w_up, w_down, ln2_g, ln2_b):
    xp, xs = x_prompt, x_sample
    pa_k, pa_v, pb_k, pb_v = [], [], [], []
    sa_k, sa_v, sb_k, sb_v = [], [], [], []
    for l in range(DEPTH):
        lw = (w_in[l], rel_bias[l], attn_sinks[l], gn_a[l], gn_b[l], w_out[l])
        yp, ka, va, kb, vb = mixer_prompt(xp, *lw)
        ys, ka2, va2, kb2, vb2 = mixer_sample(xs, cache_a_k[l], cache_a_v[l], cache_b_k[l], cache_b_v[l], *lw)
        pa_k.append(ka); pa_v.append(va); pb_k.append(kb); pb_v.append(vb)
        sa_k.append(ka2); sa_v.append(va2); sb_k.append(kb2); sb_v.append(vb2)
        xp = layer_norm(ALPHA * xp + yp, ln1_g[l], ln1_b[l])
        xs = layer_norm(ALPHA * xs + ys, ln1_g[l], ln1_b[l])
        xp = layer_norm(ALPHA * xp + moe(xp, w_router, b_router, w_gate[l], w_up[l], w_down[l]), ln2_g[l], ln2_b[l])
        xs = layer_norm(ALPHA * xs + moe(xs, w_router, b_router, w_gate[l], w_up[l], w_down[l]), ln2_g[l], ln2_b[l])
    return (xp, xs,
            jnp.stack(pa_k), jnp.stack(pa_v), jnp.stack(pb_k), jnp.stack(pb_v),
            jnp.stack(sa_k), jnp.stack(sa_v), jnp.stack(sb_k), jnp.stack(sb_v))
```

```python
import functools

import jax
import jax.numpy as jnp
import numpy as np
from jax import lax
from jax.experimental import pallas as pl
from jax.experimental.pallas import tpu as pltpu

CHUNK = 64
HEAD_DIM = 64
N_HEADS_A = 8
BAND_A = 9
REL_CLIP = 128
N_HEADS_B = 8
N_KV_B = 2
WINDOW_B = 128
BAND_B = 1 + WINDOW_B // CHUNK
N_EXPERTS = 16
N_GROUPS = 4
EXPERTS_PER_GROUP = N_EXPERTS // N_GROUPS
PAIRS_PER_GROUP = 6
N_CLASSES = N_GROUPS * PAIRS_PER_GROUP
LN_EPS = 1e-5
RMS_EPS = 1e-6

WIDTH_A = N_HEADS_A * HEAD_DIM
WIDTH_B = N_HEADS_B * HEAD_DIM
KV_WIDTH_B = N_KV_B * HEAD_DIM
D_IN = 3 * WIDTH_A + WIDTH_B + 2 * KV_WIDTH_B
COL_QA, COL_KA, COL_VA = 0, WIDTH_A, 2 * WIDTH_A
COL_QB = 3 * WIDTH_A
COL_KB = COL_QB + WIDTH_B
COL_VB = COL_KB + KV_WIDTH_B
KV_OUT = 2 * WIDTH_A + 2 * KV_WIDTH_B
PAD_A = (BAND_A - 1) * CHUNK
PAD_B = WINDOW_B

V7X_LANES = 128
TM = 512
GATE_LANES = V7X_LANES
V7X_VMEM_LIMIT = 56 * 1024 * 1024

F32 = jnp.float32
BF16 = jnp.bfloat16
NEG_INF = float("-inf")


def _cparams(sem):
    return pltpu.CompilerParams(dimension_semantics=sem, vmem_limit_bytes=V7X_VMEM_LIMIT)


def _inproj_kernel(x_ref, w_ref, h_ref, kv_ref, *, tiles_per_seq, n_prompt_tiles):
    i = pl.program_id(0)
    acc = jnp.dot(x_ref[...].astype(BF16), w_ref[...], preferred_element_type=F32)
    h_ref[...] = acc.astype(BF16)
    keeps_kv = jnp.logical_or(i % tiles_per_seq == tiles_per_seq - 1, i >= n_prompt_tiles)

    @pl.when(keeps_kv)
    def _():
        kv_ref[:, : 2 * WIDTH_A] = acc[:, COL_KA:COL_QB]
        kv_ref[:, 2 * WIDTH_A:] = acc[:, COL_KB:]


def _inproj(x, w_in, *, seq, n_prompt_tiles):
    t_pad, d = x.shape
    n_tiles = t_pad // TM
    tiles_per_seq = seq // TM
    n_seq = n_prompt_tiles // tiles_per_seq
    n_kv_blocks = n_seq + (n_tiles - n_prompt_tiles)

    def kv_map(i):
        return (jnp.where(i < n_prompt_tiles, i // tiles_per_seq, n_seq + i - n_prompt_tiles), 0)

    return pl.pallas_call(
        functools.partial(_inproj_kernel, tiles_per_seq=tiles_per_seq, n_prompt_tiles=n_prompt_tiles),
        grid=(n_tiles,),
        in_specs=[pl.BlockSpec((TM, d), lambda i: (i, 0)),
                  pl.BlockSpec((d, D_IN), lambda i: (0, 0))],
        out_specs=[pl.BlockSpec((TM, D_IN), lambda i: (i, 0)),
                   pl.BlockSpec((TM, KV_OUT), kv_map)],
        out_shape=[jax.ShapeDtypeStruct((t_pad, D_IN), BF16),
                   jax.ShapeDtypeStruct((n_kv_blocks * TM, KV_OUT), F32)],
        compiler_params=_cparams(("arbitrary",)),
        name="inproj",
    )(x, w_in)


def _softmax_pv(s_parts, v_parts, sink=None):
    m = s_parts[0].max(axis=-1, keepdims=True)
    for s in s_parts[1:]:
        m = jnp.maximum(m, s.max(axis=-1, keepdims=True))
    if sink is not None:
        m = jnp.maximum(m, sink)
    denom = None
    acc = None
    for s, v in zip(s_parts, v_parts):
        e = jnp.exp(s - m)
        d = e.sum(axis=-1, keepdims=True)
        pv = jnp.dot(e.astype(BF16), v, preferred_element_type=F32)
        denom = d if denom is None else denom + d
        acc = pv if acc is None else acc + pv
    if sink is not None:
        denom = denom + jnp.exp(sink - m)
    return acc / denom


def _nt_dot(a, b):
    return lax.dot_general(a, b, (((1,), (1,)), ((), ())), preferred_element_type=F32)


def _head_pair_lane_mask(rows):
    return lax.broadcasted_iota(jnp.int32, (rows, 2 * HEAD_DIM), 1) < HEAD_DIM


def _rms_store(o_ref, row_slice, col0, pairs, gn_ref):
    ssq = None
    for o in pairs:
        s = (o * o).sum(axis=-1, keepdims=True)
        ssq = s if ssq is None else ssq + s
    width = len(pairs) * 2 * HEAD_DIM
    inv = lax.rsqrt(ssq / width + RMS_EPS)
    for p, o in enumerate(pairs):
        c0 = p * 2 * HEAD_DIM
        g = gn_ref[:, c0:c0 + 2 * HEAD_DIM]
        o_ref[row_slice, col0 + c0:col0 + c0 + 2 * HEAD_DIM] = (o * inv * g).astype(o_ref.dtype)


def _attn_prompt_kernel(sink_ref, qa_ref, ka_ref, va_ref, qb_ref, kb_ref, vb_ref,
                        ba_ref, bb_ref, gna_ref, gnb_ref, o_ref,
                        kpa, vpa, kpb, vpb, *, seq, tq):
    j = pl.program_id(1)

    @pl.when(j == 0)
    def _():
        kpa[0:PAD_A, :] = jnp.zeros((PAD_A, WIDTH_A), BF16)
        vpa[0:PAD_A, :] = jnp.zeros((PAD_A, WIDTH_A), BF16)
        kpb[0:PAD_B, :] = jnp.zeros((PAD_B, KV_WIDTH_B), BF16)
        vpb[0:PAD_B, :] = jnp.zeros((PAD_B, KV_WIDTH_B), BF16)
        kpa[PAD_A:PAD_A + seq, :] = ka_ref[...]
        vpa[PAD_A:PAD_A + seq, :] = va_ref[...]
        kpb[PAD_B:PAD_B + seq, :] = kb_ref[...]
        vpb[PAD_B:PAD_B + seq, :] = vb_ref[...]

    lane_lo = _head_pair_lane_mask(CHUNK)
    band_a = BAND_A * CHUNK
    band_b = BAND_B * CHUNK
    col_a = lax.broadcasted_iota(jnp.int32, (CHUNK, band_a), 1)
    col_b = lax.broadcasted_iota(jnp.int32, (CHUNK, band_b), 1)
    scale = HEAD_DIM ** -0.5

    def chunk_body(c, carry):
        r0 = pl.multiple_of(c * CHUNK, CHUNK)
        cs = pl.multiple_of(j * tq + c * CHUNK, CHUNK)
        rows = pl.ds(r0, CHUNK)
        valid_a = col_a + cs >= PAD_A
        valid_b = col_b + cs >= PAD_B

        pairs_a = []
        for p in range(N_HEADS_A // 2):
            lanes = slice(p * 2 * HEAD_DIM, (p + 1) * 2 * HEAD_DIM)
            q2 = qa_ref[rows, lanes] * scale
            k2 = kpa[pl.ds(cs, band_a), lanes]
            v2 = vpa[pl.ds(cs, band_a), lanes]
            halves = []
            for half in range(2):
                keep = lane_lo if half == 0 else jnp.logical_not(lane_lo)
                qm = jnp.where(keep, q2, jnp.zeros_like(q2))
                s = _nt_dot(qm, k2) + ba_ref[2 * p + half]
                s = jnp.where(valid_a, s, NEG_INF)
                halves.append(_softmax_pv([s], [v2]))
            pairs_a.append(jnp.where(lane_lo, halves[0], halves[1]))
        _rms_store(o_ref, rows, 0, pairs_a, gna_ref)

        k2 = kpb[pl.ds(cs, band_b), :]
        v2 = vpb[pl.ds(cs, band_b), :]
        pairs_b = []
        for p in range(N_HEADS_B // 2):
            lanes = slice(p * 2 * HEAD_DIM, (p + 1) * 2 * HEAD_DIM)
            q2 = qb_ref[rows, lanes] * scale
            halves = []
            for half in range(2):
                h = 2 * p + half
                kv_head = h // (N_HEADS_B // N_KV_B)
                keep = lane_lo if half == 0 else jnp.logical_not(lane_lo)
                qm = jnp.where(keep, q2, jnp.zeros_like(q2))
                if kv_head != half:
                    qm = jnp.concatenate([qm[:, HEAD_DIM:], qm[:, :HEAD_DIM]], axis=1)
                s = _nt_dot(qm, k2) + bb_ref[h]
                s = jnp.where(valid_b, s, NEG_INF)
                o = _softmax_pv([s], [v2], sink=sink_ref[h])
                if kv_head != half:
                    o = jnp.concatenate([o[:, HEAD_DIM:], o[:, :HEAD_DIM]], axis=1)
                halves.append(o)
            pairs_b.append(jnp.where(lane_lo, halves[0], halves[1]))
        _rms_store(o_ref, rows, WIDTH_A, pairs_b, gnb_ref)
        return carry

    lax.fori_loop(0, tq // CHUNK, chunk_body, 0)


def _attn_prompt(h, sinks, bias_a, bias_b, gn_a, gn_b, *, n_seq, seq):
    tq = TM
    n_q = seq // tq
    vmem = pltpu.VMEM
    return pl.pallas_call(
        functools.partial(_attn_prompt_kernel, seq=seq, tq=tq),
        grid=(n_seq, n_q),
        in_specs=[
            pl.BlockSpec(memory_space=pltpu.SMEM),
            pl.BlockSpec((tq, WIDTH_A), lambda b, j: (b * n_q + j, COL_QA // WIDTH_A)),
            pl.BlockSpec((seq, WIDTH_A), lambda b, j: (b, COL_KA // WIDTH_A)),
            pl.BlockSpec((seq, WIDTH_A), lambda b, j: (b, COL_VA // WIDTH_A)),
            pl.BlockSpec((tq, WIDTH_B), lambda b, j: (b * n_q + j, COL_QB // WIDTH_B)),
            pl.BlockSpec((seq, KV_WIDTH_B), lambda b, j: (b, COL_KB // KV_WIDTH_B)),
            pl.BlockSpec((seq, KV_WIDTH_B), lambda b, j: (b, COL_VB // KV_WIDTH_B)),
            pl.BlockSpec(bias_a.shape, lambda b, j: (0, 0, 0)),
            pl.BlockSpec(bias_b.shape, lambda b, j: (0, 0, 0)),
            pl.BlockSpec(gn_a.shape, lambda b, j: (0, 0)),
            pl.BlockSpec(gn_b.shape, lambda b, j: (0, 0)),
        ],
        out_specs=pl.BlockSpec((tq, WIDTH_A + WIDTH_B), lambda b, j: (b * n_q + j, 0)),
        out_shape=jax.ShapeDtypeStruct((n_seq * seq, WIDTH_A + WIDTH_B), BF16),
        scratch_shapes=[vmem((PAD_A + seq, WIDTH_A), BF16), vmem((PAD_A + seq, WIDTH_A), BF16),
                        vmem((PAD_B + seq, KV_WIDTH_B), BF16), vmem((PAD_B + seq, KV_WIDTH_B), BF16)],
        compiler_params=_cparams(("arbitrary", "arbitrary")),
        name="attn_prompt",
    )(sinks, h, h, h, h, h, h, bias_a, bias_b, gn_a, gn_b)


def _attn_sample_kernel(sink_ref, qa_ref, ka_ref, va_ref, qb_ref, kb_ref, vb_ref,
                        cka_ref, cva_ref, ckb_ref, cvb_ref,
                        bac_ref, ban_ref, bbc_ref, bbn_ref, gna_ref, gnb_ref, o_ref, *, n_dec, dec_seq):
    b = pl.program_id(0)

    @pl.when(b >= n_dec)
    def _():
        o_ref[...] = jnp.zeros(o_ref.shape, o_ref.dtype)

    @pl.when(b < n_dec)
    def _():
        lane_lo = _head_pair_lane_mask(dec_seq)
        scale = HEAD_DIM ** -0.5
        rows = slice(0, dec_seq)

        pairs_a = []
        for p in range(N_HEADS_A // 2):
            lanes = slice(p * 2 * HEAD_DIM, (p + 1) * 2 * HEAD_DIM)
            q2 = qa_ref[:, lanes] * scale
            kc = cka_ref[0, :, lanes].astype(BF16)
            vc = cva_ref[0, :, lanes].astype(BF16)
            kn = ka_ref[:, lanes]
            vn = va_ref[:, lanes]
            halves = []
            for half in range(2):
                h = 2 * p + half
                keep = lane_lo if half == 0 else jnp.logical_not(lane_lo)
                qm = jnp.where(keep, q2, jnp.zeros_like(q2))
                s_c = _nt_dot(qm, kc) + bac_ref[h]
                s_n = _nt_dot(qm, kn) + ban_ref[h]
                halves.append(_softmax_pv([s_c, s_n], [vc, vn]))
            pairs_a.append(jnp.where(lane_lo, halves[0], halves[1]))
        _rms_store(o_ref, rows, 0, pairs_a, gna_ref)

        kc = ckb_ref[0].astype(BF16)
        vc = cvb_ref[0].astype(BF16)
        kn = kb_ref[...]
        vn = vb_ref[...]
        pairs_b = []
        for p in range(N_HEADS_B // 2):
            lanes = slice(p * 2 * HEAD_DIM, (p + 1) * 2 * HEAD_DIM)
            q2 = qb_ref[:, lanes] * scale
            halves = []
            for half in range(2):
                h = 2 * p + half
                kv_head = h // (N_HEADS_B // N_KV_B)
                keep = lane_lo if half == 0 else jnp.logical_not(lane_lo)
                qm = jnp.where(keep, q2, jnp.zeros_like(q2))
                if kv_head != half:
                    qm = jnp.concatenate([qm[:, HEAD_DIM:], qm[:, :HEAD_DIM]], axis=1)
                s_c = _nt_dot(qm, kc) + bbc_ref[h]
                s_n = _nt_dot(qm, kn) + bbn_ref[h]
                o = _softmax_pv([s_c, s_n], [vc, vn], sink=sink_ref[h])
                if kv_head != half:
                    o = jnp.concatenate([o[:, HEAD_DIM:], o[:, :HEAD_DIM]], axis=1)
                halves.append(o)
            pairs_b.append(jnp.where(lane_lo, halves[0], halves[1]))
        _rms_store(o_ref, rows, WIDTH_A, pairs_b, gnb_ref)


def _attn_sample(h, sinks, cka, cva, ckb, cvb, bias, gn_a, gn_b, *, first_row, n_rows_out, n_dec, dec_seq):
    bac, ban, bbc, bbn = bias
    n_steps = n_rows_out // dec_seq
    rb = first_row // dec_seq
    last = n_dec - 1

    def hrow(b):
        return rb + jnp.minimum(b, last)

    def cache_spec(c):
        return pl.BlockSpec((1,) + c.shape[1:], lambda b: (jnp.minimum(b, last), 0, 0))

    def const_spec(a):
        return pl.BlockSpec(a.shape, lambda b: (0,) * a.ndim)

    return pl.pallas_call(
        functools.partial(_attn_sample_kernel, n_dec=n_dec, dec_seq=dec_seq),
        grid=(n_steps,),
        in_specs=[
            pl.BlockSpec(memory_space=pltpu.SMEM),
            pl.BlockSpec((dec_seq, WIDTH_A), lambda b: (hrow(b), COL_QA // WIDTH_A)),
            pl.BlockSpec((dec_seq, WIDTH_A), lambda b: (hrow(b), COL_KA // WIDTH_A)),
            pl.BlockSpec((dec_seq, WIDTH_A), lambda b: (hrow(b), COL_VA // WIDTH_A)),
            pl.BlockSpec((dec_seq, WIDTH_B), lambda b: (hrow(b), COL_QB // WIDTH_B)),
            pl.BlockSpec((dec_seq, KV_WIDTH_B), lambda b: (hrow(b), COL_KB // KV_WIDTH_B)),
            pl.BlockSpec((dec_seq, KV_WIDTH_B), lambda b: (hrow(b), COL_VB // KV_WIDTH_B)),
            cache_spec(cka), cache_spec(cva), cache_spec(ckb), cache_spec(cvb),
            const_spec(bac), const_spec(ban), const_spec(bbc), const_spec(bbn),
            const_spec(gn_a), const_spec(gn_b),
        ],
        out_specs=pl.BlockSpec((dec_seq, WIDTH_A + WIDTH_B), lambda b: (b, 0)),
        out_shape=jax.ShapeDtypeStruct((n_rows_out, WIDTH_A + WIDTH_B), BF16),
        compiler_params=_cparams(("arbitrary",)),
        name="attn_sample",
    )(sinks, h, h, h, h, h, h, cka, cva, ckb, cvb, bac, ban, bbc, bbn, gn_a, gn_b)


def _layer_norm(r, g, b):
    mu = r.mean(axis=-1, keepdims=True)
    c = r - mu
    var = (c * c).mean(axis=-1, keepdims=True)
    return c * lax.rsqrt(var + LN_EPS) * g + b


def _route_rows(logits):
    rows = [logits[e:e + 1, :] for e in range(N_EXPERTS)]

    def top2_sum(a, b, c, d):
        hi1, lo1 = jnp.maximum(a, b), jnp.minimum(a, b)
        hi2, lo2 = jnp.maximum(c, d), jnp.minimum(c, d)
        return jnp.maximum(hi1, hi2) + jnp.maximum(jnp.minimum(hi1, hi2), jnp.maximum(lo1, lo2))

    scores = [top2_sum(*rows[4 * g:4 * g + 4]) for g in range(N_GROUPS)]
    best = scores[0]
    g_sel = jnp.zeros(best.shape, jnp.int32)
    for g in range(1, N_GROUPS):
        upd = scores[g] > best
        best = jnp.where(upd, scores[g], best)
        g_sel = jnp.where(upd, g, g_sel)

    vals = []
    for k in range(EXPERTS_PER_GROUP):
        v = rows[k]
        for g in range(1, N_GROUPS):
            v = jnp.where(g_sel == g, rows[4 * g + k], v)
        vals.append(v)

    v1 = vals[0]
    i1 = jnp.zeros(v1.shape, jnp.int32)
    for k in range(1, EXPERTS_PER_GROUP):
        upd = vals[k] > v1
        v1 = jnp.where(upd, vals[k], v1)
        i1 = jnp.where(upd, k, i1)
    v2 = jnp.full(v1.shape, NEG_INF, F32)
    i2 = jnp.zeros(v1.shape, jnp.int32)
    for k in range(EXPERTS_PER_GROUP):
        upd = jnp.logical_and(i1 != k, vals[k] > v2)
        v2 = jnp.where(upd, vals[k], v2)
        i2 = jnp.where(upd, k, i2)

    e = jnp.exp(v2 - v1)
    den = 1.0 + e
    w1 = 1.0 / den
    w2 = e / den
    first_is_lo = i1 < i2
    lo = jnp.where(first_is_lo, i1, i2)
    hi = jnp.where(first_is_lo, i2, i1)
    gate_lo = jnp.where(first_is_lo, w1, w2)
    gate_hi = jnp.where(first_is_lo, w2, w1)
    pair_base = jnp.where(lo == 0, 0, jnp.where(lo == 1, 3, 5))
    cls = g_sel * PAIRS_PER_GROUP + pair_base + hi - lo - 1
    return cls, gate_lo, gate_hi


def _outproj_kernel(ap_ref, as_ref, x_ref, w_ref, g_ref, b_ref, wr_ref, br_ref,
                    x1_ref, cls_ref, *, n_prompt_tiles, alpha, d):
    i = pl.program_id(0)
    a = jnp.where(i < n_prompt_tiles, ap_ref[...], as_ref[...])
    y = jnp.dot(a, w_ref[...], preferred_element_type=F32)
    x1 = _layer_norm(alpha * x_ref[...] + y, g_ref[...], b_ref[...])
    x1_ref[:, :d] = x1
    logits = lax.dot_general(wr_ref[...], x1, (((1,), (1,)), ((), ())),
                             precision=lax.Precision.HIGHEST, preferred_element_type=F32) + br_ref[...]
    cls, gate_lo, gate_hi = _route_rows(logits)
    cls_ref[...] = cls
    n = x1.shape[0]
    gates = jnp.concatenate([gate_lo, gate_hi, jnp.zeros((GATE_LANES - 2, n), F32)], axis=0)
    x1_ref[:, d:] = gates.T


def _outproj(a_prompt, a_sample, x, w_out, g, b, wr_t, br, *, n_prompt_tiles, alpha):
    t_pad, d = x.shape
    n_tiles = t_pad // TM
    last_p = n_prompt_tiles - 1
    const = lambda i: (0, 0)
    return pl.pallas_call(
        functools.partial(_outproj_kernel, n_prompt_tiles=n_prompt_tiles, alpha=alpha, d=d),
        grid=(n_tiles,),
        in_specs=[pl.BlockSpec((TM, d), lambda i: (jnp.minimum(i, last_p), 0)),
                  pl.BlockSpec((TM, d), lambda i: (jnp.maximum(i - n_prompt_tiles, 0), 0)),
                  pl.BlockSpec((TM, d), lambda i: (i, 0)),
                  pl.BlockSpec(w_out.shape, const),
                  pl.BlockSpec(g.shape, const), pl.BlockSpec(b.shape, const),
                  pl.BlockSpec(wr_t.shape, const), pl.BlockSpec(br.shape, const)],
        out_specs=[pl.BlockSpec((TM, d + GATE_LANES), lambda i: (i, 0)),
                   pl.BlockSpec((1, TM), lambda i: (0, i))],
        out_shape=[jax.ShapeDtypeStruct((t_pad, d + GATE_LANES), F32),
                   jax.ShapeDtypeStruct((1, t_pad), jnp.int32)],
        compiler_params=_cparams(("arbitrary",)),
        name="outproj_ln_route",
    )(a_prompt, a_sample, x, w_out, g, b, wr_t, br)


def _slots_kernel(cls_ref, pos_ref, ea_ref, eb_ref, nu_ref, *, rows_pad):
    cls = cls_ref[...]
    rows = cls.shape[0]
    r_i = lax.broadcasted_iota(jnp.int32, (rows_pad, rows_pad), 0)
    c_i = lax.broadcasted_iota(jnp.int32, (rows_pad, rows_pad), 1)
    lower = jnp.where(c_i < r_i, 1.0, 0.0).astype(BF16)
    l_r = lax.broadcasted_iota(jnp.int32, (V7X_LANES, V7X_LANES), 0)
    l_c = lax.broadcasted_iota(jnp.int32, (V7X_LANES, V7X_LANES), 1)
    upper = jnp.where(l_r < l_c, 1.0, 0.0).astype(BF16)

    n_t = ea_ref.shape[1]
    tile_start = lax.broadcasted_iota(jnp.int32, (1, n_t), 1).astype(F32) * float(TM)
    tile_cls = jnp.zeros((1, n_t), jnp.int32)
    off = jnp.zeros((1, 1), F32)
    pos = jnp.zeros((rows, V7X_LANES), F32)
    for c in range(N_CLASSES):
        m = jnp.where(cls == c, 1.0, 0.0)
        mb = m.astype(BF16)
        if rows_pad > rows:
            mb_pad = jnp.concatenate([mb, jnp.zeros((rows_pad - rows, V7X_LANES), BF16)], axis=0)
        else:
            mb_pad = mb
        before_rows = jnp.dot(lower, mb_pad, preferred_element_type=F32)[:rows]
        before_rows = before_rows.sum(axis=-1, keepdims=True)
        before_lanes = jnp.dot(mb, upper, preferred_element_type=F32)
        pos = pos + m * (off + before_rows + before_lanes)
        if c > 0:
            tile_cls = tile_cls + jnp.where(tile_start >= off, 1, 0)
        count = m.sum(axis=-1, keepdims=True).sum(axis=0, keepdims=True)
        off = off + jnp.floor((count + float(TM - 1)) * (1.0 / TM)) * float(TM)
    pos_ref[...] = pos.astype(jnp.int32)
    grp = (jnp.where(tile_cls >= 6, 1, 0) + jnp.where(tile_cls >= 12, 1, 0)
           + jnp.where(tile_cls >= 18, 1, 0))
    pair = tile_cls - PAIRS_PER_GROUP * grp
    lo = jnp.where(pair >= 3, 1, 0) + jnp.where(pair >= 5, 1, 0)
    hi = pair + 1 - 2 * jnp.where(pair >= 3, 1, 0) - jnp.where(pair >= 5, 1, 0)
    ea_ref[...] = EXPERTS_PER_GROUP * grp + lo
    eb_ref[...] = EXPERTS_PER_GROUP * grp + hi
    nu_ref[...] = jnp.broadcast_to(off * (1.0 / TM), nu_ref.shape).astype(jnp.int32)


def _slots(cls2d, *, n_slot_tiles):
    rows = cls2d.shape[0]
    rows_pad = -(-rows // V7X_LANES) * V7X_LANES
    n_t = -(-n_slot_tiles // V7X_LANES) * V7X_LANES
    i32 = jnp.int32
    return pl.pallas_call(
        functools.partial(_slots_kernel, rows_pad=rows_pad),
        out_shape=[jax.ShapeDtypeStruct(cls2d.shape, i32),
                   jax.ShapeDtypeStruct((1, n_t), i32), jax.ShapeDtypeStruct((1, n_t), i32),
                   jax.ShapeDtypeStruct((1, V7X_LANES), i32)],
        compiler_params=pltpu.CompilerParams(vmem_limit_bytes=V7X_VMEM_LIMIT),
        name="moe_slots",
    )(cls2d)


def _row_copy(src_hbm, dst_hbm, src_row, dst_row, sem):
    return pltpu.make_async_copy(src_hbm.at[pl.ds(src_row, 1)], dst_hbm.at[pl.ds(dst_row, 1)], sem)


def _dispatch_kernel(pos_ref, x_hbm, xs_in_hbm, xs_hbm, sem):
    del xs_in_hbm
    base = pl.program_id(0) * TM

    def start(t, carry):
        _row_copy(x_hbm, xs_hbm, base + t, pos_ref[0, 0, t], sem).start()
        return carry

    lax.fori_loop(0, TM, start, 0)

    def wait(t, carry):
        _row_copy(x_hbm, xs_hbm, 0, 0, sem).wait()
        return carry

    lax.fori_loop(0, TM, wait, 0)


def _dispatch(pos3, x1g, xs_zero):
    n_tiles = pos3.shape[0]
    any_spec = pl.BlockSpec(memory_space=pl.ANY)
    return pl.pallas_call(
        _dispatch_kernel,
        grid=(n_tiles,),
        in_specs=[pl.BlockSpec((1, 1, TM), lambda i: (i, 0, 0), memory_space=pltpu.SMEM),
                  any_spec, any_spec],
        out_specs=any_spec,
        out_shape=jax.ShapeDtypeStruct(xs_zero.shape, xs_zero.dtype),
        scratch_shapes=[pltpu.SemaphoreType.DMA(())],
        input_output_aliases={2: 0},
        compiler_params=pltpu.CompilerParams(dimension_semantics=("arbitrary",), has_side_effects=True),
        name="moe_dispatch",
    )(pos3, x1g, xs_zero)


def _moe_kernel(ea_ref, eb_ref, nu_ref, xs_ref, wga, wua, wda, wgb, wub, wdb, ys_ref, *, d):
    del ea_ref, eb_ref
    used = pl.program_id(0) < nu_ref[0]

    @pl.when(jnp.logical_not(used))
    def _():
        ys_ref[...] = jnp.zeros(ys_ref.shape, ys_ref.dtype)

    @pl.when(used)
    def _():
        x = xs_ref[:, :d].astype(BF16)
        gate_a = xs_ref[:, d:d + 1]
        gate_b = xs_ref[:, d + 1:d + 2]

        def expert(wg, wu, wd):
            g = jnp.dot(x, wg[0], preferred_element_type=F32)
            u = jnp.dot(x, wu[0], preferred_element_type=F32)
            h = (g * jax.nn.sigmoid(g)) * u
            return jnp.dot(h.astype(BF16), wd[0], preferred_element_type=F32)

        ys_ref[...] = gate_a * expert(wga, wua, wda) + gate_b * expert(wgb, wub, wdb)


def _moe(tile_ea, tile_eb, n_used, xs, w_gate, w_up, w_down):
    n_slots, width = xs.shape
    d = width - GATE_LANES
    n_tiles = n_slots // TM
    d_e = w_gate.shape[-1]

    def row_map(i, ea, eb, nu):
        return (jnp.minimum(i, nu[0] - 1), 0)

    def w_a(i, ea, eb, nu):
        return (ea[i], 0, 0)

    def w_b(i, ea, eb, nu):
        return (eb[i], 0, 0)

    grid_spec = pltpu.PrefetchScalarGridSpec(
        num_scalar_prefetch=3,
        grid=(n_tiles,),
        in_specs=[pl.BlockSpec((TM, width), row_map),
                  pl.BlockSpec((1, d, d_e), w_a), pl.BlockSpec((1, d, d_e), w_a),
                  pl.BlockSpec((1, d_e, d), w_a),
                  pl.BlockSpec((1, d, d_e), w_b), pl.BlockSpec((1, d, d_e), w_b),
                  pl.BlockSpec((1, d_e, d), w_b)],
        out_specs=pl.BlockSpec((TM, d), lambda i, ea, eb, nu: (i, 0)),
    )
    return pl.pallas_call(
        functools.partial(_moe_kernel, d=d),
        grid_spec=grid_spec,
        out_shape=jax.ShapeDtypeStruct((n_slots, d), F32),
        compiler_params=_cparams(("arbitrary",)),
        name="moe_experts",
    )(tile_ea, tile_eb, n_used, xs, w_gate, w_up, w_down, w_gate, w_up, w_down)


def _combine_kernel(pos_ref, ys_hbm, x1_ref, g_ref, b_ref, x2_ref, ybuf, sem, *, alpha, d):
    def start(t, carry):
        pltpu.make_async_copy(ys_hbm.at[pl.ds(pos_ref[0, 0, t], 1)], ybuf.at[pl.ds(t, 1)], sem).start()
        return carry

    lax.fori_loop(0, TM, start, 0)

    def wait(t, carry):
        pltpu.make_async_copy(ys_hbm.at[pl.ds(0, 1)], ybuf.at[pl.ds(0, 1)], sem).wait()
        return carry

    lax.fori_loop(0, TM, wait, 0)
    x2_ref[...] = _layer_norm(alpha * x1_ref[:, :d] + ybuf[...], g_ref[...], b_ref[...])


def _combine(pos3, ys, x1g, g, b, *, alpha):
    t_pad, width = x1g.shape
    d = width - GATE_LANES
    n_tiles = t_pad // TM
    const = lambda i: (0, 0)
    return pl.pallas_call(
        functools.partial(_combine_kernel, alpha=alpha, d=d),
        grid=(n_tiles,),
        in_specs=[pl.BlockSpec((1, 1, TM), lambda i: (i, 0, 0), memory_space=pltpu.SMEM),
                  pl.BlockSpec(memory_space=pl.ANY),
                  pl.BlockSpec((TM, width), lambda i: (i, 0)),
                  pl.BlockSpec(g.shape, const), pl.BlockSpec(b.shape, const)],
        out_specs=pl.BlockSpec((TM, d), lambda i: (i, 0)),
        out_shape=jax.ShapeDtypeStruct((t_pad, d), F32),
        scratch_shapes=[pltpu.VMEM((TM, d), F32), pltpu.SemaphoreType.DMA(())],
        compiler_params=_cparams(("arbitrary",)),
        name="moe_combine_ln",
    )(pos3, ys, x1g, g, b)


def _alibi_slopes():
    return 2.0 ** (-8.0 * jnp.arange(1, N_HEADS_B + 1, dtype=F32) / N_HEADS_B)


def _rel_bias_table(rel_bias, q0, n_q, n_k):
    dist = q0 + jnp.arange(n_q, dtype=jnp.int32)[:, None] - jnp.arange(n_k, dtype=jnp.int32)[None, :]
    return jnp.take(rel_bias.astype(F32), jnp.clip(dist, -REL_CLIP, REL_CLIP) + REL_CLIP, axis=1)


def _alibi_table(q0, n_q, n_k):
    dist = q0 + jnp.arange(n_q, dtype=jnp.int32)[:, None] - jnp.arange(n_k, dtype=jnp.int32)[None, :]
    return -_alibi_slopes()[:, None, None] * jnp.abs(dist).astype(F32)[None]


def kernel(x_prompt, x_sample, cache_a_k, cache_a_v, cache_b_k, cache_b_v, w_in, rel_bias, attn_sinks,
           gn_a, gn_b, w_out, ln1_g, ln1_b, w_router, b_router, w_gate, w_up, w_down, ln2_g, ln2_b):
    n_seq, seq, d = x_prompt.shape
    n_dec, dec_seq, _ = x_sample.shape
    depth = w_in.shape[0]
    rows_ca, rows_cb = cache_a_k.shape[2], cache_b_k.shape[2]
    alpha = (2.0 * depth) ** 0.25
    assert seq % TM == 0 and PAD_A == TM and n_dec * dec_seq <= TM and TM % dec_seq == 0
    assert dec_seq % 16 == 0 and seq >= PAD_A

    t_prompt = n_seq * seq
    t_real = t_prompt + n_dec * dec_seq
    t_pad = -(-t_real // (2 * TM)) * (2 * TM)
    n_prompt_tiles = t_prompt // TM
    n_slot_tiles = t_pad // TM + N_CLASSES

    x = jnp.concatenate([x_prompt.reshape(t_prompt, d), x_sample.reshape(n_dec * dec_seq, d),
                         jnp.zeros((t_pad - t_real, d), F32)], axis=0)

    w_in_b = w_in.astype(BF16)
    w_out_b = w_out.astype(BF16)
    w_gate_b, w_up_b, w_down_b = w_gate.astype(BF16), w_up.astype(BF16), w_down.astype(BF16)
    wr_t = w_router.astype(F32).T
    br = b_router.astype(F32).reshape(N_EXPERTS, 1)

    alibi_p = _alibi_table(PAD_B, CHUNK, BAND_B * CHUNK)
    alibi_s = _alibi_table(rows_cb, dec_seq, rows_cb + dec_seq)
    xs_zero = jnp.zeros((n_slot_tiles * TM, d + GATE_LANES), F32)

    pa_k, pa_v, pb_k, pb_v, sa_k, sa_v, sb_k, sb_v = ([] for _ in range(8))
    for l in range(depth):
        h, kv = _inproj(x, w_in_b[l], seq=seq, n_prompt_tiles=n_prompt_tiles)

        sinks = attn_sinks[l].astype(F32)
        gna = gn_a[l].astype(F32).reshape(1, WIDTH_A)
        gnb = gn_b[l].astype(F32).reshape(1, WIDTH_B)
        bias_pa = _rel_bias_table(rel_bias[l], PAD_A, CHUNK, BAND_A * CHUNK)
        a_prompt = _attn_prompt(h, sinks, bias_pa, alibi_p, gna, gnb, n_seq=n_seq, seq=seq)
        bias_sa = _rel_bias_table(rel_bias[l], rows_ca, dec_seq, rows_ca + dec_seq)
        bias_s = (bias_sa[:, :, :rows_ca], bias_sa[:, :, rows_ca:],
                  alibi_s[:, :, :rows_cb], alibi_s[:, :, rows_cb:])
        a_sample = _attn_sample(
            h, sinks,
            cache_a_k[l].reshape(n_dec, rows_ca, WIDTH_A), cache_a_v[l].reshape(n_dec, rows_ca, WIDTH_A),
            cache_b_k[l].reshape(n_dec, rows_cb, KV_WIDTH_B), cache_b_v[l].reshape(n_dec, rows_cb, KV_WIDTH_B),
            bias_s, gna, gnb, first_row=t_prompt, n_rows_out=t_pad - t_prompt, n_dec=n_dec, dec_seq=dec_seq)

        x1g, cls = _outproj(a_prompt, a_sample, x, w_out_b[l],
                            ln1_g[l].astype(F32).reshape(1, d), ln1_b[l].astype(F32).reshape(1, d),
                            wr_t, br, n_prompt_tiles=n_prompt_tiles, alpha=alpha)
        pos, tile_ea, tile_eb, n_used = _slots(cls.reshape(t_pad // V7X_LANES, V7X_LANES),
                                               n_slot_tiles=n_slot_tiles)
        pos3 = pos.reshape(t_pad // TM, 1, TM)
        xs = _dispatch(pos3, x1g, xs_zero)
        ys = _moe(tile_ea[0, :n_slot_tiles], tile_eb[0, :n_slot_tiles], n_used[0, :1],
                  xs, w_gate_b[l], w_up_b[l], w_down_b[l])
        x = _combine(pos3, ys, x1g, ln2_g[l].astype(F32).reshape(1, d),
                     ln2_b[l].astype(F32).reshape(1, d), alpha=alpha)

        kv_p = kv[:n_seq * TM].reshape(n_seq, TM, KV_OUT)
        kv_s = kv[n_seq * TM:n_seq * TM + n_dec * dec_seq].reshape(n_dec, dec_seq, KV_OUT)
        ra, rb = min(PAD_A, seq), min(WINDOW_B, seq)
        pa_k.append(kv_p[:, TM - ra:, :WIDTH_A].reshape(n_seq, ra, N_HEADS_A, HEAD_DIM))
        pa_v.append(kv_p[:, TM - ra:, WIDTH_A:2 * WIDTH_A].reshape(n_seq, ra, N_HEADS_A, HEAD_DIM))
        pb_k.append(kv_p[:, TM - rb:, 2 * WIDTH_A:2 * WIDTH_A + KV_WIDTH_B].reshape(n_seq, rb, N_KV_B, HEAD_DIM))
        pb_v.append(kv_p[:, TM - rb:, 2 * WIDTH_A + KV_WIDTH_B:].reshape(n_seq, rb, N_KV_B, HEAD_DIM))
        sa_k.append(kv_s[:, :, :WIDTH_A].reshape(n_dec, dec_seq, N_HEADS_A, HEAD_DIM))
        sa_v.append(kv_s[:, :, WIDTH_A:2 * WIDTH_A].reshape(n_dec, dec_seq, N_HEADS_A, HEAD_DIM))
        sb_k.append(kv_s[:, :, 2 * WIDTH_A:2 * WIDTH_A + KV_WIDTH_B].reshape(n_dec, dec_seq, N_KV_B, HEAD_DIM))
        sb_v.append(kv_s[:, :, 2 * WIDTH_A + KV_WIDTH_B:].reshape(n_dec, dec_seq, N_KV_B, HEAD_DIM))

    y_prompt = x[:t_prompt].reshape(n_seq, seq, d)
    y_sample = x[t_prompt:t_real].reshape(n_dec, dec_seq, d)
    return (y_prompt, y_sample,
            jnp.stack(pa_k), jnp.stack(pa_v), jnp.stack(pb_k), jnp.stack(pb_v),
            jnp.stack(sa_k), jnp.stack(sa_v), jnp.stack(sb_k), jnp.stack(sb_v))
```

```python
import functools

import jax
import jax.numpy as jnp
import numpy as np
from jax import lax
from jax.experimental import pallas as pl
from jax.experimental.pallas import tpu as pltpu

CHUNK = 64
HEAD_DIM = 64
N_HEADS_A = 8
BAND_A = 9
REL_CLIP = 128
N_HEADS_B = 8
N_KV_B = 2
WINDOW_B = 128
BAND_B = 1 + WINDOW_B // CHUNK
N_EXPERTS = 16
N_GROUPS = 4
EXPERTS_PER_GROUP = N_EXPERTS // N_GROUPS
PAIRS_PER_GROUP = 6
N_CLASSES = N_GROUPS * PAIRS_PER_GROUP
LN_EPS = 1e-5
RMS_EPS = 1e-6

WIDTH_A = N_HEADS_A * HEAD_DIM
WIDTH_B = N_HEADS_B * HEAD_DIM
KV_WIDTH_B = N_KV_B * HEAD_DIM
D_IN = 3 * WIDTH_A + WIDTH_B + 2 * KV_WIDTH_B
COL_QA, COL_KA, COL_VA = 0, WIDTH_A, 2 * WIDTH_A
COL_QB = 3 * WIDTH_A
COL_KB = COL_QB + WIDTH_B
COL_VB = COL_KB + KV_WIDTH_B
KV_OUT = 2 * WIDTH_A + 2 * KV_WIDTH_B
PAD_A = (BAND_A - 1) * CHUNK
PAD_B = WINDOW_B

V7X_LANES = 128
TM = 512
GATE_LANES = V7X_LANES
V7X_VMEM_LIMIT = 56 * 1024 * 1024

F32 = jnp.float32
BF16 = jnp.bfloat16
NEG_INF = float("-inf")


def _cparams(sem):
    return pltpu.CompilerParams(dimension_semantics=sem, vmem_limit_bytes=V7X_VMEM_LIMIT)


def _inproj_kernel(x_ref, w_ref, h_ref, kv_ref, *, tiles_per_seq, n_prompt_tiles):
    i = pl.program_id(0)
    acc = jnp.dot(x_ref[...].astype(BF16), w_ref[...], preferred_element_type=F32)
    h_ref[...] = acc.astype(BF16)
    keeps_kv = jnp.logical_or(i % tiles_per_seq == tiles_per_seq - 1, i >= n_prompt_tiles)

    @pl.when(keeps_kv)
    def _():
        kv_ref[:, : 2 * WIDTH_A] = acc[:, COL_KA:COL_QB]
        kv_ref[:, 2 * WIDTH_A:] = acc[:, COL_KB:]


def _inproj(x, w_in, *, seq, n_prompt_tiles):
    t_pad, d = x.shape
    n_tiles = t_pad // TM
    tiles_per_seq = seq // TM
    n_seq = n_prompt_tiles // tiles_per_seq
    n_kv_blocks = n_seq + (n_tiles - n_prompt_tiles)

    def kv_map(i):
        return (jnp.where(i < n_prompt_tiles, i // tiles_per_seq, n_seq + i - n_prompt_tiles), 0)

    return pl.pallas_call(
        functools.partial(_inproj_kernel, tiles_per_seq=tiles_per_seq, n_prompt_tiles=n_prompt_tiles),
        grid=(n_tiles,),
        in_specs=[pl.BlockSpec((TM, d), lambda i: (i, 0)),
                  pl.BlockSpec((d, D_IN), lambda i: (0, 0))],
        out_specs=[pl.BlockSpec((TM, D_IN), lambda i: (i, 0)),
                   pl.BlockSpec((TM, KV_OUT), kv_map)],
        out_shape=[jax.ShapeDtypeStruct((t_pad, D_IN), BF16),
                   jax.ShapeDtypeStruct((n_kv_blocks * TM, KV_OUT), F32)],
        compiler_params=_cparams(("arbitrary",)),
        name="inproj",
    )(x, w_in)


def _softmax_pv(s_parts, v_parts, sink=None):
    m = s_parts[0].max(axis=-1, keepdims=True)
    for s in s_parts[1:]:
        m = jnp.maximum(m, s.max(axis=-1, keepdims=True))
    if sink is not None:
        m = jnp.maximum(m, sink)
    denom = None
    acc = None
    for s, v in zip(s_parts, v_parts):
        e = jnp.exp(s - m)
        d = e.sum(axis=-1, keepdims=True)
        pv = jnp.dot(e.astype(BF16), v, preferred_element_type=F32)
        denom = d if denom is None else denom + d
        acc = pv if acc is None else acc + pv
    if sink is not None:
        denom = denom + jnp.exp(sink - m)
    return acc / denom


def _nt_dot(a, b):
    return lax.dot_general(a, b, (((1,), (1,)), ((), ())), preferred_element_type=F32)


def _head_pair_lane_mask(rows):
    return lax.broadcasted_iota(jnp.int32, (rows, 2 * HEAD_DIM), 1) < HEAD_DIM


def _rms_store(o_ref, row_slice, col0, pairs, gn_ref):
    ssq = None
    for o in pairs:
        s = (o * o).sum(axis=-1, keepdims=True)
        ssq = s if ssq is None else ssq + s
    width = len(pairs) * 2 * HEAD_DIM
    inv = lax.rsqrt(ssq / width + RMS_EPS)
    for p, o in enumerate(pairs):
        c0 = p * 2 * HEAD_DIM
        g = gn_ref[:, c0:c0 + 2 * HEAD_DIM]
        o_ref[row_slice, col0 + c0:col0 + c0 + 2 * HEAD_DIM] = (o * inv * g).astype(o_ref.dtype)


KEY_BLOCK = V7X_LANES


def _swap_halves(x):
    return jnp.concatenate([x[:, HEAD_DIM:], x[:, :HEAD_DIM]], axis=1)


def _pair_rows(q2, lane_lo, kv_half=None):
    hi_lanes = jnp.logical_not(lane_lo)
    zero = jnp.zeros_like(q2)
    if kv_half is None:
        lo, hi = jnp.where(lane_lo, q2, zero), jnp.where(hi_lanes, q2, zero)
    elif kv_half == 0:
        lo, hi = jnp.where(lane_lo, q2, zero), jnp.where(lane_lo, _swap_halves(q2), zero)
    else:
        lo, hi = jnp.where(hi_lanes, _swap_halves(q2), zero), jnp.where(hi_lanes, q2, zero)
    return jnp.concatenate([lo, hi], axis=0)


def _tn_dot(a, b):
    return lax.dot_general(a, b, (((0,), (0,)), ((), ())), preferred_element_type=F32)


def _band_probs(w, k_ref, lanes, row0, n_keys, bias_ref, pair, pad, masked, sink_block=None):
    blocks = []
    for off in range(0, n_keys, KEY_BLOCK):
        n = min(KEY_BLOCK, n_keys - off)
        start = pl.multiple_of(row0 + off, CHUNK)
        s = _nt_dot(k_ref[pl.ds(start, n), lanes], w) + bias_ref[pair, off:off + n, :]
        if masked:
            key_row = lax.broadcasted_iota(jnp.int32, s.shape, 0) + start
            s = jnp.where(key_row >= pad, s, NEG_INF)
        blocks.append((s, start, n))
    m = None
    for s, _, _ in blocks:
        bm = s.max(axis=0, keepdims=True)
        m = bm if m is None else jnp.maximum(m, bm)
    if sink_block is not None:
        m = jnp.maximum(m, sink_block.max(axis=0, keepdims=True))
    probs = [(jnp.exp(s - m).astype(BF16), start, n) for s, start, n in blocks]
    sink_p = None if sink_block is None else jnp.exp(sink_block - m).astype(BF16)
    return probs, sink_p


def _band_values(probs, sink_p, v_ref, lanes):
    acc = None
    for p, start, n in probs:
        v1 = jnp.concatenate([v_ref[pl.ds(start, n), lanes], jnp.ones((n, V7X_LANES), BF16)], axis=1)
        c = _tn_dot(p, v1)
        acc = c if acc is None else acc + c
    if sink_p is not None:
        n = sink_p.shape[0]
        v1 = jnp.concatenate([jnp.zeros((n, V7X_LANES), BF16), jnp.ones((n, V7X_LANES), BF16)], axis=1)
        acc = acc + _tn_dot(sink_p, v1)
    return acc[:, :V7X_LANES] / acc[:, V7X_LANES:]


def _attn_prompt_kernel(qa_ref, ka_ref, va_ref, qb_ref, kb_ref, vb_ref,
                        ba_ref, bb_ref, sink_ref, gna_ref, gnb_ref, o_ref,
                        kpa, vpa, kpb, vpb, *, seq, tq):
    j = pl.program_id(1)

    @pl.when(j == 0)
    def _():
        kpa[0:PAD_A, :] = jnp.zeros((PAD_A, WIDTH_A), BF16)
        vpa[0:PAD_A, :] = jnp.zeros((PAD_A, WIDTH_A), BF16)
        kpb[0:PAD_B, :] = jnp.zeros((PAD_B, KV_WIDTH_B), BF16)
        vpb[0:PAD_B, :] = jnp.zeros((PAD_B, KV_WIDTH_B), BF16)
        kpa[PAD_A:PAD_A + seq, :] = ka_ref[...]
        vpa[PAD_A:PAD_A + seq, :] = va_ref[...]
        kpb[PAD_B:PAD_B + seq, :] = kb_ref[...]
        vpb[PAD_B:PAD_B + seq, :] = vb_ref[...]

    lane_lo = _head_pair_lane_mask(CHUNK)
    scale = HEAD_DIM ** -0.5
    all_lanes = slice(0, KV_WIDTH_B)

    def make_chunk_body(masked):
        def chunk_body(c, carry):
            r0 = pl.multiple_of(c * CHUNK, CHUNK)
            cs = pl.multiple_of(j * tq + c * CHUNK, CHUNK)
            rows = pl.ds(r0, CHUNK)

            probs_a, probs_b = [], []
            for p in range(N_HEADS_A // 2):
                lanes = slice(p * 2 * HEAD_DIM, (p + 1) * 2 * HEAD_DIM)
                w = _pair_rows(qa_ref[rows, lanes] * scale, lane_lo)
                probs_a.append(_band_probs(w, kpa, lanes, cs, BAND_A * CHUNK, ba_ref, p, PAD_A, masked))
            for p in range(N_HEADS_B // 2):
                lanes = slice(p * 2 * HEAD_DIM, (p + 1) * 2 * HEAD_DIM)
                kv_half = (2 * p) // (N_HEADS_B // N_KV_B)
                w = _pair_rows(qb_ref[rows, lanes] * scale, lane_lo, kv_half=kv_half)
                probs_b.append(_band_probs(w, kpb, all_lanes, cs, BAND_B * CHUNK, bb_ref, p, PAD_B, masked,
                                           sink_block=sink_ref[p]))

            pairs_a = []
            for p in range(N_HEADS_A // 2):
                lanes = slice(p * 2 * HEAD_DIM, (p + 1) * 2 * HEAD_DIM)
                o = _band_values(*probs_a[p], vpa, lanes)
                pairs_a.append(jnp.where(lane_lo, o[:CHUNK], o[CHUNK:]))
            _rms_store(o_ref, rows, 0, pairs_a, gna_ref)

            pairs_b = []
            for p in range(N_HEADS_B // 2):
                kv_half = (2 * p) // (N_HEADS_B // N_KV_B)
                o = _band_values(*probs_b[p], vpb, all_lanes)
                lo, hi = o[:CHUNK], o[CHUNK:]
                if kv_half == 0:
                    hi = _swap_halves(hi)
                else:
                    lo = _swap_halves(lo)
                pairs_b.append(jnp.where(lane_lo, lo, hi))
            _rms_store(o_ref, rows, WIDTH_A, pairs_b, gnb_ref)
            return carry
        return chunk_body

    @pl.when(j == 0)
    def _():
        lax.fori_loop(0, tq // CHUNK, make_chunk_body(True), 0)

    @pl.when(j > 0)
    def _():
        lax.fori_loop(0, tq // CHUNK, make_chunk_body(False), 0)


def _pair_transposed(table):
    n_h, n_q, n_k = table.shape
    return table.reshape(n_h // 2, 2, n_q, n_k).transpose(0, 3, 1, 2).reshape(n_h // 2, n_k, 2 * n_q)


def _sink_blocks(sinks):
    row = jnp.repeat(sinks.astype(F32).reshape(N_HEADS_B // 2, 2), HEAD_DIM, axis=1)
    rest = jnp.full((N_HEADS_B // 2, 15, 2 * HEAD_DIM), NEG_INF, F32)
    return jnp.concatenate([row[:, None, :], rest], axis=1)


def _attn_prompt(h, sinks, bias_a, bias_b, gn_a, gn_b, *, n_seq, seq):
    tq = TM
    assert tq >= PAD_A and tq >= PAD_B and tq % CHUNK == 0
    n_q = seq // tq
    vmem = pltpu.VMEM
    bias_a, bias_b = _pair_transposed(bias_a), _pair_transposed(bias_b)
    sinks = _sink_blocks(sinks)
    return pl.pallas_call(
        functools.partial(_attn_prompt_kernel, seq=seq, tq=tq),
        grid=(n_seq, n_q),
        in_specs=[
            pl.BlockSpec((tq, WIDTH_A), lambda b, j: (b * n_q + j, COL_QA // WIDTH_A)),
            pl.BlockSpec((seq, WIDTH_A), lambda b, j: (b, COL_KA // WIDTH_A)),
            pl.BlockSpec((seq, WIDTH_A), lambda b, j: (b, COL_VA // WIDTH_A)),
            pl.BlockSpec((tq, WIDTH_B), lambda b, j: (b * n_q + j, COL_QB // WIDTH_B)),
            pl.BlockSpec((seq, KV_WIDTH_B), lambda b, j: (b, COL_KB // KV_WIDTH_B)),
            pl.BlockSpec((seq, KV_WIDTH_B), lambda b, j: (b, COL_VB // KV_WIDTH_B)),
            pl.BlockSpec(bias_a.shape, lambda b, j: (0, 0, 0)),
            pl.BlockSpec(bias_b.shape, lambda b, j: (0, 0, 0)),
            pl.BlockSpec(sinks.shape, lambda b, j: (0, 0, 0)),
            pl.BlockSpec(gn_a.shape, lambda b, j: (0, 0)),
            pl.BlockSpec(gn_b.shape, lambda b, j: (0, 0)),
        ],
        out_specs=pl.BlockSpec((tq, WIDTH_A + WIDTH_B), lambda b, j: (b * n_q + j, 0)),
        out_shape=jax.ShapeDtypeStruct((n_seq * seq, WIDTH_A + WIDTH_B), BF16),
        scratch_shapes=[vmem((PAD_A + seq, WIDTH_A), BF16), vmem((PAD_A + seq, WIDTH_A), BF16),
                        vmem((PAD_B + seq, KV_WIDTH_B), BF16), vmem((PAD_B + seq, KV_WIDTH_B), BF16)],
        compiler_params=_cparams(("arbitrary", "arbitrary")),
        name="attn_prompt",
    )(h, h, h, h, h, h, bias_a, bias_b, sinks, gn_a, gn_b)


def _attn_sample_kernel(sink_ref, qa_ref, ka_ref, va_ref, qb_ref, kb_ref, vb_ref,
                        cka_ref, cva_ref, ckb_ref, cvb_ref,
                        bac_ref, ban_ref, bbc_ref, bbn_ref, gna_ref, gnb_ref, o_ref, *, n_dec, dec_seq):
    b = pl.program_id(0)

    @pl.when(b >= n_dec)
    def _():
        o_ref[...] = jnp.zeros(o_ref.shape, o_ref.dtype)

    @pl.when(b < n_dec)
    def _():
        lane_lo = _head_pair_lane_mask(dec_seq)
        scale = HEAD_DIM ** -0.5
        rows = slice(0, dec_seq)

        pairs_a = []
        for p in range(N_HEADS_A // 2):
            lanes = slice(p * 2 * HEAD_DIM, (p + 1) * 2 * HEAD_DIM)
            q2 = qa_ref[:, lanes] * scale
            kc = cka_ref[0, :, lanes].astype(BF16)
            vc = cva_ref[0, :, lanes].astype(BF16)
            kn = ka_ref[:, lanes]
            vn = va_ref[:, lanes]
            halves = []
            for half in range(2):
                h = 2 * p + half
                keep = lane_lo if half == 0 else jnp.logical_not(lane_lo)
                qm = jnp.where(keep, q2, jnp.zeros_like(q2))
                s_c = _nt_dot(qm, kc) + bac_ref[h]
                s_n = _nt_dot(qm, kn) + ban_ref[h]
                halves.append(_softmax_pv([s_c, s_n], [vc, vn]))
            pairs_a.append(jnp.where(lane_lo, halves[0], halves[1]))
        _rms_store(o_ref, rows, 0, pairs_a, gna_ref)

        kc = ckb_ref[0].astype(BF16)
        vc = cvb_ref[0].astype(BF16)
        kn = kb_ref[...]
        vn = vb_ref[...]
        pairs_b = []
        for p in range(N_HEADS_B // 2):
            lanes = slice(p * 2 * HEAD_DIM, (p + 1) * 2 * HEAD_DIM)
            q2 = qb_ref[:, lanes] * scale
            halves = []
            for half in range(2):
                h = 2 * p + half
                kv_head = h // (N_HEADS_B // N_KV_B)
                keep = lane_lo if half == 0 else jnp.logical_not(lane_lo)
                qm = jnp.where(keep, q2, jnp.zeros_like(q2))
                if kv_head != half:
                    qm = jnp.concatenate([qm[:, HEAD_DIM:], qm[:, :HEAD_DIM]], axis=1)
                s_c = _nt_dot(qm, kc) + bbc_ref[h]
                s_n = _nt_dot(qm, kn) + bbn_ref[h]
                o = _softmax_pv([s_c, s_n], [vc, vn], sink=sink_ref[h])
                if kv_head != half:
                    o = jnp.concatenate([o[:, HEAD_DIM:], o[:, :HEAD_DIM]], axis=1)
                halves.append(o)
            pairs_b.append(jnp.where(lane_lo, halves[0], halves[1]))
        _rms_store(o_ref, rows, WIDTH_A, pairs_b, gnb_ref)


def _attn_sample(h, sinks, cka, cva, ckb, cvb, bias, gn_a, gn_b, *, first_row, n_rows_out, n_dec, dec_seq):
    bac, ban, bbc, bbn = bias
    n_steps = n_rows_out // dec_seq
    rb = first_row // dec_seq
    last = n_dec - 1

    def hrow(b):
        return rb + jnp.minimum(b, last)

    def cache_spec(c):
        return pl.BlockSpec((1,) + c.shape[1:], lambda b: (jnp.minimum(b, last), 0, 0))

    def const_spec(a):
        return pl.BlockSpec(a.shape, lambda b: (0,) * a.ndim)

    return pl.pallas_call(
        functools.partial(_attn_sample_kernel, n_dec=n_dec, dec_seq=dec_seq),
        grid=(n_steps,),
        in_specs=[
            pl.BlockSpec(memory_space=pltpu.SMEM),
            pl.BlockSpec((dec_seq, WIDTH_A), lambda b: (hrow(b), COL_QA // WIDTH_A)),
            pl.BlockSpec((dec_seq, WIDTH_A), lambda b: (hrow(b), COL_KA // WIDTH_A)),
            pl.BlockSpec((dec_seq, WIDTH_A), lambda b: (hrow(b), COL_VA // WIDTH_A)),
            pl.BlockSpec((dec_seq, WIDTH_B), lambda b: (hrow(b), COL_QB // WIDTH_B)),
            pl.BlockSpec((dec_seq, KV_WIDTH_B), lambda b: (hrow(b), COL_KB // KV_WIDTH_B)),
            pl.BlockSpec((dec_seq, KV_WIDTH_B), lambda b: (hrow(b), COL_VB // KV_WIDTH_B)),
            cache_spec(cka), cache_spec(cva), cache_spec(ckb), cache_spec(cvb),
            const_spec(bac), const_spec(ban), const_spec(bbc), const_spec(bbn),
            const_spec(gn_a), const_spec(gn_b),
        ],
        out_specs=pl.BlockSpec((dec_seq, WIDTH_A + WIDTH_B), lambda b: (b, 0)),
        out_shape=jax.ShapeDtypeStruct((n_rows_out, WIDTH_A + WIDTH_B), BF16),
        compiler_params=_cparams(("arbitrary",)),
        name="attn_sample",
    )(sinks, h, h, h, h, h, h, cka, cva, ckb, cvb, bac, ban, bbc, bbn, gn_a, gn_b)


def _layer_norm(r, g, b):
    mu = r.mean(axis=-1, keepdims=True)
    c = r - mu
    var = (c * c).mean(axis=-1, keepdims=True)
    return c * lax.rsqrt(var + LN_EPS) * g + b


def _route_rows(logits):
    rows = [logits[e:e + 1, :] for e in range(N_EXPERTS)]

    def top2_sum(a, b, c, d):
        hi1, lo1 = jnp.maximum(a, b), jnp.minimum(a, b)
        hi2, lo2 = jnp.maximum(c, d), jnp.minimum(c, d)
        return jnp.maximum(hi1, hi2) + jnp.maximum(jnp.minimum(hi1, hi2), jnp.maximum(lo1, lo2))

    scores = [top2_sum(*rows[4 * g:4 * g + 4]) for g in range(N_GROUPS)]
    best = scores[0]
    g_sel = jnp.zeros(best.shape, jnp.int32)
    for g in range(1, N_GROUPS):
        upd = scores[g] > best
        best = jnp.where(upd, scores[g], best)
        g_sel = jnp.where(upd, g, g_sel)

    vals = []
    for k in range(EXPERTS_PER_GROUP):
        v = rows[k]
        for g in range(1, N_GROUPS):
            v = jnp.where(g_sel == g, rows[4 * g + k], v)
        vals.append(v)

    v1 = vals[0]
    i1 = jnp.zeros(v1.shape, jnp.int32)
    for k in range(1, EXPERTS_PER_GROUP):
        upd = vals[k] > v1
        v1 = jnp.where(upd, vals[k], v1)
        i1 = jnp.where(upd, k, i1)
    v2 = jnp.full(v1.shape, NEG_INF, F32)
    i2 = jnp.zeros(v1.shape, jnp.int32)
    for k in range(EXPERTS_PER_GROUP):
        upd = jnp.logical_and(i1 != k, vals[k] > v2)
        v2 = jnp.where(upd, vals[k], v2)
        i2 = jnp.where(upd, k, i2)

    e = jnp.exp(v2 - v1)
    den = 1.0 + e
    w1 = 1.0 / den
    w2 = e / den
    first_is_lo = i1 < i2
    lo = jnp.where(first_is_lo, i1, i2)
    hi = jnp.where(first_is_lo, i2, i1)
    gate_lo = jnp.where(first_is_lo, w1, w2)
    gate_hi = jnp.where(first_is_lo, w2, w1)
    pair_base = jnp.where(lo == 0, 0, jnp.where(lo == 1, 3, 5))
    cls = g_sel * PAIRS_PER_GROUP + pair_base + hi - lo - 1
    return cls, gate_lo, gate_hi


def _outproj_kernel(ap_ref, as_ref, x_ref, w_ref, g_ref, b_ref, wr_ref, br_ref,
                    x1_ref, cls_ref, *, n_prompt_tiles, alpha, d):
    i = pl.program_id(0)
    a = jnp.where(i < n_prompt_tiles, ap_ref[...], as_ref[...])
    y = jnp.dot(a, w_ref[...], preferred_element_type=F32)
    x1 = _layer_norm(alpha * x_ref[...] + y, g_ref[...], b_ref[...])
    x1_ref[:, :d] = x1
    logits = lax.dot_general(wr_ref[...], x1, (((1,), (1,)), ((), ())),
                             precision=lax.Precision.HIGHEST, preferred_element_type=F32) + br_ref[...]
    cls, gate_lo, gate_hi = _route_rows(logits)
    cls_ref[...] = cls
    n = x1.shape[0]
    gates = jnp.concatenate([gate_lo, gate_hi, jnp.zeros((GATE_LANES - 2, n), F32)], axis=0)
    x1_ref[:, d:] = gates.T


def _outproj(a_prompt, a_sample, x, w_out, g, b, wr_t, br, *, n_prompt_tiles, alpha):
    t_pad, d = x.shape
    n_tiles = t_pad // TM
    last_p = n_prompt_tiles - 1
    const = lambda i: (0, 0)
    return pl.pallas_call(
        functools.partial(_outproj_kernel, n_prompt_tiles=n_prompt_tiles, alpha=alpha, d=d),
        grid=(n_tiles,),
        in_specs=[pl.BlockSpec((TM, d), lambda i: (jnp.minimum(i, last_p), 0)),
                  pl.BlockSpec((TM, d), lambda i: (jnp.maximum(i - n_prompt_tiles, 0), 0)),
                  pl.BlockSpec((TM, d), lambda i: (i, 0)),
                  pl.BlockSpec(w_out.shape, const),
                  pl.BlockSpec(g.shape, const), pl.BlockSpec(b.shape, const),
                  pl.BlockSpec(wr_t.shape, const), pl.BlockSpec(br.shape, const)],
        out_specs=[pl.BlockSpec((TM, d + GATE_LANES), lambda i: (i, 0)),
                   pl.BlockSpec((1, TM), lambda i: (0, i))],
        out_shape=[jax.ShapeDtypeStruct((t_pad, d + GATE_LANES), F32),
                   jax.ShapeDtypeStruct((1, t_pad), jnp.int32)],
        compiler_params=_cparams(("arbitrary",)),
        name="outproj_ln_route",
    )(a_prompt, a_sample, x, w_out, g, b, wr_t, br)


def _slots_kernel(cls_ref, pos_ref, ea_ref, eb_ref, nu_ref, padlo_ref, padhi_ref, *, rows_pad):
    lane = lax.broadcasted_iota(jnp.int32, (1, V7X_LANES), 1)
    pad_lo = jnp.zeros((1, V7X_LANES), F32)
    pad_hi = jnp.zeros((1, V7X_LANES), F32)
    cls = cls_ref[...]
    rows = cls.shape[0]
    r_i = lax.broadcasted_iota(jnp.int32, (rows_pad, rows_pad), 0)
    c_i = lax.broadcasted_iota(jnp.int32, (rows_pad, rows_pad), 1)
    lower = jnp.where(c_i < r_i, 1.0, 0.0).astype(BF16)
    l_r = lax.broadcasted_iota(jnp.int32, (V7X_LANES, V7X_LANES), 0)
    l_c = lax.broadcasted_iota(jnp.int32, (V7X_LANES, V7X_LANES), 1)
    upper = jnp.where(l_r < l_c, 1.0, 0.0).astype(BF16)

    n_t = ea_ref.shape[1]
    tile_start = lax.broadcasted_iota(jnp.int32, (1, n_t), 1).astype(F32) * float(TM)
    tile_cls = jnp.zeros((1, n_t), jnp.int32)
    off = jnp.zeros((1, 1), F32)
    pos = jnp.zeros((rows, V7X_LANES), F32)
    for c in range(N_CLASSES):
        m = jnp.where(cls == c, 1.0, 0.0)
        mb = m.astype(BF16)
        if rows_pad > rows:
            mb_pad = jnp.concatenate([mb, jnp.zeros((rows_pad - rows, V7X_LANES), BF16)], axis=0)
        else:
            mb_pad = mb
        before_rows = jnp.dot(lower, mb_pad, preferred_element_type=F32)[:rows]
        before_rows = before_rows.sum(axis=-1, keepdims=True)
        before_lanes = jnp.dot(mb, upper, preferred_element_type=F32)
        pos = pos + m * (off + before_rows + before_lanes)
        if c > 0:
            tile_cls = tile_cls + jnp.where(tile_start >= off, 1, 0)
        count = m.sum(axis=-1, keepdims=True).sum(axis=0, keepdims=True)
        pad_lo = jnp.where(lane == c, off + count, pad_lo)
        off = off + jnp.floor((count + float(TM - 1)) * (1.0 / TM)) * float(TM)
        pad_hi = jnp.where(lane == c, off, pad_hi)
    pos_ref[...] = pos.astype(jnp.int32)
    padlo_ref[...] = pad_lo.astype(jnp.int32)
    padhi_ref[...] = pad_hi.astype(jnp.int32)
    grp = (jnp.where(tile_cls >= 6, 1, 0) + jnp.where(tile_cls >= 12, 1, 0)
           + jnp.where(tile_cls >= 18, 1, 0))
    pair = tile_cls - PAIRS_PER_GROUP * grp
    lo = jnp.where(pair >= 3, 1, 0) + jnp.where(pair >= 5, 1, 0)
    hi = pair + 1 - 2 * jnp.where(pair >= 3, 1, 0) - jnp.where(pair >= 5, 1, 0)
    ea_ref[...] = EXPERTS_PER_GROUP * grp + lo
    eb_ref[...] = EXPERTS_PER_GROUP * grp + hi
    nu_ref[...] = jnp.broadcast_to(off * (1.0 / TM), nu_ref.shape).astype(jnp.int32)


def _slots(cls2d, *, n_slot_tiles):
    rows = cls2d.shape[0]
    rows_pad = -(-rows // V7X_LANES) * V7X_LANES
    n_t = -(-n_slot_tiles // V7X_LANES) * V7X_LANES
    i32 = jnp.int32
    return pl.pallas_call(
        functools.partial(_slots_kernel, rows_pad=rows_pad),
        out_shape=[jax.ShapeDtypeStruct(cls2d.shape, i32),
                   jax.ShapeDtypeStruct((1, n_t), i32), jax.ShapeDtypeStruct((1, n_t), i32),
                   jax.ShapeDtypeStruct((1, V7X_LANES), i32), jax.ShapeDtypeStruct((1, V7X_LANES), i32),
                   jax.ShapeDtypeStruct((1, V7X_LANES), i32)],
        compiler_params=pltpu.CompilerParams(vmem_limit_bytes=V7X_VMEM_LIMIT),
        name="moe_slots",
    )(cls2d)


ROW_DMA_UNROLL = 8


def _row_copy(src, dst, src_row, dst_row, sem):
    return pltpu.make_async_copy(src.at[pl.ds(src_row, 1)], dst.at[pl.ds(dst_row, 1)], sem)


def _dispatch_kernel(pos_ref, padlo_ref, padhi_ref, nu_ref, x_ref, xs_hbm, zbuf, sem, *, n_slot_tiles):
    @pl.when(pl.program_id(0) == 0)
    def _():
        zbuf[...] = jnp.zeros(zbuf.shape, zbuf.dtype)

        def zero_tile(t, carry):
            cp = pltpu.make_async_copy(zbuf, xs_hbm.at[pl.ds(pl.multiple_of(t * TM, TM), TM)], sem)
            cp.start()
            cp.wait()
            return carry

        lax.fori_loop(nu_ref[0], n_slot_tiles, zero_tile, 0)

        for c in range(N_CLASSES):
            lo, hi = padlo_ref[c], padhi_ref[c]

            def zero_row(s, carry):
                _row_copy(zbuf, xs_hbm, 0, s, sem).start()
                return carry

            def zero_wait(s, carry):
                _row_copy(zbuf, xs_hbm, 0, 0, sem).wait()
                return carry

            lax.fori_loop(lo, hi, zero_row, 0)
            lax.fori_loop(lo, hi, zero_wait, 0)

    def start(t, carry):
        _row_copy(x_ref, xs_hbm, t, pos_ref[0, 0, t], sem).start()
        return carry

    lax.fori_loop(0, TM, start, 0, unroll=ROW_DMA_UNROLL)
    pltpu.make_async_copy(x_ref, xs_hbm.at[pl.ds(0, TM)], sem).wait()


def _dispatch(pos3, pad_lo, pad_hi, n_used, x1g, *, n_slot_tiles):
    n_tiles = pos3.shape[0]
    width = x1g.shape[1]
    smem = pl.BlockSpec(memory_space=pltpu.SMEM)
    return pl.pallas_call(
        functools.partial(_dispatch_kernel, n_slot_tiles=n_slot_tiles),
        grid=(n_tiles,),
        in_specs=[pl.BlockSpec((1, 1, TM), lambda i: (i, 0, 0), memory_space=pltpu.SMEM),
                  smem, smem, smem,
                  pl.BlockSpec((TM, width), lambda i: (i, 0))],
        out_specs=pl.BlockSpec(memory_space=pl.ANY),
        out_shape=jax.ShapeDtypeStruct((n_slot_tiles * TM, width), x1g.dtype),
        scratch_shapes=[pltpu.VMEM((TM, width), x1g.dtype), pltpu.SemaphoreType.DMA(())],
        compiler_params=pltpu.CompilerParams(dimension_semantics=("arbitrary",), has_side_effects=True,
                                             vmem_limit_bytes=V7X_VMEM_LIMIT),
        name="moe_dispatch",
    )(pos3, pad_lo, pad_hi, n_used, x1g)


def _moe_kernel(ea_ref, eb_ref, nu_ref, xs_ref, wga, wua, wda, wgb, wub, wdb, ys_ref, *, d):
    del ea_ref, eb_ref
    used = pl.program_id(0) < nu_ref[0]

    @pl.when(jnp.logical_not(used))
    def _():
        ys_ref[...] = jnp.zeros(ys_ref.shape, ys_ref.dtype)

    @pl.when(used)
    def _():
        x = xs_ref[:, :d].astype(BF16)
        gate_a = xs_ref[:, d:d + 1]
        gate_b = xs_ref[:, d + 1:d + 2]

        def expert(wg, wu, wd):
            g = jnp.dot(x, wg[0], preferred_element_type=F32)
            u = jnp.dot(x, wu[0], preferred_element_type=F32)
            h = (g * jax.nn.sigmoid(g)) * u
            return jnp.dot(h.astype(BF16), wd[0], preferred_element_type=F32)

        ys_ref[...] = gate_a * expert(wga, wua, wda) + gate_b * expert(wgb, wub, wdb)


def _moe(tile_ea, tile_eb, n_used, xs, w_gate, w_up, w_down):
    n_slots, width = xs.shape
    d = width - GATE_LANES
    n_tiles = n_slots // TM
    d_e = w_gate.shape[-1]

    def row_map(i, ea, eb, nu):
        return (jnp.minimum(i, nu[0] - 1), 0)

    def w_a(i, ea, eb, nu):
        return (ea[i], 0, 0)

    def w_b(i, ea, eb, nu):
        return (eb[i], 0, 0)

    grid_spec = pltpu.PrefetchScalarGridSpec(
        num_scalar_prefetch=3,
        grid=(n_tiles,),
        in_specs=[pl.BlockSpec((TM, width), row_map),
                  pl.BlockSpec((1, d, d_e), w_a), pl.BlockSpec((1, d, d_e), w_a),
                  pl.BlockSpec((1, d_e, d), w_a),
                  pl.BlockSpec((1, d, d_e), w_b), pl.BlockSpec((1, d, d_e), w_b),
                  pl.BlockSpec((1, d_e, d), w_b)],
        out_specs=pl.BlockSpec((TM, d), lambda i, ea, eb, nu: (i, 0)),
    )
    return pl.pallas_call(
        functools.partial(_moe_kernel, d=d),
        grid_spec=grid_spec,
        out_shape=jax.ShapeDtypeStruct((n_slots, d), F32),
        compiler_params=_cparams(("arbitrary",)),
        name="moe_experts",
    )(tile_ea, tile_eb, n_used, xs, w_gate, w_up, w_down, w_gate, w_up, w_down)


def _combine_kernel(pos_ref, ys_hbm, x1_ref, g_ref, b_ref, x2_ref, ybuf, sem, *, alpha, d):
    def start(t, carry):
        _row_copy(ys_hbm, ybuf, pos_ref[0, 0, t], t, sem).start()
        return carry

    lax.fori_loop(0, TM, start, 0, unroll=ROW_DMA_UNROLL)
    pltpu.make_async_copy(ys_hbm.at[pl.ds(0, TM)], ybuf, sem).wait()
    x2_ref[...] = _layer_norm(alpha * x1_ref[:, :d] + ybuf[...], g_ref[...], b_ref[...])


def _combine(pos3, ys, x1g, g, b, *, alpha):
    t_pad, width = x1g.shape
    d = width - GATE_LANES
    n_tiles = t_pad // TM
    const = lambda i: (0, 0)
    return pl.pallas_call(
        functools.partial(_combine_kernel, alpha=alpha, d=d),
        grid=(n_tiles,),
        in_specs=[pl.BlockSpec((1, 1, TM), lambda i: (i, 0, 0), memory_space=pltpu.SMEM),
                  pl.BlockSpec(memory_space=pl.ANY),
                  pl.BlockSpec((TM, width), lambda i: (i, 0)),
                  pl.BlockSpec(g.shape, const), pl.BlockSpec(b.shape, const)],
        out_specs=pl.BlockSpec((TM, d), lambda i: (i, 0)),
        out_shape=jax.ShapeDtypeStruct((t_pad, d), F32),
        scratch_shapes=[pltpu.VMEM((TM, d), F32), pltpu.SemaphoreType.DMA(())],
        compiler_params=_cparams(("arbitrary",)),
        name="moe_combine_ln",
    )(pos3, ys, x1g, g, b)


def _alibi_slopes():
    return 2.0 ** (-8.0 * jnp.arange(1, N_HEADS_B + 1, dtype=F32) / N_HEADS_B)


def _rel_bias_table(rel_bias, q0, n_q, n_k):
    n_diag = n_q + n_k - 1
    dist = q0 + (n_q - 1) - jnp.arange(n_diag, dtype=jnp.int32)
    diag = jnp.take(rel_bias.astype(F32), jnp.clip(dist, -REL_CLIP, REL_CLIP) + REL_CLIP, axis=1)
    n_h = diag.shape[0]
    ext = jnp.concatenate([diag, jnp.zeros((n_h, 1), F32)], axis=1)
    shifted = jnp.tile(ext, (1, n_q))[:, :n_q * n_diag].reshape(n_h, n_q, n_diag)
    return shifted[:, :, n_q - 1:n_q - 1 + n_k]


def _alibi_table(q0, n_q, n_k):
    dist = q0 + jnp.arange(n_q, dtype=jnp.int32)[:, None] - jnp.arange(n_k, dtype=jnp.int32)[None, :]
    return -_alibi_slopes()[:, None, None] * jnp.abs(dist).astype(F32)[None]


def kernel(x_prompt, x_sample, cache_a_k, cache_a_v, cache_b_k, cache_b_v, w_in, rel_bias, attn_sinks,
           gn_a, gn_b, w_out, ln1_g, ln1_b, w_router, b_router, w_gate, w_up, w_down, ln2_g, ln2_b):
    n_seq, seq, d = x_prompt.shape
    n_dec, dec_seq, _ = x_sample.shape
    depth = w_in.shape[0]
    rows_ca, rows_cb = cache_a_k.shape[2], cache_b_k.shape[2]
    alpha = (2.0 * depth) ** 0.25
    assert seq % TM == 0 and PAD_A == TM and n_dec * dec_seq <= TM and TM % dec_seq == 0
    assert dec_seq % 16 == 0 and seq >= PAD_A

    t_prompt = n_seq * seq
    t_real = t_prompt + n_dec * dec_seq
    t_pad = -(-t_real // (2 * TM)) * (2 * TM)
    n_prompt_tiles = t_prompt // TM
    n_slot_tiles = t_pad // TM + N_CLASSES

    x = jnp.concatenate([x_prompt.reshape(t_prompt, d), x_sample.reshape(n_dec * dec_seq, d),
                         jnp.zeros((t_pad - t_real, d), F32)], axis=0)

    w_in_b = w_in.astype(BF16)
    w_out_b = w_out.astype(BF16)
    w_gate_b, w_up_b, w_down_b = w_gate.astype(BF16), w_up.astype(BF16), w_down.astype(BF16)
    wr_t = w_router.astype(F32).T
    br = b_router.astype(F32).reshape(N_EXPERTS, 1)

    alibi_p = _alibi_table(PAD_B, CHUNK, BAND_B * CHUNK)
    alibi_s = _alibi_table(rows_cb, dec_seq, rows_cb + dec_seq)

    pa_k, pa_v, pb_k, pb_v, sa_k, sa_v, sb_k, sb_v = ([] for _ in range(8))
    for l in range(depth):
        h, kv = _inproj(x, w_in_b[l], seq=seq, n_prompt_tiles=n_prompt_tiles)

        sinks = attn_sinks[l].astype(F32)
        gna = gn_a[l].astype(F32).reshape(1, WIDTH_A)
        gnb = gn_b[l].astype(F32).reshape(1, WIDTH_B)
        bias_pa = _rel_bias_table(rel_bias[l], PAD_A, CHUNK, BAND_A * CHUNK)
        a_prompt = _attn_prompt(h, sinks, bias_pa, alibi_p, gna, gnb, n_seq=n_seq, seq=seq)
        bias_sa = _rel_bias_table(rel_bias[l], rows_ca, dec_seq, rows_ca + dec_seq)
        bias_s = (bias_sa[:, :, :rows_ca], bias_sa[:, :, rows_ca:],
                  alibi_s[:, :, :rows_cb], alibi_s[:, :, rows_cb:])
        a_sample = _attn_sample(
            h, sinks,
            cache_a_k[l].reshape(n_dec, rows_ca, WIDTH_A), cache_a_v[l].reshape(n_dec, rows_ca, WIDTH_A),
            cache_b_k[l].reshape(n_dec, rows_cb, KV_WIDTH_B), cache_b_v[l].reshape(n_dec, rows_cb, KV_WIDTH_B),
            bias_s, gna, gnb, first_row=t_prompt, n_rows_out=t_pad - t_prompt, n_dec=n_dec, dec_seq=dec_seq)

        x1g, cls = _outproj(a_prompt, a_sample, x, w_out_b[l],
                            ln1_g[l].astype(F32).reshape(1, d), ln1_b[l].astype(F32).reshape(1, d),
                            wr_t, br, n_prompt_tiles=n_prompt_tiles, alpha=alpha)
        pos, tile_ea, tile_eb, n_used, pad_lo, pad_hi = _slots(
            cls.reshape(t_pad // V7X_LANES, V7X_LANES), n_slot_tiles=n_slot_tiles)
        pos3 = pos.reshape(t_pad // TM, 1, TM)
        xs = _dispatch(pos3, pad_lo[0], pad_hi[0], n_used[0], x1g, n_slot_tiles=n_slot_tiles)
        ys = _moe(tile_ea[0, :n_slot_tiles], tile_eb[0, :n_slot_tiles], n_used[0, :1],
                  xs, w_gate_b[l], w_up_b[l], w_down_b[l])
        x = _combine(pos3, ys, x1g, ln2_g[l].astype(F32).reshape(1, d),
                     ln2_b[l].astype(F32).reshape(1, d), alpha=alpha)

        kv_p = kv[:n_seq * TM].reshape(n_seq, TM, KV_OUT)
        kv_s = kv[n_seq * TM:n_seq * TM + n_dec * dec_seq].reshape(n_dec, dec_seq, KV_OUT)
        ra, rb = min(PAD_A, seq), min(WINDOW_B, seq)
        pa_k.append(kv_p[:, TM - ra:, :WIDTH_A].reshape(n_seq, ra, N_HEADS_A, HEAD_DIM))
        pa_v.append(kv_p[:, TM - ra:, WIDTH_A:2 * WIDTH_A].reshape(n_seq, ra, N_HEADS_A, HEAD_DIM))
        pb_k.append(kv_p[:, TM - rb:, 2 * WIDTH_A:2 * WIDTH_A + KV_WIDTH_B].reshape(n_seq, rb, N_KV_B, HEAD_DIM))
        pb_v.append(kv_p[:, TM - rb:, 2 * WIDTH_A + KV_WIDTH_B:].reshape(n_seq, rb, N_KV_B, HEAD_DIM))
        sa_k.append(kv_s[:, :, :WIDTH_A].reshape(n_dec, dec_seq, N_HEADS_A, HEAD_DIM))
        sa_v.append(kv_s[:, :, WIDTH_A:2 * WIDTH_A].reshape(n_dec, dec_seq, N_HEADS_A, HEAD_DIM))
        sb_k.append(kv_s[:, :, 2 * WIDTH_A:2 * WIDTH_A + KV_WIDTH_B].reshape(n_dec, dec_seq, N_KV_B, HEAD_DIM))
        sb_v.append(kv_s[:, :, 2 * WIDTH_A + KV_WIDTH_B:].reshape(n_dec, dec_seq, N_KV_B, HEAD_DIM))

    y_prompt = x[:t_prompt].reshape(n_seq, seq, d)
    y_sample = x[t_prompt:t_real].reshape(n_dec, dec_seq, d)
    return (y_prompt, y_sample,
            jnp.stack(pa_k), jnp.stack(pa_v), jnp.stack(pb_k), jnp.stack(pb_v),
            jnp.stack(sa_k), jnp.stack(sa_v), jnp.stack(sb_k), jnp.stack(sb_v))
```

```python
import functools

import jax
import jax.numpy as jnp
import numpy as np
from jax import lax
from jax.experimental import pallas as pl
from jax.experimental.pallas import tpu as pltpu

CHUNK = 64
HEAD_DIM = 64
N_HEADS_A = 8
BAND_A = 9
REL_CLIP = 128
N_HEADS_B = 8
N_KV_B = 2
WINDOW_B = 128
BAND_B = 1 + WINDOW_B // CHUNK
N_EXPERTS = 16
N_GROUPS = 4
EXPERTS_PER_GROUP = N_EXPERTS // N_GROUPS
PAIRS_PER_GROUP = 6
N_CLASSES = N_GROUPS * PAIRS_PER_GROUP
LN_EPS = 1e-5
RMS_EPS = 1e-6

WIDTH_A = N_HEADS_A * HEAD_DIM
WIDTH_B = N_HEADS_B * HEAD_DIM
KV_WIDTH_B = N_KV_B * HEAD_DIM
D_IN = 3 * WIDTH_A + WIDTH_B + 2 * KV_WIDTH_B
COL_QA, COL_KA, COL_VA = 0, WIDTH_A, 2 * WIDTH_A
COL_QB = 3 * WIDTH_A
COL_KB = COL_QB + WIDTH_B
COL_VB = COL_KB + KV_WIDTH_B
KV_OUT = 2 * WIDTH_A + 2 * KV_WIDTH_B
PAD_A = (BAND_A - 1) * CHUNK
PAD_B = WINDOW_B

V7X_LANES = 128
TM = 512
GATE_LANES = V7X_LANES
V7X_VMEM_LIMIT = 56 * 1024 * 1024

F32 = jnp.float32
BF16 = jnp.bfloat16
NEG_INF = float("-inf")


def _cparams(sem):
    return pltpu.CompilerParams(dimension_semantics=sem, vmem_limit_bytes=V7X_VMEM_LIMIT)


def _inproj_kernel(x_ref, w_ref, h_ref, kv_ref, *, tiles_per_seq, n_prompt_tiles):
    i = pl.program_id(0)
    acc = jnp.dot(x_ref[...].astype(BF16), w_ref[...], preferred_element_type=F32)
    h_ref[...] = acc.astype(BF16)
    keeps_kv = jnp.logical_or(i % tiles_per_seq == tiles_per_seq - 1, i >= n_prompt_tiles)

    @pl.when(keeps_kv)
    def _():
        kv_ref[:, : 2 * WIDTH_A] = acc[:, COL_KA:COL_QB]
        kv_ref[:, 2 * WIDTH_A:] = acc[:, COL_KB:]


def _inproj(x, w_in, *, seq, n_prompt_tiles):
    t_pad, d = x.shape
    n_tiles = t_pad // TM
    tiles_per_seq = seq // TM
    n_seq = n_prompt_tiles // tiles_per_seq
    n_kv_blocks = n_seq + (n_tiles - n_prompt_tiles)

    def kv_map(i):
        return (jnp.where(i < n_prompt_tiles, i // tiles_per_seq, n_seq + i - n_prompt_tiles), 0)

    return pl.pallas_call(
        functools.partial(_inproj_kernel, tiles_per_seq=tiles_per_seq, n_prompt_tiles=n_prompt_tiles),
        grid=(n_tiles,),
        in_specs=[pl.BlockSpec((TM, d), lambda i: (i, 0)),
                  pl.BlockSpec((d, D_IN), lambda i: (0, 0))],
        out_specs=[pl.BlockSpec((TM, D_IN), lambda i: (i, 0)),
                   pl.BlockSpec((TM, KV_OUT), kv_map)],
        out_shape=[jax.ShapeDtypeStruct((t_pad, D_IN), BF16),
                   jax.ShapeDtypeStruct((n_kv_blocks * TM, KV_OUT), F32)],
        compiler_params=_cparams(("arbitrary",)),
        name="inproj",
    )(x, w_in)


def _softmax_pv(s_parts, v_parts, sink=None):
    m = s_parts[0].max(axis=-1, keepdims=True)
    for s in s_parts[1:]:
        m = jnp.maximum(m, s.max(axis=-1, keepdims=True))
    if sink is not None:
        m = jnp.maximum(m, sink)
    denom = None
    acc = None
    for s, v in zip(s_parts, v_parts):
        e = jnp.exp(s - m)
        d = e.sum(axis=-1, keepdims=True)
        pv = jnp.dot(e.astype(BF16), v, preferred_element_type=F32)
        denom = d if denom is None else denom + d
        acc = pv if acc is None else acc + pv
    if sink is not None:
        denom = denom + jnp.exp(sink - m)
    return acc / denom


def _nt_dot(a, b):
    return lax.dot_general(a, b, (((1,), (1,)), ((), ())), preferred_element_type=F32)


def _head_pair_lane_mask(rows):
    return lax.broadcasted_iota(jnp.int32, (rows, 2 * HEAD_DIM), 1) < HEAD_DIM


def _rms_store(o_ref, row_slice, col0, pairs, gn_ref):
    ssq = None
    for o in pairs:
        s = (o * o).sum(axis=-1, keepdims=True)
        ssq = s if ssq is None else ssq + s
    width = len(pairs) * 2 * HEAD_DIM
    inv = lax.rsqrt(ssq / width + RMS_EPS)
    for p, o in enumerate(pairs):
        c0 = p * 2 * HEAD_DIM
        g = gn_ref[:, c0:c0 + 2 * HEAD_DIM]
        o_ref[row_slice, col0 + c0:col0 + c0 + 2 * HEAD_DIM] = (o * inv * g).astype(o_ref.dtype)


KEY_BLOCK = V7X_LANES


def _swap_halves(x):
    return jnp.concatenate([x[:, HEAD_DIM:], x[:, :HEAD_DIM]], axis=1)


def _pair_rows(q2, lane_lo, kv_half=None):
    hi_lanes = jnp.logical_not(lane_lo)
    zero = jnp.zeros_like(q2)
    if kv_half is None:
        lo, hi = jnp.where(lane_lo, q2, zero), jnp.where(hi_lanes, q2, zero)
    elif kv_half == 0:
        lo, hi = jnp.where(lane_lo, q2, zero), jnp.where(lane_lo, _swap_halves(q2), zero)
    else:
        lo, hi = jnp.where(hi_lanes, _swap_halves(q2), zero), jnp.where(hi_lanes, q2, zero)
    return jnp.concatenate([lo, hi], axis=0)


def _tn_dot(a, b):
    return lax.dot_general(a, b, (((0,), (0,)), ((), ())), preferred_element_type=F32)


PV_KEYS = 256


def _score_blocks(k_ref, lanes, row0, n_keys, w):
    out = []
    for off in range(0, n_keys, KEY_BLOCK):
        n = min(KEY_BLOCK, n_keys - off)
        start = pl.multiple_of(row0 + off, CHUNK)
        out.append((_nt_dot(k_ref[pl.ds(start, n), lanes], w), start, n, off))
    return out


def _pair_probs(score_blocks, lane_tile, bias_ref, pair, pad, masked, sink_block=None):
    lanes = slice(lane_tile * V7X_LANES, (lane_tile + 1) * V7X_LANES)
    blocks = []
    for s_all, start, n, off in score_blocks:
        s = s_all[:, lanes] + bias_ref[pair, off:off + n, :]
        if masked:
            key_row = lax.broadcasted_iota(jnp.int32, s.shape, 0) + start
            s = jnp.where(key_row >= pad, s, NEG_INF)
        blocks.append((s, start, n))
    m = None
    for s, _, _ in blocks:
        bm = s.max(axis=0, keepdims=True)
        m = bm if m is None else jnp.maximum(m, bm)
    if sink_block is not None:
        m = jnp.maximum(m, sink_block.max(axis=0, keepdims=True))
    probs = [(jnp.exp(s - m).astype(BF16), start, n) for s, start, n in blocks]
    sink_p = None if sink_block is None else jnp.exp(sink_block - m).astype(BF16)
    return probs, sink_p


def _pair_values(probs, sink_p, v_ref, lanes):
    groups, cur, cur_rows = [], [], 0
    for blk in probs:
        if cur and cur_rows + blk[2] > PV_KEYS:
            groups.append(cur)
            cur, cur_rows = [], 0
        cur.append(blk)
        cur_rows += blk[2]
    groups.append(cur)
    acc = None
    for gi, grp in enumerate(groups):
        rows = sum(n for _, _, n in grp)
        p = grp[0][0] if len(grp) == 1 else jnp.concatenate([b[0] for b in grp], axis=0)
        v1 = jnp.concatenate([v_ref[pl.ds(grp[0][1], rows), lanes], jnp.ones((rows, V7X_LANES), BF16)], axis=1)
        if sink_p is not None and gi == len(groups) - 1:
            n = sink_p.shape[0]
            p = jnp.concatenate([p, sink_p], axis=0)
            sink_v = jnp.concatenate([jnp.zeros((n, V7X_LANES), BF16), jnp.ones((n, V7X_LANES), BF16)], axis=1)
            v1 = jnp.concatenate([v1, sink_v], axis=0)
        c = _tn_dot(p, v1)
        acc = c if acc is None else acc + c
    return acc[:, :V7X_LANES] / acc[:, V7X_LANES:]


def _quad_rows(q4):
    head_of_lane = lax.broadcasted_iota(jnp.int32, q4.shape, 1) // HEAD_DIM
    zero = jnp.zeros_like(q4)
    return jnp.concatenate([jnp.where(head_of_lane == h, q4, zero) for h in range(4)], axis=0)


def _attn_prompt_kernel(qa_ref, ka_ref, va_ref, qb_ref, kb_ref, vb_ref,
                        ba_ref, bb_ref, sink_ref, gna_ref, gnb_ref, o_ref,
                        kpa, vpa, kpb, vpb, pa_even, pb_even, pa_odd, pb_odd, *, seq, tq):
    j = pl.program_id(1)

    @pl.when(j == 0)
    def _():
        kpa[0:PAD_A, :] = jnp.zeros((PAD_A, WIDTH_A), BF16)
        vpa[0:PAD_A, :] = jnp.zeros((PAD_A, WIDTH_A), BF16)
        kpb[0:PAD_B, :] = jnp.zeros((PAD_B, KV_WIDTH_B), BF16)
        vpb[0:PAD_B, :] = jnp.zeros((PAD_B, KV_WIDTH_B), BF16)
        kpa[PAD_A:PAD_A + seq, :] = ka_ref[...]
        vpa[PAD_A:PAD_A + seq, :] = va_ref[...]
        kpb[PAD_B:PAD_B + seq, :] = kb_ref[...]
        vpb[PAD_B:PAD_B + seq, :] = vb_ref[...]

    lane_lo = _head_pair_lane_mask(CHUNK)
    scale = HEAD_DIM ** -0.5
    all_lanes = slice(0, KV_WIDTH_B)

    n_pairs_a, n_pairs_b = N_HEADS_A // 2, N_HEADS_B // 2
    band_a, band_b = BAND_A * CHUNK, BAND_B * CHUNK
    n_sink = sink_ref.shape[1]
    even_bufs, odd_bufs = (pa_even, pb_even), (pa_odd, pb_odd)

    def band_start(c):
        return pl.multiple_of(j * tq + c * CHUNK, CHUNK)

    def probs_phase(c, masked, bufs):
        pa_buf, pb_buf = bufs
        rows = pl.ds(pl.multiple_of(c * CHUNK, CHUNK), CHUNK)
        cs = band_start(c)
        scores_a = []
        for quad in range(N_HEADS_A // 4):
            lanes = slice(quad * 4 * HEAD_DIM, (quad + 1) * 4 * HEAD_DIM)
            w = _quad_rows(qa_ref[rows, lanes] * scale)
            scores_a.append(_score_blocks(kpa, lanes, cs, band_a, w))
        w = jnp.concatenate(
            [_pair_rows(qb_ref[rows, p * 2 * HEAD_DIM:(p + 1) * 2 * HEAD_DIM] * scale, lane_lo,
                        kv_half=(2 * p) // (N_HEADS_B // N_KV_B))
             for p in range(n_pairs_b)], axis=0)
        scores_b = _score_blocks(kpb, all_lanes, cs, band_b, w)
        for p in range(n_pairs_a):
            probs, _ = _pair_probs(scores_a[p // 2], p % 2, ba_ref, p, PAD_A, masked)
            for blk, (_, _, n, off) in zip(probs, scores_a[p // 2]):
                pa_buf[p, off:off + n, :] = blk[0]
        for p in range(n_pairs_b):
            probs, sink_p = _pair_probs(scores_b, p, bb_ref, p, PAD_B, masked, sink_block=sink_ref[p])
            for blk, (_, _, n, off) in zip(probs, scores_b):
                pb_buf[p, off:off + n, :] = blk[0]
            pb_buf[p, band_b:band_b + n_sink, :] = sink_p

    def values_phase(c, bufs):
        pa_buf, pb_buf = bufs
        rows = pl.ds(pl.multiple_of(c * CHUNK, CHUNK), CHUNK)
        cs = band_start(c)

        def groups(buf, p, n_keys):
            return [(buf[p, off:off + min(PV_KEYS, n_keys - off), :], pl.multiple_of(cs + off, CHUNK),
                     min(PV_KEYS, n_keys - off)) for off in range(0, n_keys, PV_KEYS)]

        pairs_a = []
        for p in range(n_pairs_a):
            lanes = slice(p * 2 * HEAD_DIM, (p + 1) * 2 * HEAD_DIM)
            o = _pair_values(groups(pa_buf, p, band_a), None, vpa, lanes)
            pairs_a.append(jnp.where(lane_lo, o[:CHUNK], o[CHUNK:]))
        _rms_store(o_ref, rows, 0, pairs_a, gna_ref)

        pairs_b = []
        for p in range(n_pairs_b):
            kv_half = (2 * p) // (N_HEADS_B // N_KV_B)
            o = _pair_values(groups(pb_buf, p, band_b), pb_buf[p, band_b:band_b + n_sink, :], vpb, all_lanes)
            lo, hi = o[:CHUNK], o[CHUNK:]
            if kv_half == 0:
                hi = _swap_halves(hi)
            else:
                lo = _swap_halves(lo)
            pairs_b.append(jnp.where(lane_lo, lo, hi))
        _rms_store(o_ref, rows, WIDTH_A, pairs_b, gnb_ref)

    def run(masked):
        n_chunks = tq // CHUNK
        assert n_chunks % 2 == 0
        probs_phase(0, masked, even_bufs)

        def body(k, carry):
            probs_phase(2 * k + 1, masked, odd_bufs)
            values_phase(2 * k, even_bufs)
            probs_phase(2 * k + 2, masked, even_bufs)
            values_phase(2 * k + 1, odd_bufs)
            return carry

        lax.fori_loop(0, n_chunks // 2 - 1, body, 0)
        probs_phase(n_chunks - 1, masked, odd_bufs)
        values_phase(n_chunks - 2, even_bufs)
        values_phase(n_chunks - 1, odd_bufs)

    @pl.when(j == 0)
    def _():
        run(True)

    @pl.when(j > 0)
    def _():
        run(False)


def _pair_transposed(table):
    n_h, n_q, n_k = table.shape
    return table.reshape(n_h // 2, 2, n_q, n_k).transpose(0, 3, 1, 2).reshape(n_h // 2, n_k, 2 * n_q)


def _sink_blocks(sinks):
    row = jnp.repeat(sinks.astype(F32).reshape(N_HEADS_B // 2, 2), HEAD_DIM, axis=1)
    rest = jnp.full((N_HEADS_B // 2, 15, 2 * HEAD_DIM), NEG_INF, F32)
    return jnp.concatenate([row[:, None, :], rest], axis=1)


def _attn_prompt(h, sinks, bias_a, bias_b, gn_a, gn_b, *, n_seq, seq):
    tq = TM
    assert tq >= PAD_A and tq >= PAD_B and tq % CHUNK == 0
    n_q = seq // tq
    vmem = pltpu.VMEM
    bias_a, bias_b = _pair_transposed(bias_a), _pair_transposed(bias_b)
    sinks = _sink_blocks(sinks)
    return pl.pallas_call(
        functools.partial(_attn_prompt_kernel, seq=seq, tq=tq),
        grid=(n_seq, n_q),
        in_specs=[
            pl.BlockSpec((tq, WIDTH_A), lambda b, j: (b * n_q + j, COL_QA // WIDTH_A)),
            pl.BlockSpec((seq, WIDTH_A), lambda b, j: (b, COL_KA // WIDTH_A)),
            pl.BlockSpec((seq, WIDTH_A), lambda b, j: (b, COL_VA // WIDTH_A)),
            pl.BlockSpec((tq, WIDTH_B), lambda b, j: (b * n_q + j, COL_QB // WIDTH_B)),
            pl.BlockSpec((seq, KV_WIDTH_B), lambda b, j: (b, COL_KB // KV_WIDTH_B)),
            pl.BlockSpec((seq, KV_WIDTH_B), lambda b, j: (b, COL_VB // KV_WIDTH_B)),
            pl.BlockSpec(bias_a.shape, lambda b, j: (0, 0, 0)),
            pl.BlockSpec(bias_b.shape, lambda b, j: (0, 0, 0)),
            pl.BlockSpec(sinks.shape, lambda b, j: (0, 0, 0)),
            pl.BlockSpec(gn_a.shape, lambda b, j: (0, 0)),
            pl.BlockSpec(gn_b.shape, lambda b, j: (0, 0)),
        ],
        out_specs=pl.BlockSpec((tq, WIDTH_A + WIDTH_B), lambda b, j: (b * n_q + j, 0)),
        out_shape=jax.ShapeDtypeStruct((n_seq * seq, WIDTH_A + WIDTH_B), BF16),
        scratch_shapes=[vmem((PAD_A + seq, WIDTH_A), BF16), vmem((PAD_A + seq, WIDTH_A), BF16),
                        vmem((PAD_B + seq, KV_WIDTH_B), BF16), vmem((PAD_B + seq, KV_WIDTH_B), BF16),
                        vmem((N_HEADS_A // 2, BAND_A * CHUNK, V7X_LANES), BF16),
                        vmem((N_HEADS_B // 2, BAND_B * CHUNK + sinks.shape[1], V7X_LANES), BF16),
                        vmem((N_HEADS_A // 2, BAND_A * CHUNK, V7X_LANES), BF16),
                        vmem((N_HEADS_B // 2, BAND_B * CHUNK + sinks.shape[1], V7X_LANES), BF16)],
        compiler_params=_cparams(("arbitrary", "arbitrary")),
        name="attn_prompt",
    )(h, h, h, h, h, h, bias_a, bias_b, sinks, gn_a, gn_b)


def _attn_sample_kernel(sink_ref, qa_ref, ka_ref, va_ref, qb_ref, kb_ref, vb_ref,
                        cka_ref, cva_ref, ckb_ref, cvb_ref,
                        bac_ref, ban_ref, bbc_ref, bbn_ref, gna_ref, gnb_ref, o_ref, *, n_dec, dec_seq):
    b = pl.program_id(0)

    @pl.when(b >= n_dec)
    def _():
        o_ref[...] = jnp.zeros(o_ref.shape, o_ref.dtype)

    @pl.when(b < n_dec)
    def _():
        lane_lo = _head_pair_lane_mask(dec_seq)
        scale = HEAD_DIM ** -0.5
        rows = slice(0, dec_seq)

        pairs_a = []
        for p in range(N_HEADS_A // 2):
            lanes = slice(p * 2 * HEAD_DIM, (p + 1) * 2 * HEAD_DIM)
            q2 = qa_ref[:, lanes] * scale
            kc = cka_ref[0, :, lanes].astype(BF16)
            vc = cva_ref[0, :, lanes].astype(BF16)
            kn = ka_ref[:, lanes]
            vn = va_ref[:, lanes]
            halves = []
            for half in range(2):
                h = 2 * p + half
                keep = lane_lo if half == 0 else jnp.logical_not(lane_lo)
                qm = jnp.where(keep, q2, jnp.zeros_like(q2))
                s_c = _nt_dot(qm, kc) + bac_ref[h]
                s_n = _nt_dot(qm, kn) + ban_ref[h]
                halves.append(_softmax_pv([s_c, s_n], [vc, vn]))
            pairs_a.append(jnp.where(lane_lo, halves[0], halves[1]))
        _rms_store(o_ref, rows, 0, pairs_a, gna_ref)

        kc = ckb_ref[0].astype(BF16)
        vc = cvb_ref[0].astype(BF16)
        kn = kb_ref[...]
        vn = vb_ref[...]
        pairs_b = []
        for p in range(N_HEADS_B // 2):
            lanes = slice(p * 2 * HEAD_DIM, (p + 1) * 2 * HEAD_DIM)
            q2 = qb_ref[:, lanes] * scale
            halves = []
            for half in range(2):
                h = 2 * p + half
                kv_head = h // (N_HEADS_B // N_KV_B)
                keep = lane_lo if half == 0 else jnp.logical_not(lane_lo)
                qm = jnp.where(keep, q2, jnp.zeros_like(q2))
                if kv_head != half:
                    qm = jnp.concatenate([qm[:, HEAD_DIM:], qm[:, :HEAD_DIM]], axis=1)
                s_c = _nt_dot(qm, kc) + bbc_ref[h]
                s_n = _nt_dot(qm, kn) + bbn_ref[h]
                o = _softmax_pv([s_c, s_n], [vc, vn], sink=sink_ref[h])
                if kv_head != half:
                    o = jnp.concatenate([o[:, HEAD_DIM:], o[:, :HEAD_DIM]], axis=1)
                halves.append(o)
            pairs_b.append(jnp.where(lane_lo, halves[0], halves[1]))
        _rms_store(o_ref, rows, WIDTH_A, pairs_b, gnb_ref)


def _attn_sample(h, sinks, cka, cva, ckb, cvb, bias, gn_a, gn_b, *, first_row, n_rows_out, n_dec, dec_seq):
    bac, ban, bbc, bbn = bias
    n_steps = n_rows_out // dec_seq
    rb = first_row // dec_seq
    last = n_dec - 1

    def hrow(b):
        return rb + jnp.minimum(b, last)

    def cache_spec(c):
        return pl.BlockSpec((1,) + c.shape[1:], lambda b: (jnp.minimum(b, last), 0, 0))

    def const_spec(a):
        return pl.BlockSpec(a.shape, lambda b: (0,) * a.ndim)

    return pl.pallas_call(
        functools.partial(_attn_sample_kernel, n_dec=n_dec, dec_seq=dec_seq),
        grid=(n_steps,),
        in_specs=[
            pl.BlockSpec(memory_space=pltpu.SMEM),
            pl.BlockSpec((dec_seq, WIDTH_A), lambda b: (hrow(b), COL_QA // WIDTH_A)),
            pl.BlockSpec((dec_seq, WIDTH_A), lambda b: (hrow(b), COL_KA // WIDTH_A)),
            pl.BlockSpec((dec_seq, WIDTH_A), lambda b: (hrow(b), COL_VA // WIDTH_A)),
            pl.BlockSpec((dec_seq, WIDTH_B), lambda b: (hrow(b), COL_QB // WIDTH_B)),
            pl.BlockSpec((dec_seq, KV_WIDTH_B), lambda b: (hrow(b), COL_KB // KV_WIDTH_B)),
            pl.BlockSpec((dec_seq, KV_WIDTH_B), lambda b: (hrow(b), COL_VB // KV_WIDTH_B)),
            cache_spec(cka), cache_spec(cva), cache_spec(ckb), cache_spec(cvb),
            const_spec(bac), const_spec(ban), const_spec(bbc), const_spec(bbn),
            const_spec(gn_a), const_spec(gn_b),
        ],
        out_specs=pl.BlockSpec((dec_seq, WIDTH_A + WIDTH_B), lambda b: (b, 0)),
        out_shape=jax.ShapeDtypeStruct((n_rows_out, WIDTH_A + WIDTH_B), BF16),
        compiler_params=_cparams(("arbitrary",)),
        name="attn_sample",
    )(sinks, h, h, h, h, h, h, cka, cva, ckb, cvb, bac, ban, bbc, bbn, gn_a, gn_b)


def _layer_norm(r, g, b):
    mu = r.mean(axis=-1, keepdims=True)
    c = r - mu
    var = (c * c).mean(axis=-1, keepdims=True)
    return c * lax.rsqrt(var + LN_EPS) * g + b


def _route_rows(logits):
    rows = [logits[e:e + 1, :] for e in range(N_EXPERTS)]

    def top2_sum(a, b, c, d):
        hi1, lo1 = jnp.maximum(a, b), jnp.minimum(a, b)
        hi2, lo2 = jnp.maximum(c, d), jnp.minimum(c, d)
        return jnp.maximum(hi1, hi2) + jnp.maximum(jnp.minimum(hi1, hi2), jnp.maximum(lo1, lo2))

    scores = [top2_sum(*rows[4 * g:4 * g + 4]) for g in range(N_GROUPS)]
    best = scores[0]
    g_sel = jnp.zeros(best.shape, jnp.int32)
    for g in range(1, N_GROUPS):
        upd = scores[g] > best
        best = jnp.where(upd, scores[g], best)
        g_sel = jnp.where(upd, g, g_sel)

    vals = []
    for k in range(EXPERTS_PER_GROUP):
        v = rows[k]
        for g in range(1, N_GROUPS):
            v = jnp.where(g_sel == g, rows[4 * g + k], v)
        vals.append(v)

    v1 = vals[0]
    i1 = jnp.zeros(v1.shape, jnp.int32)
    for k in range(1, EXPERTS_PER_GROUP):
        upd = vals[k] > v1
        v1 = jnp.where(upd, vals[k], v1)
        i1 = jnp.where(upd, k, i1)
    v2 = jnp.full(v1.shape, NEG_INF, F32)
    i2 = jnp.zeros(v1.shape, jnp.int32)
    for k in range(EXPERTS_PER_GROUP):
        upd = jnp.logical_and(i1 != k, vals[k] > v2)
        v2 = jnp.where(upd, vals[k], v2)
        i2 = jnp.where(upd, k, i2)

    e = jnp.exp(v2 - v1)
    den = 1.0 + e
    w1 = 1.0 / den
    w2 = e / den
    first_is_lo = i1 < i2
    lo = jnp.where(first_is_lo, i1, i2)
    hi = jnp.where(first_is_lo, i2, i1)
    gate_lo = jnp.where(first_is_lo, w1, w2)
    gate_hi = jnp.where(first_is_lo, w2, w1)
    pair_base = jnp.where(lo == 0, 0, jnp.where(lo == 1, 3, 5))
    cls = g_sel * PAIRS_PER_GROUP + pair_base + hi - lo - 1
    return cls, gate_lo, gate_hi


def _outproj_kernel(ap_ref, as_ref, x_ref, w_ref, g_ref, b_ref, wr_ref, br_ref,
                    x1_ref, cls_ref, *, n_prompt_tiles, alpha, d):
    i = pl.program_id(0)
    a = jnp.where(i < n_prompt_tiles, ap_ref[...], as_ref[...])
    y = jnp.dot(a, w_ref[...], preferred_element_type=F32)
    x1 = _layer_norm(alpha * x_ref[...] + y, g_ref[...], b_ref[...])
    x1_ref[:, :d] = x1
    logits = lax.dot_general(wr_ref[...], x1, (((1,), (1,)), ((), ())),
                             precision=lax.Precision.HIGHEST, preferred_element_type=F32) + br_ref[...]
    cls, gate_lo, gate_hi = _route_rows(logits)
    cls_ref[...] = cls
    n = x1.shape[0]
    gates = jnp.concatenate([gate_lo, gate_hi, jnp.zeros((GATE_LANES - 2, n), F32)], axis=0)
    x1_ref[:, d:] = gates.T


def _outproj(a_prompt, a_sample, x, w_out, g, b, wr_t, br, *, n_prompt_tiles, alpha):
    t_pad, d = x.shape
    n_tiles = t_pad // TM
    last_p = n_prompt_tiles - 1
    const = lambda i: (0, 0)
    return pl.pallas_call(
        functools.partial(_outproj_kernel, n_prompt_tiles=n_prompt_tiles, alpha=alpha, d=d),
        grid=(n_tiles,),
        in_specs=[pl.BlockSpec((TM, d), lambda i: (jnp.minimum(i, last_p), 0)),
                  pl.BlockSpec((TM, d), lambda i: (jnp.maximum(i - n_prompt_tiles, 0), 0)),
                  pl.BlockSpec((TM, d), lambda i: (i, 0)),
                  pl.BlockSpec(w_out.shape, const),
                  pl.BlockSpec(g.shape, const), pl.BlockSpec(b.shape, const),
                  pl.BlockSpec(wr_t.shape, const), pl.BlockSpec(br.shape, const)],
        out_specs=[pl.BlockSpec((TM, d + GATE_LANES), lambda i: (i, 0)),
                   pl.BlockSpec((1, TM), lambda i: (0, i))],
        out_shape=[jax.ShapeDtypeStruct((t_pad, d + GATE_LANES), F32),
                   jax.ShapeDtypeStruct((1, t_pad), jnp.int32)],
        compiler_params=_cparams(("arbitrary",)),
        name="outproj_ln_route",
    )(a_prompt, a_sample, x, w_out, g, b, wr_t, br)


def _slots_kernel(cls_ref, pos_ref, ea_ref, eb_ref, nu_ref, padlo_ref, padhi_ref, *, rows_pad):
    lane = lax.broadcasted_iota(jnp.int32, (1, V7X_LANES), 1)
    pad_lo = jnp.zeros((1, V7X_LANES), F32)
    pad_hi = jnp.zeros((1, V7X_LANES), F32)
    cls = cls_ref[...]
    rows = cls.shape[0]
    r_i = lax.broadcasted_iota(jnp.int32, (rows_pad, rows_pad), 0)
    c_i = lax.broadcasted_iota(jnp.int32, (rows_pad, rows_pad), 1)
    lower = jnp.where(c_i < r_i, 1.0, 0.0).astype(BF16)
    l_r = lax.broadcasted_iota(jnp.int32, (V7X_LANES, V7X_LANES), 0)
    l_c = lax.broadcasted_iota(jnp.int32, (V7X_LANES, V7X_LANES), 1)
    upper = jnp.where(l_r < l_c, 1.0, 0.0).astype(BF16)

    n_t = ea_ref.shape[1]
    tile_start = lax.broadcasted_iota(jnp.int32, (1, n_t), 1).astype(F32) * float(TM)
    tile_cls = jnp.zeros((1, n_t), jnp.int32)
    off = jnp.zeros((1, 1), F32)
    pos = jnp.zeros((rows, V7X_LANES), F32)
    for c in range(N_CLASSES):
        m = jnp.where(cls == c, 1.0, 0.0)
        mb = m.astype(BF16)
        if rows_pad > rows:
            mb_pad = jnp.concatenate([mb, jnp.zeros((rows_pad - rows, V7X_LANES), BF16)], axis=0)
        else:
            mb_pad = mb
        before_rows = jnp.dot(lower, mb_pad, preferred_element_type=F32)[:rows]
        before_rows = before_rows.sum(axis=-1, keepdims=True)
        before_lanes = jnp.dot(mb, upper, preferred_element_type=F32)
        pos = pos + m * (off + before_rows + before_lanes)
        if c > 0:
            tile_cls = tile_cls + jnp.where(tile_start >= off, 1, 0)
        count = m.sum(axis=-1, keepdims=True).sum(axis=0, keepdims=True)
        pad_lo = jnp.where(lane == c, off + count, pad_lo)
        off = off + jnp.floor((count + float(TM - 1)) * (1.0 / TM)) * float(TM)
        pad_hi = jnp.where(lane == c, off, pad_hi)
    pos_ref[...] = pos.astype(jnp.int32)
    padlo_ref[...] = pad_lo.astype(jnp.int32)
    padhi_ref[...] = pad_hi.astype(jnp.int32)
    grp = (jnp.where(tile_cls >= 6, 1, 0) + jnp.where(tile_cls >= 12, 1, 0)
           + jnp.where(tile_cls >= 18, 1, 0))
    pair = tile_cls - PAIRS_PER_GROUP * grp
    lo = jnp.where(pair >= 3, 1, 0) + jnp.where(pair >= 5, 1, 0)
    hi = pair + 1 - 2 * jnp.where(pair >= 3, 1, 0) - jnp.where(pair >= 5, 1, 0)
    ea_ref[...] = EXPERTS_PER_GROUP * grp + lo
    eb_ref[...] = EXPERTS_PER_GROUP * grp + hi
    nu_ref[...] = jnp.broadcast_to(off * (1.0 / TM), nu_ref.shape).astype(jnp.int32)


def _slots(cls2d, *, n_slot_tiles):
    rows = cls2d.shape[0]
    rows_pad = -(-rows // V7X_LANES) * V7X_LANES
    n_t = -(-n_slot_tiles // V7X_LANES) * V7X_LANES
    i32 = jnp.int32
    return pl.pallas_call(
        functools.partial(_slots_kernel, rows_pad=rows_pad),
        out_shape=[jax.ShapeDtypeStruct(cls2d.shape, i32),
                   jax.ShapeDtypeStruct((1, n_t), i32), jax.ShapeDtypeStruct((1, n_t), i32),
                   jax.ShapeDtypeStruct((1, V7X_LANES), i32), jax.ShapeDtypeStruct((1, V7X_LANES), i32),
                   jax.ShapeDtypeStruct((1, V7X_LANES), i32)],
        compiler_params=pltpu.CompilerParams(vmem_limit_bytes=V7X_VMEM_LIMIT),
        name="moe_slots",
    )(cls2d)


ROW_DMA_UNROLL = 8


def _row_copy(src, dst, src_row, dst_row, sem):
    return pltpu.make_async_copy(src.at[pl.ds(src_row, 1)], dst.at[pl.ds(dst_row, 1)], sem)


def _dispatch_kernel(pos_ref, padlo_ref, padhi_ref, nu_ref, x_ref, xs_hbm, zbuf, sem, *, n_slot_tiles):
    @pl.when(pl.program_id(0) == 0)
    def _():
        zbuf[...] = jnp.zeros(zbuf.shape, zbuf.dtype)

        def zero_tile(t, carry):
            cp = pltpu.make_async_copy(zbuf, xs_hbm.at[pl.ds(pl.multiple_of(t * TM, TM), TM)], sem)
            cp.start()
            cp.wait()
            return carry

        lax.fori_loop(nu_ref[0], n_slot_tiles, zero_tile, 0)

        for c in range(N_CLASSES):
            lo, hi = padlo_ref[c], padhi_ref[c]

            def zero_row(s, carry):
                _row_copy(zbuf, xs_hbm, 0, s, sem).start()
                return carry

            def zero_wait(s, carry):
                _row_copy(zbuf, xs_hbm, 0, 0, sem).wait()
                return carry

            lax.fori_loop(lo, hi, zero_row, 0)
            lax.fori_loop(lo, hi, zero_wait, 0)

    def start(t, carry):
        _row_copy(x_ref, xs_hbm, t, pos_ref[0, 0, t], sem).start()
        return carry

    lax.fori_loop(0, TM, start, 0, unroll=ROW_DMA_UNROLL)
    pltpu.make_async_copy(x_ref, xs_hbm.at[pl.ds(0, TM)], sem).wait()


def _dispatch(pos3, pad_lo, pad_hi, n_used, x1g, *, n_slot_tiles):
    n_tiles = pos3.shape[0]
    width = x1g.shape[1]
    smem = pl.BlockSpec(memory_space=pltpu.SMEM)
    return pl.pallas_call(
        functools.partial(_dispatch_kernel, n_slot_tiles=n_slot_tiles),
        grid=(n_tiles,),
        in_specs=[pl.BlockSpec((1, 1, TM), lambda i: (i, 0, 0), memory_space=pltpu.SMEM),
                  smem, smem, smem,
                  pl.BlockSpec((TM, width), lambda i: (i, 0))],
        out_specs=pl.BlockSpec(memory_space=pl.ANY),
        out_shape=jax.ShapeDtypeStruct((n_slot_tiles * TM, width), x1g.dtype),
        scratch_shapes=[pltpu.VMEM((TM, width), x1g.dtype), pltpu.SemaphoreType.DMA(())],
        compiler_params=pltpu.CompilerParams(dimension_semantics=("arbitrary",), has_side_effects=True,
                                             vmem_limit_bytes=V7X_VMEM_LIMIT),
        name="moe_dispatch",
    )(pos3, pad_lo, pad_hi, n_used, x1g)


def _moe_kernel(ea_ref, eb_ref, nu_ref, xs_ref, wga, wua, wda, wgb, wub, wdb, ys_ref, *, d):
    del ea_ref, eb_ref
    used = pl.program_id(0) < nu_ref[0]

    @pl.when(jnp.logical_not(used))
    def _():
        ys_ref[...] = jnp.zeros(ys_ref.shape, ys_ref.dtype)

    @pl.when(used)
    def _():
        x = xs_ref[:, :d].astype(BF16)
        gate_a = xs_ref[:, d:d + 1]
        gate_b = xs_ref[:, d + 1:d + 2]

        def expert(wg, wu, wd):
            g = jnp.dot(x, wg[0], preferred_element_type=F32)
            u = jnp.dot(x, wu[0], preferred_element_type=F32)
            h = (g * jax.nn.sigmoid(g)) * u
            return jnp.dot(h.astype(BF16), wd[0], preferred_element_type=F32)

        ys_ref[...] = gate_a * expert(wga, wua, wda) + gate_b * expert(wgb, wub, wdb)


def _moe(tile_ea, tile_eb, n_used, xs, w_gate, w_up, w_down):
    n_slots, width = xs.shape
    d = width - GATE_LANES
    n_tiles = n_slots // TM
    d_e = w_gate.shape[-1]

    def row_map(i, ea, eb, nu):
        return (jnp.minimum(i, nu[0] - 1), 0)

    def w_a(i, ea, eb, nu):
        return (ea[i], 0, 0)

    def w_b(i, ea, eb, nu):
        return (eb[i], 0, 0)

    grid_spec = pltpu.PrefetchScalarGridSpec(
        num_scalar_prefetch=3,
        grid=(n_tiles,),
        in_specs=[pl.BlockSpec((TM, width), row_map),
                  pl.BlockSpec((1, d, d_e), w_a), pl.BlockSpec((1, d, d_e), w_a),
                  pl.BlockSpec((1, d_e, d), w_a),
                  pl.BlockSpec((1, d, d_e), w_b), pl.BlockSpec((1, d, d_e), w_b),
                  pl.BlockSpec((1, d_e, d), w_b)],
        out_specs=pl.BlockSpec((TM, d), lambda i, ea, eb, nu: (i, 0)),
    )
    return pl.pallas_call(
        functools.partial(_moe_kernel, d=d),
        grid_spec=grid_spec,
        out_shape=jax.ShapeDtypeStruct((n_slots, d), F32),
        compiler_params=_cparams(("arbitrary",)),
        name="moe_experts",
    )(tile_ea, tile_eb, n_used, xs, w_gate, w_up, w_down, w_gate, w_up, w_down)


def _combine_final_kernel(pos_ref, ys_hbm, x1_ref, g_ref, b_ref, yp_ref, yt_ref, ybuf, sem,
                          *, alpha, d, n_prompt_tiles):
    def start(t, carry):
        _row_copy(ys_hbm, ybuf, pos_ref[0, 0, t], t, sem).start()
        return carry

    lax.fori_loop(0, TM, start, 0, unroll=ROW_DMA_UNROLL)
    pltpu.make_async_copy(ys_hbm.at[pl.ds(0, TM)], ybuf, sem).wait()
    x2 = _layer_norm(alpha * x1_ref[:, :d] + ybuf[...], g_ref[...], b_ref[...])
    is_prompt = pl.program_id(0) < n_prompt_tiles

    @pl.when(is_prompt)
    def _():
        yp_ref[...] = x2

    @pl.when(jnp.logical_not(is_prompt))
    def _():
        yt_ref[...] = x2


def _combine_final(pos3, ys, x1g, g, b, *, alpha, n_prompt_tiles):
    t_pad, width = x1g.shape
    d = width - GATE_LANES
    n_tiles = t_pad // TM
    last_p = n_prompt_tiles - 1
    const = lambda i: (0, 0)
    return pl.pallas_call(
        functools.partial(_combine_final_kernel, alpha=alpha, d=d, n_prompt_tiles=n_prompt_tiles),
        grid=(n_tiles,),
        in_specs=[pl.BlockSpec((1, 1, TM), lambda i: (i, 0, 0), memory_space=pltpu.SMEM),
                  pl.BlockSpec(memory_space=pl.ANY),
                  pl.BlockSpec((TM, width), lambda i: (i, 0)),
                  pl.BlockSpec(g.shape, const), pl.BlockSpec(b.shape, const)],
        out_specs=[pl.BlockSpec((TM, d), lambda i: (jnp.minimum(i, last_p), 0)),
                   pl.BlockSpec((TM, d), lambda i: (jnp.maximum(i - n_prompt_tiles, 0), 0))],
        out_shape=[jax.ShapeDtypeStruct((n_prompt_tiles * TM, d), F32),
                   jax.ShapeDtypeStruct(((n_tiles - n_prompt_tiles) * TM, d), F32)],
        scratch_shapes=[pltpu.VMEM((TM, d), F32), pltpu.SemaphoreType.DMA(())],
        compiler_params=_cparams(("arbitrary",)),
        name="moe_combine_ln_final",
    )(pos3, ys, x1g, g, b)


def _combine_inproj_kernel(pos_ref, posn_ref, ys_hbm, x1_ref, g_ref, b_ref, w_ref,
                           x2_ref, h_ref, kv_ref, ybuf, sems,
                           *, alpha, d, n_tiles, tiles_per_seq, n_prompt_tiles):
    i = pl.program_id(0)
    slot = i % 2

    def gather(p_ref, s):
        def start(t, carry):
            _row_copy(ys_hbm, ybuf.at[s], p_ref[0, 0, t], t, sems.at[s]).start()
            return carry
        lax.fori_loop(0, TM, start, 0, unroll=ROW_DMA_UNROLL)

    def wait(s):
        pltpu.make_async_copy(ys_hbm.at[pl.ds(0, TM)], ybuf.at[s], sems.at[s]).wait()

    @pl.when(i == 0)
    def _():
        gather(pos_ref, 0)

    gather(posn_ref, 1 - slot)
    wait(slot)
    x2 = _layer_norm(alpha * x1_ref[:, :d] + ybuf[slot], g_ref[...], b_ref[...])
    x2_ref[...] = x2
    acc = jnp.dot(x2.astype(BF16), w_ref[...], preferred_element_type=F32)
    h_ref[...] = acc.astype(BF16)
    keeps_kv = jnp.logical_or(i % tiles_per_seq == tiles_per_seq - 1, i >= n_prompt_tiles)

    @pl.when(keeps_kv)
    def _():
        kv_ref[:, : 2 * WIDTH_A] = acc[:, COL_KA:COL_QB]
        kv_ref[:, 2 * WIDTH_A:] = acc[:, COL_KB:]

    @pl.when(i == n_tiles - 1)
    def _():
        wait(1 - slot)


def _combine_inproj(pos3, ys, x1g, g, b, w_in, *, alpha, seq, n_prompt_tiles):
    t_pad, width = x1g.shape
    d = width - GATE_LANES
    n_tiles = t_pad // TM
    tiles_per_seq = seq // TM
    n_seq = n_prompt_tiles // tiles_per_seq
    n_kv_blocks = n_seq + (n_tiles - n_prompt_tiles)
    const = lambda i: (0, 0)

    def kv_map(i):
        return (jnp.where(i < n_prompt_tiles, i // tiles_per_seq, n_seq + i - n_prompt_tiles), 0)

    return pl.pallas_call(
        functools.partial(_combine_inproj_kernel, alpha=alpha, d=d, n_tiles=n_tiles,
                          tiles_per_seq=tiles_per_seq, n_prompt_tiles=n_prompt_tiles),
        grid=(n_tiles,),
        in_specs=[pl.BlockSpec((1, 1, TM), lambda i: (i, 0, 0), memory_space=pltpu.SMEM),
                  pl.BlockSpec((1, 1, TM), lambda i: (jnp.minimum(i + 1, n_tiles - 1), 0, 0),
                               memory_space=pltpu.SMEM),
                  pl.BlockSpec(memory_space=pl.ANY),
                  pl.BlockSpec((TM, width), lambda i: (i, 0)),
                  pl.BlockSpec(g.shape, const), pl.BlockSpec(b.shape, const),
                  pl.BlockSpec(w_in.shape, const)],
        out_specs=[pl.BlockSpec((TM, d), lambda i: (i, 0)),
                   pl.BlockSpec((TM, D_IN), lambda i: (i, 0)),
                   pl.BlockSpec((TM, KV_OUT), kv_map)],
        out_shape=[jax.ShapeDtypeStruct((t_pad, d), F32),
                   jax.ShapeDtypeStruct((t_pad, D_IN), BF16),
                   jax.ShapeDtypeStruct((n_kv_blocks * TM, KV_OUT), F32)],
        scratch_shapes=[pltpu.VMEM((2, TM, d), F32), pltpu.SemaphoreType.DMA((2,))],
        compiler_params=_cparams(("arbitrary",)),
        name="moe_combine_ln_inproj",
    )(pos3, pos3, ys, x1g, g, b, w_in)


def _alibi_slopes():
    return 2.0 ** (-8.0 * jnp.arange(1, N_HEADS_B + 1, dtype=F32) / N_HEADS_B)


def _rel_bias_table(rel_bias, q0, n_q, n_k):
    n_diag = n_q + n_k - 1
    dist = q0 + (n_q - 1) - jnp.arange(n_diag, dtype=jnp.int32)
    diag = jnp.take(rel_bias.astype(F32), jnp.clip(dist, -REL_CLIP, REL_CLIP) + REL_CLIP, axis=1)
    n_h = diag.shape[0]
    ext = jnp.concatenate([diag, jnp.zeros((n_h, 1), F32)], axis=1)
    shifted = jnp.tile(ext, (1, n_q))[:, :n_q * n_diag].reshape(n_h, n_q, n_diag)
    return shifted[:, :, n_q - 1:n_q - 1 + n_k]


def _alibi_table(q0, n_q, n_k):
    dist = q0 + jnp.arange(n_q, dtype=jnp.int32)[:, None] - jnp.arange(n_k, dtype=jnp.int32)[None, :]
    return -_alibi_slopes()[:, None, None] * jnp.abs(dist).astype(F32)[None]


def kernel(x_prompt, x_sample, cache_a_k, cache_a_v, cache_b_k, cache_b_v, w_in, rel_bias, attn_sinks,
           gn_a, gn_b, w_out, ln1_g, ln1_b, w_router, b_router, w_gate, w_up, w_down, ln2_g, ln2_b):
    n_seq, seq, d = x_prompt.shape
    n_dec, dec_seq, _ = x_sample.shape
    depth = w_in.shape[0]
    rows_ca, rows_cb = cache_a_k.shape[2], cache_b_k.shape[2]
    alpha = (2.0 * depth) ** 0.25
    assert seq % TM == 0 and PAD_A == TM and n_dec * dec_seq <= TM and TM % dec_seq == 0
    assert dec_seq % 16 == 0 and seq >= PAD_A

    t_prompt = n_seq * seq
    t_real = t_prompt + n_dec * dec_seq
    t_pad = -(-t_real // (2 * TM)) * (2 * TM)
    n_prompt_tiles = t_prompt // TM
    n_slot_tiles = t_pad // TM + N_CLASSES

    x = jnp.concatenate([x_prompt.reshape(t_prompt, d), x_sample.reshape(n_dec * dec_seq, d),
                         jnp.zeros((t_pad - t_real, d), F32)], axis=0)

    w_in_b = w_in.astype(BF16)
    w_out_b = w_out.astype(BF16)
    w_gate_b, w_up_b, w_down_b = w_gate.astype(BF16), w_up.astype(BF16), w_down.astype(BF16)
    wr_t = w_router.astype(F32).T
    br = b_router.astype(F32).reshape(N_EXPERTS, 1)

    alibi_p = _alibi_table(PAD_B, CHUNK, BAND_B * CHUNK)
    alibi_s = _alibi_table(rows_cb, dec_seq, rows_cb + dec_seq)

    pa_k, pa_v, pb_k, pb_v, sa_k, sa_v, sb_k, sb_v = ([] for _ in range(8))
    h, kv = _inproj(x, w_in_b[0], seq=seq, n_prompt_tiles=n_prompt_tiles)
    for l in range(depth):
        sinks = attn_sinks[l].astype(F32)
        gna = gn_a[l].astype(F32).reshape(1, WIDTH_A)
        gnb = gn_b[l].astype(F32).reshape(1, WIDTH_B)
        bias_pa = _rel_bias_table(rel_bias[l], PAD_A, CHUNK, BAND_A * CHUNK)
        a_prompt = _attn_prompt(h, sinks, bias_pa, alibi_p, gna, gnb, n_seq=n_seq, seq=seq)
        bias_sa = _rel_bias_table(rel_bias[l], rows_ca, dec_seq, rows_ca + dec_seq)
        bias_s = (bias_sa[:, :, :rows_ca], bias_sa[:, :, rows_ca:],
                  alibi_s[:, :, :rows_cb], alibi_s[:, :, rows_cb:])
        a_sample = _attn_sample(
            h, sinks,
            cache_a_k[l].reshape(n_dec, rows_ca, WIDTH_A), cache_a_v[l].reshape(n_dec, rows_ca, WIDTH_A),
            cache_b_k[l].reshape(n_dec, rows_cb, KV_WIDTH_B), cache_b_v[l].reshape(n_dec, rows_cb, KV_WIDTH_B),
            bias_s, gna, gnb, first_row=t_prompt, n_rows_out=t_pad - t_prompt, n_dec=n_dec, dec_seq=dec_seq)

        x1g, cls = _outproj(a_prompt, a_sample, x, w_out_b[l],
                            ln1_g[l].astype(F32).reshape(1, d), ln1_b[l].astype(F32).reshape(1, d),
                            wr_t, br, n_prompt_tiles=n_prompt_tiles, alpha=alpha)
        pos, tile_ea, tile_eb, n_used, pad_lo, pad_hi = _slots(
            cls.reshape(t_pad // V7X_LANES, V7X_LANES), n_slot_tiles=n_slot_tiles)
        pos3 = pos.reshape(t_pad // TM, 1, TM)
        xs = _dispatch(pos3, pad_lo[0], pad_hi[0], n_used[0], x1g, n_slot_tiles=n_slot_tiles)
        ys = _moe(tile_ea[0, :n_slot_tiles], tile_eb[0, :n_slot_tiles], n_used[0, :1],
                  xs, w_gate_b[l], w_up_b[l], w_down_b[l])
        kv_l = kv
        g2, b2 = ln2_g[l].astype(F32).reshape(1, d), ln2_b[l].astype(F32).reshape(1, d)
        if l + 1 < depth:
            x, h, kv = _combine_inproj(pos3, ys, x1g, g2, b2, w_in_b[l + 1], alpha=alpha, seq=seq,
                                       n_prompt_tiles=n_prompt_tiles)
        else:
            y_p, y_t = _combine_final(pos3, ys, x1g, g2, b2, alpha=alpha, n_prompt_tiles=n_prompt_tiles)

        kv_p = kv_l[:n_seq * TM].reshape(n_seq, TM, KV_OUT)
        kv_s = kv_l[n_seq * TM:n_seq * TM + n_dec * dec_seq].reshape(n_dec, dec_seq, KV_OUT)
        ra, rb = min(PAD_A, seq), min(WINDOW_B, seq)
        pa_k.append(kv_p[:, TM - ra:, :WIDTH_A].reshape(n_seq, ra, N_HEADS_A, HEAD_DIM))
        pa_v.append(kv_p[:, TM - ra:, WIDTH_A:2 * WIDTH_A].reshape(n_seq, ra, N_HEADS_A, HEAD_DIM))
        pb_k.append(kv_p[:, TM - rb:, 2 * WIDTH_A:2 * WIDTH_A + KV_WIDTH_B].reshape(n_seq, rb, N_KV_B, HEAD_DIM))
        pb_v.append(kv_p[:, TM - rb:, 2 * WIDTH_A + KV_WIDTH_B:].reshape(n_seq, rb, N_KV_B, HEAD_DIM))
        sa_k.append(kv_s[:, :, :WIDTH_A].reshape(n_dec, dec_seq, N_HEADS_A, HEAD_DIM))
        sa_v.append(kv_s[:, :, WIDTH_A:2 * WIDTH_A].reshape(n_dec, dec_seq, N_HEADS_A, HEAD_DIM))
        sb_k.append(kv_s[:, :, 2 * WIDTH_A:2 * WIDTH_A + KV_WIDTH_B].reshape(n_dec, dec_seq, N_KV_B, HEAD_DIM))
        sb_v.append(kv_s[:, :, 2 * WIDTH_A + KV_WIDTH_B:].reshape(n_dec, dec_seq, N_KV_B, HEAD_DIM))

    y_prompt = y_p.reshape(n_seq, seq, d)
    y_sample = y_t[:n_dec * dec_seq].reshape(n_dec, dec_seq, d)
    return (y_prompt, y_sample,
            jnp.stack(pa_k), jnp.stack(pa_v), jnp.stack(pb_k), jnp.stack(pb_v),
            jnp.stack(sa_k), jnp.stack(sa_v), jnp.stack(sb_k), jnp.stack(sb_v))
```

```python
import functools

import jax
import jax.numpy as jnp
import numpy as np
from jax import lax
from jax.experimental import pallas as pl
from jax.experimental.pallas import tpu as pltpu

CHUNK = 64
HEAD_DIM = 64
N_HEADS_A = 8
BAND_A = 9
REL_CLIP = 128
N_HEADS_B = 8
N_KV_B = 2
WINDOW_B = 128
BAND_B = 1 + WINDOW_B // CHUNK
N_EXPERTS = 16
N_GROUPS = 4
EXPERTS_PER_GROUP = N_EXPERTS // N_GROUPS
PAIRS_PER_GROUP = 6
N_CLASSES = N_GROUPS * PAIRS_PER_GROUP
LN_EPS = 1e-5
RMS_EPS = 1e-6

WIDTH_A = N_HEADS_A * HEAD_DIM
WIDTH_B = N_HEADS_B * HEAD_DIM
KV_WIDTH_B = N_KV_B * HEAD_DIM
D_IN = 3 * WIDTH_A + WIDTH_B + 2 * KV_WIDTH_B
COL_QA, COL_KA, COL_VA = 0, WIDTH_A, 2 * WIDTH_A
COL_QB = 3 * WIDTH_A
COL_KB = COL_QB + WIDTH_B
COL_VB = COL_KB + KV_WIDTH_B
KV_OUT = 2 * WIDTH_A + 2 * KV_WIDTH_B
PAD_A = (BAND_A - 1) * CHUNK
PAD_B = WINDOW_B

V7X_LANES = 128
TM = 512
GATE_LANES = V7X_LANES
V7X_VMEM_LIMIT = 56 * 1024 * 1024

F32 = jnp.float32
BF16 = jnp.bfloat16
NEG_INF = float("-inf")


def _cparams(sem):
    return pltpu.CompilerParams(dimension_semantics=sem, vmem_limit_bytes=V7X_VMEM_LIMIT)


def _inproj_kernel(x_ref, w_ref, h_ref, kv_ref, *, tiles_per_seq, n_prompt_tiles):
    i = pl.program_id(0)
    acc = jnp.dot(x_ref[...].astype(BF16), w_ref[...], preferred_element_type=F32)
    h_ref[...] = acc.astype(BF16)
    keeps_kv = jnp.logical_or(i % tiles_per_seq == tiles_per_seq - 1, i >= n_prompt_tiles)

    @pl.when(keeps_kv)
    def _():
        kv_ref[:, : 2 * WIDTH_A] = acc[:, COL_KA:COL_QB]
        kv_ref[:, 2 * WIDTH_A:] = acc[:, COL_KB:]


def _inproj(x, w_in, *, seq, n_prompt_tiles):
    t_pad, d = x.shape
    n_tiles = t_pad // TM
    tiles_per_seq = seq // TM
    n_seq = n_prompt_tiles // tiles_per_seq
    n_kv_blocks = n_seq + (n_tiles - n_prompt_tiles)

    def kv_map(i):
        return (jnp.where(i < n_prompt_tiles, i // tiles_per_seq, n_seq + i - n_prompt_tiles), 0)

    return pl.pallas_call(
        functools.partial(_inproj_kernel, tiles_per_seq=tiles_per_seq, n_prompt_tiles=n_prompt_tiles),
        grid=(n_tiles,),
        in_specs=[pl.BlockSpec((TM, d), lambda i: (i, 0)),
                  pl.BlockSpec((d, D_IN), lambda i: (0, 0))],
        out_specs=[pl.BlockSpec((TM, D_IN), lambda i: (i, 0)),
                   pl.BlockSpec((TM, KV_OUT), kv_map)],
        out_shape=[jax.ShapeDtypeStruct((t_pad, D_IN), BF16),
                   jax.ShapeDtypeStruct((n_kv_blocks * TM, KV_OUT), F32)],
        compiler_params=_cparams(("arbitrary",)),
        name="inproj",
    )(x, w_in)


def _softmax_pv(s_parts, v_parts, sink=None):
    m = s_parts[0].max(axis=-1, keepdims=True)
    for s in s_parts[1:]:
        m = jnp.maximum(m, s.max(axis=-1, keepdims=True))
    if sink is not None:
        m = jnp.maximum(m, sink)
    denom = None
    acc = None
    for s, v in zip(s_parts, v_parts):
        e = jnp.exp(s - m)
        d = e.sum(axis=-1, keepdims=True)
        pv = jnp.dot(e.astype(BF16), v, preferred_element_type=F32)
        denom = d if denom is None else denom + d
        acc = pv if acc is None else acc + pv
    if sink is not None:
        denom = denom + jnp.exp(sink - m)
    return acc / denom


def _nt_dot(a, b):
    return lax.dot_general(a, b, (((1,), (1,)), ((), ())), preferred_element_type=F32)


def _head_pair_lane_mask(rows):
    return lax.broadcasted_iota(jnp.int32, (rows, 2 * HEAD_DIM), 1) < HEAD_DIM


def _rms_store(o_ref, row_slice, col0, pairs, gn_ref):
    ssq = None
    for o in pairs:
        s = (o * o).sum(axis=-1, keepdims=True)
        ssq = s if ssq is None else ssq + s
    width = len(pairs) * 2 * HEAD_DIM
    inv = lax.rsqrt(ssq / width + RMS_EPS)
    for p, o in enumerate(pairs):
        c0 = p * 2 * HEAD_DIM
        g = gn_ref[:, c0:c0 + 2 * HEAD_DIM]
        o_ref[row_slice, col0 + c0:col0 + c0 + 2 * HEAD_DIM] = (o * inv * g).astype(o_ref.dtype)


KEY_BLOCK = V7X_LANES


def _swap_halves(x):
    return jnp.concatenate([x[:, HEAD_DIM:], x[:, :HEAD_DIM]], axis=1)


def _pair_rows(q2, lane_lo, kv_half=None):
    hi_lanes = jnp.logical_not(lane_lo)
    zero = jnp.zeros_like(q2)
    if kv_half is None:
        lo, hi = jnp.where(lane_lo, q2, zero), jnp.where(hi_lanes, q2, zero)
    elif kv_half == 0:
        lo, hi = jnp.where(lane_lo, q2, zero), jnp.where(lane_lo, _swap_halves(q2), zero)
    else:
        lo, hi = jnp.where(hi_lanes, _swap_halves(q2), zero), jnp.where(hi_lanes, q2, zero)
    return jnp.concatenate([lo, hi], axis=0)


def _tn_dot(a, b):
    return lax.dot_general(a, b, (((0,), (0,)), ((), ())), preferred_element_type=F32)


PV_KEYS = 256


def _score_blocks(k_ref, lanes, row0, n_keys, w):
    out = []
    for off in range(0, n_keys, KEY_BLOCK):
        n = min(KEY_BLOCK, n_keys - off)
        start = pl.multiple_of(row0 + off, CHUNK)
        out.append((_nt_dot(k_ref[pl.ds(start, n), lanes], w), start, n, off))
    return out


def _pair_probs(score_blocks, lane_tile, bias_ref, pair, pad, masked, sink_block=None):
    lanes = slice(lane_tile * V7X_LANES, (lane_tile + 1) * V7X_LANES)
    blocks = []
    for s_all, start, n, off in score_blocks:
        s = s_all[:, lanes] + bias_ref[pair, off:off + n, :]
        if masked:
            key_row = lax.broadcasted_iota(jnp.int32, s.shape, 0) + start
            s = jnp.where(key_row >= pad, s, NEG_INF)
        blocks.append((s, start, n))
    m = None
    for s, _, _ in blocks:
        bm = s.max(axis=0, keepdims=True)
        m = bm if m is None else jnp.maximum(m, bm)
    if sink_block is not None:
        m = jnp.maximum(m, sink_block.max(axis=0, keepdims=True))
    probs = [(jnp.exp(s - m).astype(BF16), start, n) for s, start, n in blocks]
    sink_p = None if sink_block is None else jnp.exp(sink_block - m).astype(BF16)
    return probs, sink_p


def _pair_values(probs, sink_p, v_ref, lanes):
    groups, cur, cur_rows = [], [], 0
    for blk in probs:
        if cur and cur_rows + blk[2] > PV_KEYS:
            groups.append(cur)
            cur, cur_rows = [], 0
        cur.append(blk)
        cur_rows += blk[2]
    groups.append(cur)
    acc = None
    for gi, grp in enumerate(groups):
        rows = sum(n for _, _, n in grp)
        p = grp[0][0] if len(grp) == 1 else jnp.concatenate([b[0] for b in grp], axis=0)
        v1 = jnp.concatenate([v_ref[pl.ds(grp[0][1], rows), lanes], jnp.ones((rows, V7X_LANES), BF16)], axis=1)
        if sink_p is not None and gi == len(groups) - 1:
            n = sink_p.shape[0]
            p = jnp.concatenate([p, sink_p], axis=0)
            sink_v = jnp.concatenate([jnp.zeros((n, V7X_LANES), BF16), jnp.ones((n, V7X_LANES), BF16)], axis=1)
            v1 = jnp.concatenate([v1, sink_v], axis=0)
        c = _tn_dot(p, v1)
        acc = c if acc is None else acc + c
    return acc[:, :V7X_LANES] / acc[:, V7X_LANES:]


def _quad_rows(q4):
    head_of_lane = lax.broadcasted_iota(jnp.int32, q4.shape, 1) // HEAD_DIM
    zero = jnp.zeros_like(q4)
    return jnp.concatenate([jnp.where(head_of_lane == h, q4, zero) for h in range(4)], axis=0)


def _attn_prompt_kernel(qa_ref, ka_ref, va_ref, qb_ref, kb_ref, vb_ref,
                        ba_ref, bb_ref, sink_ref, gna_ref, gnb_ref, o_ref,
                        kpa, vpa, kpb, vpb, pa_even, pb_even, pa_odd, pb_odd, *, seq, tq):
    j = pl.program_id(1)

    @pl.when(j == 0)
    def _():
        kpa[0:PAD_A, :] = jnp.zeros((PAD_A, WIDTH_A), BF16)
        vpa[0:PAD_A, :] = jnp.zeros((PAD_A, WIDTH_A), BF16)
        kpb[0:PAD_B, :] = jnp.zeros((PAD_B, KV_WIDTH_B), BF16)
        vpb[0:PAD_B, :] = jnp.zeros((PAD_B, KV_WIDTH_B), BF16)
        kpa[PAD_A:PAD_A + seq, :] = ka_ref[...]
        vpa[PAD_A:PAD_A + seq, :] = va_ref[...]
        kpb[PAD_B:PAD_B + seq, :] = kb_ref[...]
        vpb[PAD_B:PAD_B + seq, :] = vb_ref[...]

    lane_lo = _head_pair_lane_mask(CHUNK)
    scale = HEAD_DIM ** -0.5
    all_lanes = slice(0, KV_WIDTH_B)

    n_pairs_a, n_pairs_b = N_HEADS_A // 2, N_HEADS_B // 2
    band_a, band_b = BAND_A * CHUNK, BAND_B * CHUNK
    n_sink = sink_ref.shape[1]
    even_bufs, odd_bufs = (pa_even, pb_even), (pa_odd, pb_odd)

    def band_start(c):
        return pl.multiple_of(j * tq + c * CHUNK, CHUNK)

    def probs_phase(c, masked, bufs):
        pa_buf, pb_buf = bufs
        rows = pl.ds(pl.multiple_of(c * CHUNK, CHUNK), CHUNK)
        cs = band_start(c)
        scores_a = []
        for quad in range(N_HEADS_A // 4):
            lanes = slice(quad * 4 * HEAD_DIM, (quad + 1) * 4 * HEAD_DIM)
            w = _quad_rows(qa_ref[rows, lanes] * scale)
            scores_a.append(_score_blocks(kpa, lanes, cs, band_a, w))
        w = jnp.concatenate(
            [_pair_rows(qb_ref[rows, p * 2 * HEAD_DIM:(p + 1) * 2 * HEAD_DIM] * scale, lane_lo,
                        kv_half=(2 * p) // (N_HEADS_B // N_KV_B))
             for p in range(n_pairs_b)], axis=0)
        scores_b = _score_blocks(kpb, all_lanes, cs, band_b, w)
        for p in range(n_pairs_a):
            probs, _ = _pair_probs(scores_a[p // 2], p % 2, ba_ref, p, PAD_A, masked)
            for blk, (_, _, n, off) in zip(probs, scores_a[p // 2]):
                pa_buf[p, off:off + n, :] = blk[0]
        for p in range(n_pairs_b):
            probs, sink_p = _pair_probs(scores_b, p, bb_ref, p, PAD_B, masked, sink_block=sink_ref[p])
            for blk, (_, _, n, off) in zip(probs, scores_b):
                pb_buf[p, off:off + n, :] = blk[0]
            pb_buf[p, band_b:band_b + n_sink, :] = sink_p

    def values_phase(c, bufs):
        pa_buf, pb_buf = bufs
        rows = pl.ds(pl.multiple_of(c * CHUNK, CHUNK), CHUNK)
        cs = band_start(c)

        def groups(buf, p, n_keys):
            return [(buf[p, off:off + min(PV_KEYS, n_keys - off), :], pl.multiple_of(cs + off, CHUNK),
                     min(PV_KEYS, n_keys - off)) for off in range(0, n_keys, PV_KEYS)]

        pairs_a = []
        for p in range(n_pairs_a):
            lanes = slice(p * 2 * HEAD_DIM, (p + 1) * 2 * HEAD_DIM)
            o = _pair_values(groups(pa_buf, p, band_a), None, vpa, lanes)
            pairs_a.append(jnp.where(lane_lo, o[:CHUNK], o[CHUNK:]))
        _rms_store(o_ref, rows, 0, pairs_a, gna_ref)

        pairs_b = []
        for p in range(n_pairs_b):
            kv_half = (2 * p) // (N_HEADS_B // N_KV_B)
            o = _pair_values(groups(pb_buf, p, band_b), pb_buf[p, band_b:band_b + n_sink, :], vpb, all_lanes)
            lo, hi = o[:CHUNK], o[CHUNK:]
            if kv_half == 0:
                hi = _swap_halves(hi)
            else:
                lo = _swap_halves(lo)
            pairs_b.append(jnp.where(lane_lo, lo, hi))
        _rms_store(o_ref, rows, WIDTH_A, pairs_b, gnb_ref)

    def run(masked):
        n_chunks = tq // CHUNK
        bufs = (even_bufs, odd_bufs)
        probs_phase(0, masked, even_bufs)
        for c in range(1, n_chunks):
            probs_phase(c, masked, bufs[c % 2])
            values_phase(c - 1, bufs[(c - 1) % 2])
        values_phase(n_chunks - 1, bufs[(n_chunks - 1) % 2])

    @pl.when(j == 0)
    def _():
        run(True)

    @pl.when(j > 0)
    def _():
        run(False)


def _pair_transposed(table):
    n_h, n_q, n_k = table.shape
    return table.reshape(n_h // 2, 2, n_q, n_k).transpose(0, 3, 1, 2).reshape(n_h // 2, n_k, 2 * n_q)


def _sink_blocks(sinks):
    row = jnp.repeat(sinks.astype(F32).reshape(N_HEADS_B // 2, 2), HEAD_DIM, axis=1)
    rest = jnp.full((N_HEADS_B // 2, 15, 2 * HEAD_DIM), NEG_INF, F32)
    return jnp.concatenate([row[:, None, :], rest], axis=1)


def _attn_prompt(h, sinks, bias_a, bias_b, gn_a, gn_b, *, n_seq, seq):
    tq = TM
    assert tq >= PAD_A and tq >= PAD_B and tq % CHUNK == 0
    n_q = seq // tq
    vmem = pltpu.VMEM
    bias_a, bias_b = _pair_transposed(bias_a), _pair_transposed(bias_b)
    sinks = _sink_blocks(sinks)
    return pl.pallas_call(
        functools.partial(_attn_prompt_kernel, seq=seq, tq=tq),
        grid=(n_seq, n_q),
        in_specs=[
            pl.BlockSpec((tq, WIDTH_A), lambda b, j: (b * n_q + j, COL_QA // WIDTH_A)),
            pl.BlockSpec((seq, WIDTH_A), lambda b, j: (b, COL_KA // WIDTH_A)),
            pl.BlockSpec((seq, WIDTH_A), lambda b, j: (b, COL_VA // WIDTH_A)),
            pl.BlockSpec((tq, WIDTH_B), lambda b, j: (b * n_q + j, COL_QB // WIDTH_B)),
            pl.BlockSpec((seq, KV_WIDTH_B), lambda b, j: (b, COL_KB // KV_WIDTH_B)),
            pl.BlockSpec((seq, KV_WIDTH_B), lambda b, j: (b, COL_VB // KV_WIDTH_B)),
            pl.BlockSpec(bias_a.shape, lambda b, j: (0, 0, 0)),
            pl.BlockSpec(bias_b.shape, lambda b, j: (0, 0, 0)),
            pl.BlockSpec(sinks.shape, lambda b, j: (0, 0, 0)),
            pl.BlockSpec(gn_a.shape, lambda b, j: (0, 0)),
            pl.BlockSpec(gn_b.shape, lambda b, j: (0, 0)),
        ],
        out_specs=pl.BlockSpec((tq, WIDTH_A + WIDTH_B), lambda b, j: (b * n_q + j, 0)),
        out_shape=jax.ShapeDtypeStruct((n_seq * seq, WIDTH_A + WIDTH_B), BF16),
        scratch_shapes=[vmem((PAD_A + seq, WIDTH_A), BF16), vmem((PAD_A + seq, WIDTH_A), BF16),
                        vmem((PAD_B + seq, KV_WIDTH_B), BF16), vmem((PAD_B + seq, KV_WIDTH_B), BF16),
                        vmem((N_HEADS_A // 2, BAND_A * CHUNK, V7X_LANES), BF16),
                        vmem((N_HEADS_B // 2, BAND_B * CHUNK + sinks.shape[1], V7X_LANES), BF16),
                        vmem((N_HEADS_A // 2, BAND_A * CHUNK, V7X_LANES), BF16),
                        vmem((N_HEADS_B // 2, BAND_B * CHUNK + sinks.shape[1], V7X_LANES), BF16)],
        compiler_params=_cparams(("arbitrary", "arbitrary")),
        name="attn_prompt",
    )(h, h, h, h, h, h, bias_a, bias_b, sinks, gn_a, gn_b)


def _attn_sample_kernel(sink_ref, qa_ref, ka_ref, va_ref, qb_ref, kb_ref, vb_ref,
                        cka_ref, cva_ref, ckb_ref, cvb_ref,
                        bac_ref, ban_ref, bbc_ref, bbn_ref, gna_ref, gnb_ref, o_ref, *, n_dec, dec_seq):
    b = pl.program_id(0)

    @pl.when(b >= n_dec)
    def _():
        o_ref[...] = jnp.zeros(o_ref.shape, o_ref.dtype)

    @pl.when(b < n_dec)
    def _():
        lane_lo = _head_pair_lane_mask(dec_seq)
        scale = HEAD_DIM ** -0.5
        rows = slice(0, dec_seq)

        pairs_a = []
        for p in range(N_HEADS_A // 2):
            lanes = slice(p * 2 * HEAD_DIM, (p + 1) * 2 * HEAD_DIM)
            q2 = qa_ref[:, lanes] * scale
            kc = cka_ref[0, :, lanes].astype(BF16)
            vc = cva_ref[0, :, lanes].astype(BF16)
            kn = ka_ref[:, lanes]
            vn = va_ref[:, lanes]
            halves = []
            for half in range(2):
                h = 2 * p + half
                keep = lane_lo if half == 0 else jnp.logical_not(lane_lo)
                qm = jnp.where(keep, q2, jnp.zeros_like(q2))
                s_c = _nt_dot(qm, kc) + bac_ref[h]
                s_n = _nt_dot(qm, kn) + ban_ref[h]
                halves.append(_softmax_pv([s_c, s_n], [vc, vn]))
            pairs_a.append(jnp.where(lane_lo, halves[0], halves[1]))
        _rms_store(o_ref, rows, 0, pairs_a, gna_ref)

        kc = ckb_ref[0].astype(BF16)
        vc = cvb_ref[0].astype(BF16)
        kn = kb_ref[...]
        vn = vb_ref[...]
        pairs_b = []
        for p in range(N_HEADS_B // 2):
            lanes = slice(p * 2 * HEAD_DIM, (p + 1) * 2 * HEAD_DIM)
            q2 = qb_ref[:, lanes] * scale
            halves = []
            for half in range(2):
                h = 2 * p + half
                kv_head = h // (N_HEADS_B // N_KV_B)
                keep = lane_lo if half == 0 else jnp.logical_not(lane_lo)
                qm = jnp.where(keep, q2, jnp.zeros_like(q2))
                if kv_head != half:
                    qm = jnp.concatenate([qm[:, HEAD_DIM:], qm[:, :HEAD_DIM]], axis=1)
                s_c = _nt_dot(qm, kc) + bbc_ref[h]
                s_n = _nt_dot(qm, kn) + bbn_ref[h]
                o = _softmax_pv([s_c, s_n], [vc, vn], sink=sink_ref[h])
                if kv_head != half:
                    o = jnp.concatenate([o[:, HEAD_DIM:], o[:, :HEAD_DIM]], axis=1)
                halves.append(o)
            pairs_b.append(jnp.where(lane_lo, halves[0], halves[1]))
        _rms_store(o_ref, rows, WIDTH_A, pairs_b, gnb_ref)


def _attn_sample(h, sinks, cka, cva, ckb, cvb, bias, gn_a, gn_b, *, first_row, n_rows_out, n_dec, dec_seq):
    bac, ban, bbc, bbn = bias
    n_steps = n_rows_out // dec_seq
    rb = first_row // dec_seq
    last = n_dec - 1

    def hrow(b):
        return rb + jnp.minimum(b, last)

    def cache_spec(c):
        return pl.BlockSpec((1,) + c.shape[1:], lambda b: (jnp.minimum(b, last), 0, 0))

    def const_spec(a):
        return pl.BlockSpec(a.shape, lambda b: (0,) * a.ndim)

    return pl.pallas_call(
        functools.partial(_attn_sample_kernel, n_dec=n_dec, dec_seq=dec_seq),
        grid=(n_steps,),
        in_specs=[
            pl.BlockSpec(memory_space=pltpu.SMEM),
            pl.BlockSpec((dec_seq, WIDTH_A), lambda b: (hrow(b), COL_QA // WIDTH_A)),
            pl.BlockSpec((dec_seq, WIDTH_A), lambda b: (hrow(b), COL_KA // WIDTH_A)),
            pl.BlockSpec((dec_seq, WIDTH_A), lambda b: (hrow(b), COL_VA // WIDTH_A)),
            pl.BlockSpec((dec_seq, WIDTH_B), lambda b: (hrow(b), COL_QB // WIDTH_B)),
            pl.BlockSpec((dec_seq, KV_WIDTH_B), lambda b: (hrow(b), COL_KB // KV_WIDTH_B)),
            pl.BlockSpec((dec_seq, KV_WIDTH_B), lambda b: (hrow(b), COL_VB // KV_WIDTH_B)),
            cache_spec(cka), cache_spec(cva), cache_spec(ckb), cache_spec(cvb),
            const_spec(bac), const_spec(ban), const_spec(bbc), const_spec(bbn),
            const_spec(gn_a), const_spec(gn_b),
        ],
        out_specs=pl.BlockSpec((dec_seq, WIDTH_A + WIDTH_B), lambda b: (b, 0)),
        out_shape=jax.ShapeDtypeStruct((n_rows_out, WIDTH_A + WIDTH_B), BF16),
        compiler_params=_cparams(("arbitrary",)),
        name="attn_sample",
    )(sinks, h, h, h, h, h, h, cka, cva, ckb, cvb, bac, ban, bbc, bbn, gn_a, gn_b)


def _layer_norm(r, g, b):
    mu = r.mean(axis=-1, keepdims=True)
    c = r - mu
    var = (c * c).mean(axis=-1, keepdims=True)
    return c * lax.rsqrt(var + LN_EPS) * g + b


def _route_rows(logits):
    rows = [logits[e:e + 1, :] for e in range(N_EXPERTS)]

    def top2_sum(a, b, c, d):
        hi1, lo1 = jnp.maximum(a, b), jnp.minimum(a, b)
        hi2, lo2 = jnp.maximum(c, d), jnp.minimum(c, d)
        return jnp.maximum(hi1, hi2) + jnp.maximum(jnp.minimum(hi1, hi2), jnp.maximum(lo1, lo2))

    scores = [top2_sum(*rows[4 * g:4 * g + 4]) for g in range(N_GROUPS)]
    best = scores[0]
    g_sel = jnp.zeros(best.shape, jnp.int32)
    for g in range(1, N_GROUPS):
        upd = scores[g] > best
        best = jnp.where(upd, scores[g], best)
        g_sel = jnp.where(upd, g, g_sel)

    vals = []
    for k in range(EXPERTS_PER_GROUP):
        v = rows[k]
        for g in range(1, N_GROUPS):
            v = jnp.where(g_sel == g, rows[4 * g + k], v)
        vals.append(v)

    v1 = vals[0]
    i1 = jnp.zeros(v1.shape, jnp.int32)
    for k in range(1, EXPERTS_PER_GROUP):
        upd = vals[k] > v1
        v1 = jnp.where(upd, vals[k], v1)
        i1 = jnp.where(upd, k, i1)
    v2 = jnp.full(v1.shape, NEG_INF, F32)
    i2 = jnp.zeros(v1.shape, jnp.int32)
    for k in range(EXPERTS_PER_GROUP):
        upd = jnp.logical_and(i1 != k, vals[k] > v2)
        v2 = jnp.where(upd, vals[k], v2)
        i2 = jnp.where(upd, k, i2)

    e = jnp.exp(v2 - v1)
    den = 1.0 + e
    w1 = 1.0 / den
    w2 = e / den
    first_is_lo = i1 < i2
    lo = jnp.where(first_is_lo, i1, i2)
    hi = jnp.where(first_is_lo, i2, i1)
    gate_lo = jnp.where(first_is_lo, w1, w2)
    gate_hi = jnp.where(first_is_lo, w2, w1)
    pair_base = jnp.where(lo == 0, 0, jnp.where(lo == 1, 3, 5))
    cls = g_sel * PAIRS_PER_GROUP + pair_base + hi - lo - 1
    return cls, gate_lo, gate_hi


OUTPROJ_SUBTILES = 4


def _outproj_kernel(ap_ref, as_ref, x_ref, w_ref, g_ref, b_ref, wrh_ref, wrl_ref, br_ref,
                    x1_ref, cls_ref, *, n_prompt_tiles, alpha, d):
    i = pl.program_id(0)
    sub = TM // OUTPROJ_SUBTILES
    w = w_ref[...]
    ys = []
    for s in range(OUTPROJ_SUBTILES):
        rows = slice(s * sub, (s + 1) * sub)
        a = jnp.where(i < n_prompt_tiles, ap_ref[rows, :], as_ref[rows, :])
        ys.append(jnp.dot(a, w, preferred_element_type=F32))
    for s in range(OUTPROJ_SUBTILES):
        rows = slice(s * sub, (s + 1) * sub)
        x1 = _layer_norm(alpha * x_ref[rows, :] + ys[s], g_ref[...], b_ref[...])
        x1_ref[rows, :d] = x1
        x_hi = x1.astype(BF16)
        x_lo = (x1 - x_hi.astype(F32)).astype(BF16)
        logits = (_nt_dot(wrh_ref[...], x_hi) + _nt_dot(wrh_ref[...], x_lo) + _nt_dot(wrl_ref[...], x_hi)
                  + br_ref[...])
        cls, gate_lo, gate_hi = _route_rows(logits)
        cls_ref[:, rows] = cls
        gates = jnp.concatenate([gate_lo, gate_hi, jnp.zeros((GATE_LANES - 2, sub), F32)], axis=0)
        x1_ref[rows, d:] = gates.T


def _outproj(a_prompt, a_sample, x, w_out, g, b, wr_hi, wr_lo, br, *, n_prompt_tiles, alpha):
    t_pad, d = x.shape
    n_tiles = t_pad // TM
    last_p = n_prompt_tiles - 1
    const = lambda i: (0, 0)
    return pl.pallas_call(
        functools.partial(_outproj_kernel, n_prompt_tiles=n_prompt_tiles, alpha=alpha, d=d),
        grid=(n_tiles,),
        in_specs=[pl.BlockSpec((TM, d), lambda i: (jnp.minimum(i, last_p), 0)),
                  pl.BlockSpec((TM, d), lambda i: (jnp.maximum(i - n_prompt_tiles, 0), 0)),
                  pl.BlockSpec((TM, d), lambda i: (i, 0)),
                  pl.BlockSpec(w_out.shape, const),
                  pl.BlockSpec(g.shape, const), pl.BlockSpec(b.shape, const),
                  pl.BlockSpec(wr_hi.shape, const), pl.BlockSpec(wr_lo.shape, const),
                  pl.BlockSpec(br.shape, const)],
        out_specs=[pl.BlockSpec((TM, d + GATE_LANES), lambda i: (i, 0)),
                   pl.BlockSpec((1, TM), lambda i: (0, i))],
        out_shape=[jax.ShapeDtypeStruct((t_pad, d + GATE_LANES), F32),
                   jax.ShapeDtypeStruct((1, t_pad), jnp.int32)],
        compiler_params=_cparams(("arbitrary",)),
        name="outproj_ln_route",
    )(a_prompt, a_sample, x, w_out, g, b, wr_hi, wr_lo, br)


def _slots_kernel(cls_ref, pos_ref, ea_ref, eb_ref, nu_ref, padlo_ref, padhi_ref, *, rows_pad):
    lane = lax.broadcasted_iota(jnp.int32, (1, V7X_LANES), 1)
    pad_lo = jnp.zeros((1, V7X_LANES), F32)
    pad_hi = jnp.zeros((1, V7X_LANES), F32)
    cls = cls_ref[...]
    rows = cls.shape[0]
    r_i = lax.broadcasted_iota(jnp.int32, (rows_pad, rows_pad), 0)
    c_i = lax.broadcasted_iota(jnp.int32, (rows_pad, rows_pad), 1)
    lower = jnp.where(c_i < r_i, 1.0, 0.0).astype(BF16)
    l_r = lax.broadcasted_iota(jnp.int32, (V7X_LANES, V7X_LANES), 0)
    l_c = lax.broadcasted_iota(jnp.int32, (V7X_LANES, V7X_LANES), 1)
    upper = jnp.where(l_r < l_c, 1.0, 0.0).astype(BF16)

    n_t = ea_ref.shape[1]
    tile_start = lax.broadcasted_iota(jnp.int32, (1, n_t), 1).astype(F32) * float(TM)
    tile_cls = jnp.zeros((1, n_t), jnp.int32)
    off = jnp.zeros((1, 1), F32)
    pos = jnp.zeros((rows, V7X_LANES), F32)
    for c in range(N_CLASSES):
        m = jnp.where(cls == c, 1.0, 0.0)
        mb = m.astype(BF16)
        if rows_pad > rows:
            mb_pad = jnp.concatenate([mb, jnp.zeros((rows_pad - rows, V7X_LANES), BF16)], axis=0)
        else:
            mb_pad = mb
        before_rows = jnp.dot(lower, mb_pad, preferred_element_type=F32)[:rows]
        before_rows = before_rows.sum(axis=-1, keepdims=True)
        before_lanes = jnp.dot(mb, upper, preferred_element_type=F32)
        pos = pos + m * (off + before_rows + before_lanes)
        if c > 0:
            tile_cls = tile_cls + jnp.where(tile_start >= off, 1, 0)
        count = m.sum(axis=-1, keepdims=True).sum(axis=0, keepdims=True)
        pad_lo = jnp.where(lane == c, off + count, pad_lo)
        off = off + jnp.floor((count + float(TM - 1)) * (1.0 / TM)) * float(TM)
        pad_hi = jnp.where(lane == c, off, pad_hi)
    pos_ref[...] = pos.astype(jnp.int32)
    padlo_ref[...] = pad_lo.astype(jnp.int32)
    padhi_ref[...] = pad_hi.astype(jnp.int32)
    grp = (jnp.where(tile_cls >= 6, 1, 0) + jnp.where(tile_cls >= 12, 1, 0)
           + jnp.where(tile_cls >= 18, 1, 0))
    pair = tile_cls - PAIRS_PER_GROUP * grp
    lo = jnp.where(pair >= 3, 1, 0) + jnp.where(pair >= 5, 1, 0)
    hi = pair + 1 - 2 * jnp.where(pair >= 3, 1, 0) - jnp.where(pair >= 5, 1, 0)
    ea_ref[...] = EXPERTS_PER_GROUP * grp + lo
    eb_ref[...] = EXPERTS_PER_GROUP * grp + hi
    nu_ref[...] = jnp.broadcast_to(off * (1.0 / TM), nu_ref.shape).astype(jnp.int32)


def _slots(cls2d, *, n_slot_tiles):
    rows = cls2d.shape[0]
    rows_pad = -(-rows // V7X_LANES) * V7X_LANES
    n_t = -(-n_slot_tiles // V7X_LANES) * V7X_LANES
    i32 = jnp.int32
    return pl.pallas_call(
        functools.partial(_slots_kernel, rows_pad=rows_pad),
        out_shape=[jax.ShapeDtypeStruct(cls2d.shape, i32),
                   jax.ShapeDtypeStruct((1, n_t), i32), jax.ShapeDtypeStruct((1, n_t), i32),
                   jax.ShapeDtypeStruct((1, V7X_LANES), i32), jax.ShapeDtypeStruct((1, V7X_LANES), i32),
                   jax.ShapeDtypeStruct((1, V7X_LANES), i32)],
        compiler_params=pltpu.CompilerParams(vmem_limit_bytes=V7X_VMEM_LIMIT),
        name="moe_slots",
    )(cls2d)


ROW_DMA_UNROLL = 8


def _row_copy(src, dst, src_row, dst_row, sem):
    return pltpu.make_async_copy(src.at[pl.ds(src_row, 1)], dst.at[pl.ds(dst_row, 1)], sem)


def _dispatch_kernel(pos_ref, padlo_ref, padhi_ref, nu_ref, x_ref, xs_hbm, zbuf, sem, *, n_slot_tiles):
    @pl.when(pl.program_id(0) == 0)
    def _():
        zbuf[...] = jnp.zeros(zbuf.shape, zbuf.dtype)

        def zero_tile(t, carry):
            cp = pltpu.make_async_copy(zbuf, xs_hbm.at[pl.ds(pl.multiple_of(t * TM, TM), TM)], sem)
            cp.start()
            cp.wait()
            return carry

        lax.fori_loop(nu_ref[0], n_slot_tiles, zero_tile, 0)

        for c in range(N_CLASSES):
            lo, hi = padlo_ref[c], padhi_ref[c]

            def zero_row(s, carry):
                _row_copy(zbuf, xs_hbm, 0, s, sem).start()
                return carry

            def zero_wait(s, carry):
                _row_copy(zbuf, xs_hbm, 0, 0, sem).wait()
                return carry

            lax.fori_loop(lo, hi, zero_row, 0)
            lax.fori_loop(lo, hi, zero_wait, 0)

    def start(t, carry):
        _row_copy(x_ref, xs_hbm, t, pos_ref[0, 0, t], sem).start()
        return carry

    lax.fori_loop(0, TM, start, 0, unroll=ROW_DMA_UNROLL)
    pltpu.make_async_copy(x_ref, xs_hbm.at[pl.ds(0, TM)], sem).wait()


def _dispatch(pos3, pad_lo, pad_hi, n_used, x1g, *, n_slot_tiles):
    n_tiles = pos3.shape[0]
    width = x1g.shape[1]
    smem = pl.BlockSpec(memory_space=pltpu.SMEM)
    return pl.pallas_call(
        functools.partial(_dispatch_kernel, n_slot_tiles=n_slot_tiles),
        grid=(n_tiles,),
        in_specs=[pl.BlockSpec((1, 1, TM), lambda i: (i, 0, 0), memory_space=pltpu.SMEM),
                  smem, smem, smem,
                  pl.BlockSpec((TM, width), lambda i: (i, 0))],
        out_specs=pl.BlockSpec(memory_space=pl.ANY),
        out_shape=jax.ShapeDtypeStruct((n_slot_tiles * TM, width), x1g.dtype),
        scratch_shapes=[pltpu.VMEM((TM, width), x1g.dtype), pltpu.SemaphoreType.DMA(())],
        compiler_params=pltpu.CompilerParams(dimension_semantics=("arbitrary",), has_side_effects=True,
                                             vmem_limit_bytes=V7X_VMEM_LIMIT),
        name="moe_dispatch",
    )(pos3, pad_lo, pad_hi, n_used, x1g)


def _moe_kernel(ea_ref, eb_ref, nu_ref, xs_ref, wga, wua, wda, wgb, wub, wdb, ys_ref, *, d):
    del ea_ref, eb_ref
    used = pl.program_id(0) < nu_ref[0]

    @pl.when(jnp.logical_not(used))
    def _():
        ys_ref[...] = jnp.zeros(ys_ref.shape, ys_ref.dtype)

    @pl.when(used)
    def _():
        x = xs_ref[:, :d].astype(BF16)
        gate_a = xs_ref[:, d:d + 1]
        gate_b = xs_ref[:, d + 1:d + 2]

        def expert(wg, wu, wd):
            g = jnp.dot(x, wg[0], preferred_element_type=F32)
            u = jnp.dot(x, wu[0], preferred_element_type=F32)
            h = (g * jax.nn.sigmoid(g)) * u
            return jnp.dot(h.astype(BF16), wd[0], preferred_element_type=F32)

        ys_ref[...] = gate_a * expert(wga, wua, wda) + gate_b * expert(wgb, wub, wdb)


def _moe(tile_ea, tile_eb, n_used, xs, w_gate, w_up, w_down):
    n_slots, width = xs.shape
    d = width - GATE_LANES
    n_tiles = n_slots // TM
    d_e = w_gate.shape[-1]

    def row_map(i, ea, eb, nu):
        return (jnp.minimum(i, nu[0] - 1), 0)

    def w_a(i, ea, eb, nu):
        return (ea[i], 0, 0)

    def w_b(i, ea, eb, nu):
        return (eb[i], 0, 0)

    grid_spec = pltpu.PrefetchScalarGridSpec(
        num_scalar_prefetch=3,
        grid=(n_tiles,),
        in_specs=[pl.BlockSpec((TM, width), row_map),
                  pl.BlockSpec((1, d, d_e), w_a), pl.BlockSpec((1, d, d_e), w_a),
                  pl.BlockSpec((1, d_e, d), w_a),
                  pl.BlockSpec((1, d, d_e), w_b), pl.BlockSpec((1, d, d_e), w_b),
                  pl.BlockSpec((1, d_e, d), w_b)],
        out_specs=pl.BlockSpec((TM, d), lambda i, ea, eb, nu: (i, 0)),
    )
    return pl.pallas_call(
        functools.partial(_moe_kernel, d=d),
        grid_spec=grid_spec,
        out_shape=jax.ShapeDtypeStruct((n_slots, d), F32),
        compiler_params=_cparams(("arbitrary",)),
        name="moe_experts",
    )(tile_ea, tile_eb, n_used, xs, w_gate, w_up, w_down, w_gate, w_up, w_down)


def _combine_final_kernel(pos_ref, ys_hbm, x1_ref, g_ref, b_ref, yp_ref, yt_ref, ybuf, sem,
                          *, alpha, d, n_prompt_tiles):
    def start(t, carry):
        _row_copy(ys_hbm, ybuf, pos_ref[0, 0, t], t, sem).start()
        return carry

    lax.fori_loop(0, TM, start, 0, unroll=ROW_DMA_UNROLL)
    pltpu.make_async_copy(ys_hbm.at[pl.ds(0, TM)], ybuf, sem).wait()
    x2 = _layer_norm(alpha * x1_ref[:, :d] + ybuf[...], g_ref[...], b_ref[...])
    is_prompt = pl.program_id(0) < n_prompt_tiles

    @pl.when(is_prompt)
    def _():
        yp_ref[...] = x2

    @pl.when(jnp.logical_not(is_prompt))
    def _():
        yt_ref[...] = x2


def _combine_final(pos3, ys, x1g, g, b, *, alpha, n_prompt_tiles):
    t_pad, width = x1g.shape
    d = width - GATE_LANES
    n_tiles = t_pad // TM
    last_p = n_prompt_tiles - 1
    const = lambda i: (0, 0)
    return pl.pallas_call(
        functools.partial(_combine_final_kernel, alpha=alpha, d=d, n_prompt_tiles=n_prompt_tiles),
        grid=(n_tiles,),
        in_specs=[pl.BlockSpec((1, 1, TM), lambda i: (i, 0, 0), memory_space=pltpu.SMEM),
                  pl.BlockSpec(memory_space=pl.ANY),
                  pl.BlockSpec((TM, width), lambda i: (i, 0)),
                  pl.BlockSpec(g.shape, const), pl.BlockSpec(b.shape, const)],
        out_specs=[pl.BlockSpec((TM, d), lambda i: (jnp.minimum(i, last_p), 0)),
                   pl.BlockSpec((TM, d), lambda i: (jnp.maximum(i - n_prompt_tiles, 0), 0))],
        out_shape=[jax.ShapeDtypeStruct((n_prompt_tiles * TM, d), F32),
                   jax.ShapeDtypeStruct(((n_tiles - n_prompt_tiles) * TM, d), F32)],
        scratch_shapes=[pltpu.VMEM((TM, d), F32), pltpu.SemaphoreType.DMA(())],
        compiler_params=_cparams(("arbitrary",)),
        name="moe_combine_ln_final",
    )(pos3, ys, x1g, g, b)


def _combine_inproj_kernel(pos_ref, posn_ref, ys_hbm, x1_ref, g_ref, b_ref, w_ref,
                           x2_ref, h_ref, kv_ref, ybuf, sems,
                           *, alpha, d, n_tiles):
    i = pl.program_id(0)
    slot = i % 2

    def gather(p_ref, s):
        def start(t, carry):
            _row_copy(ys_hbm, ybuf.at[s], p_ref[0, 0, t], t, sems.at[s]).start()
            return carry
        lax.fori_loop(0, TM, start, 0, unroll=ROW_DMA_UNROLL)

    def wait(s):
        pltpu.make_async_copy(ys_hbm.at[pl.ds(0, TM)], ybuf.at[s], sems.at[s]).wait()

    @pl.when(i == 0)
    def _():
        gather(pos_ref, 0)

    gather(posn_ref, 1 - slot)
    wait(slot)
    sub = TM // OUTPROJ_SUBTILES
    w = w_ref[...]
    for s in range(OUTPROJ_SUBTILES):
        rows = slice(s * sub, (s + 1) * sub)
        x2 = _layer_norm(alpha * x1_ref[rows, :d] + ybuf[slot, rows, :], g_ref[...], b_ref[...])
        x2_ref[rows, :] = x2
        acc = jnp.dot(x2.astype(BF16), w, preferred_element_type=F32)
        h_ref[rows, :] = acc.astype(BF16)
        kv_ref[rows, : 2 * WIDTH_A] = acc[:, COL_KA:COL_QB]
        kv_ref[rows, 2 * WIDTH_A:] = acc[:, COL_KB:]

    @pl.when(i == n_tiles - 1)
    def _():
        wait(1 - slot)


def _combine_inproj(pos3, ys, x1g, g, b, w_in, *, alpha, seq, n_prompt_tiles):
    t_pad, width = x1g.shape
    d = width - GATE_LANES
    n_tiles = t_pad // TM
    tiles_per_seq = seq // TM
    n_seq = n_prompt_tiles // tiles_per_seq
    n_kv_blocks = n_seq + (n_tiles - n_prompt_tiles)
    const = lambda i: (0, 0)

    def kv_map(i):
        return (jnp.where(i < n_prompt_tiles, i // tiles_per_seq, n_seq + i - n_prompt_tiles), 0)

    return pl.pallas_call(
        functools.partial(_combine_inproj_kernel, alpha=alpha, d=d, n_tiles=n_tiles),
        grid=(n_tiles,),
        in_specs=[pl.BlockSpec((1, 1, TM), lambda i: (i, 0, 0), memory_space=pltpu.SMEM),
                  pl.BlockSpec((1, 1, TM), lambda i: (jnp.minimum(i + 1, n_tiles - 1), 0, 0),
                               memory_space=pltpu.SMEM),
                  pl.BlockSpec(memory_space=pl.ANY),
                  pl.BlockSpec((TM, width), lambda i: (i, 0)),
                  pl.BlockSpec(g.shape, const), pl.BlockSpec(b.shape, const),
                  pl.BlockSpec(w_in.shape, const)],
        out_specs=[pl.BlockSpec((TM, d), lambda i: (i, 0)),
                   pl.BlockSpec((TM, D_IN), lambda i: (i, 0)),
                   pl.BlockSpec((TM, KV_OUT), kv_map)],
        out_shape=[jax.ShapeDtypeStruct((t_pad, d), F32),
                   jax.ShapeDtypeStruct((t_pad, D_IN), BF16),
                   jax.ShapeDtypeStruct((n_kv_blocks * TM, KV_OUT), F32)],
        scratch_shapes=[pltpu.VMEM((2, TM, d), F32), pltpu.SemaphoreType.DMA((2,))],
        compiler_params=_cparams(("arbitrary",)),
        name="moe_combine_ln_inproj",
    )(pos3, pos3, ys, x1g, g, b, w_in)


def _alibi_slopes():
    return 2.0 ** (-8.0 * jnp.arange(1, N_HEADS_B + 1, dtype=F32) / N_HEADS_B)


def _rel_bias_table(rel_bias, q0, n_q, n_k):
    n_diag = n_q + n_k - 1
    dist = q0 + (n_q - 1) - jnp.arange(n_diag, dtype=jnp.int32)
    diag = jnp.take(rel_bias.astype(F32), jnp.clip(dist, -REL_CLIP, REL_CLIP) + REL_CLIP, axis=1)
    n_h = diag.shape[0]
    ext = jnp.concatenate([diag, jnp.zeros((n_h, 1), F32)], axis=1)
    shifted = jnp.tile(ext, (1, n_q))[:, :n_q * n_diag].reshape(n_h, n_q, n_diag)
    return shifted[:, :, n_q - 1:n_q - 1 + n_k]


def _alibi_table(q0, n_q, n_k):
    dist = q0 + jnp.arange(n_q, dtype=jnp.int32)[:, None] - jnp.arange(n_k, dtype=jnp.int32)[None, :]
    return -_alibi_slopes()[:, None, None] * jnp.abs(dist).astype(F32)[None]


def kernel(x_prompt, x_sample, cache_a_k, cache_a_v, cache_b_k, cache_b_v, w_in, rel_bias, attn_sinks,
           gn_a, gn_b, w_out, ln1_g, ln1_b, w_router, b_router, w_gate, w_up, w_down, ln2_g, ln2_b):
    n_seq, seq, d = x_prompt.shape
    n_dec, dec_seq, _ = x_sample.shape
    depth = w_in.shape[0]
    rows_ca, rows_cb = cache_a_k.shape[2], cache_b_k.shape[2]
    alpha = (2.0 * depth) ** 0.25
    assert seq % TM == 0 and PAD_A == TM and n_dec * dec_seq <= TM and TM % dec_seq == 0
    assert dec_seq % 16 == 0 and seq >= PAD_A

    t_prompt = n_seq * seq
    t_real = t_prompt + n_dec * dec_seq
    t_pad = -(-t_real // (2 * TM)) * (2 * TM)
    n_prompt_tiles = t_prompt // TM
    n_slot_tiles = t_pad // TM + N_CLASSES

    x = jnp.concatenate([x_prompt.reshape(t_prompt, d), x_sample.reshape(n_dec * dec_seq, d),
                         jnp.zeros((t_pad - t_real, d), F32)], axis=0)

    w_in_b = w_in.astype(BF16)
    w_out_b = w_out.astype(BF16)
    w_gate_b, w_up_b, w_down_b = w_gate.astype(BF16), w_up.astype(BF16), w_down.astype(BF16)
    wr_t = w_router.astype(F32).T
    wr_hi = wr_t.astype(BF16)
    wr_lo = (wr_t - wr_hi.astype(F32)).astype(BF16)
    br = b_router.astype(F32).reshape(N_EXPERTS, 1)

    alibi_p = _alibi_table(PAD_B, CHUNK, BAND_B * CHUNK)
    alibi_s = _alibi_table(rows_cb, dec_seq, rows_cb + dec_seq)

    pa_k, pa_v, pb_k, pb_v, sa_k, sa_v, sb_k, sb_v = ([] for _ in range(8))
    h, kv = _inproj(x, w_in_b[0], seq=seq, n_prompt_tiles=n_prompt_tiles)
    for l in range(depth):
        sinks = attn_sinks[l].astype(F32)
        gna = gn_a[l].astype(F32).reshape(1, WIDTH_A)
        gnb = gn_b[l].astype(F32).reshape(1, WIDTH_B)
        bias_pa = _rel_bias_table(rel_bias[l], PAD_A, CHUNK, BAND_A * CHUNK)
        a_prompt = _attn_prompt(h, sinks, bias_pa, alibi_p, gna, gnb, n_seq=n_seq, seq=seq)
        bias_sa = _rel_bias_table(rel_bias[l], rows_ca, dec_seq, rows_ca + dec_seq)
        bias_s = (bias_sa[:, :, :rows_ca], bias_sa[:, :, rows_ca:],
                  alibi_s[:, :, :rows_cb], alibi_s[:, :, rows_cb:])
        a_sample = _attn_sample(
            h, sinks,
            cache_a_k[l].reshape(n_dec, rows_ca, WIDTH_A), cache_a_v[l].reshape(n_dec, rows_ca, WIDTH_A),
            cache_b_k[l].reshape(n_dec, rows_cb, KV_WIDTH_B), cache_b_v[l].reshape(n_dec, rows_cb, KV_WIDTH_B),
            bias_s, gna, gnb, first_row=t_prompt, n_rows_out=t_pad - t_prompt, n_dec=n_dec, dec_seq=dec_seq)

        x1g, cls = _outproj(a_prompt, a_sample, x, w_out_b[l],
                            ln1_g[l].astype(F32).reshape(1, d), ln1_b[l].astype(F32).reshape(1, d),
                            wr_hi, wr_lo, br, n_prompt_tiles=n_prompt_tiles, alpha=alpha)
        pos, tile_ea, tile_eb, n_used, pad_lo, pad_hi = _slots(
            cls.reshape(t_pad // V7X_LANES, V7X_LANES), n_slot_tiles=n_slot_tiles)
        pos3 = pos.reshape(t_pad // TM, 1, TM)
        xs = _dispatch(pos3, pad_lo[0], pad_hi[0], n_used[0], x1g, n_slot_tiles=n_slot_tiles)
        ys = _moe(tile_ea[0, :n_slot_tiles], tile_eb[0, :n_slot_tiles], n_used[0, :1],
                  xs, w_gate_b[l], w_up_b[l], w_down_b[l])
        kv_l = kv
        g2, b2 = ln2_g[l].astype(F32).reshape(1, d), ln2_b[l].astype(F32).reshape(1, d)
        if l + 1 < depth:
            x, h, kv = _combine_inproj(pos3, ys, x1g, g2, b2, w_in_b[l + 1], alpha=alpha, seq=seq,
                                       n_prompt_tiles=n_prompt_tiles)
        else:
            y_p, y_t = _combine_final(pos3, ys, x1g, g2, b2, alpha=alpha, n_prompt_tiles=n_prompt_tiles)

        kv_p = kv_l[:n_seq * TM].reshape(n_seq, TM, KV_OUT)
        kv_s = kv_l[n_seq * TM:n_seq * TM + n_dec * dec_seq].reshape(n_dec, dec_seq, KV_OUT)
        ra, rb = min(PAD_A, seq), min(WINDOW_B, seq)
        pa_k.append(kv_p[:, TM - ra:, :WIDTH_A].reshape(n_seq, ra, N_HEADS_A, HEAD_DIM))
        pa_v.append(kv_p[:, TM - ra:, WIDTH_A:2 * WIDTH_A].reshape(n_seq, ra, N_HEADS_A, HEAD_DIM))
        pb_k.append(kv_p[:, TM - rb:, 2 * WIDTH_A:2 * WIDTH_A + KV_WIDTH_B].reshape(n_seq, rb, N_KV_B, HEAD_DIM))
        pb_v.append(kv_p[:, TM - rb:, 2 * WIDTH_A + KV_WIDTH_B:].reshape(n_seq, rb, N_KV_B, HEAD_DIM))
        sa_k.append(kv_s[:, :, :WIDTH_A].reshape(n_dec, dec_seq, N_HEADS_A, HEAD_DIM))
        sa_v.append(kv_s[:, :, WIDTH_A:2 * WIDTH_A].reshape(n_dec, dec_seq, N_HEADS_A, HEAD_DIM))
        sb_k.append(kv_s[:, :, 2 * WIDTH_A:2 * WIDTH_A + KV_WIDTH_B].reshape(n_dec, dec_seq, N_KV_B, HEAD_DIM))
        sb_v.append(kv_s[:, :, 2 * WIDTH_A + KV_WIDTH_B:].reshape(n_dec, dec_seq, N_KV_B, HEAD_DIM))

    y_prompt = y_p.reshape(n_seq, seq, d)
    y_sample = y_t[:n_dec * dec_seq].reshape(n_dec, dec_seq, d)
    return (y_prompt, y_sample,
            jnp.stack(pa_k), jnp.stack(pa_v), jnp.stack(pb_k), jnp.stack(pb_v),
            jnp.stack(sa_k), jnp.stack(sa_v), jnp.stack(sb_k), jnp.stack(sb_v))
```

```python
import functools

import jax
import jax.numpy as jnp
import numpy as np
from jax import lax
from jax.experimental import pallas as pl
from jax.experimental.pallas import tpu as pltpu

CHUNK = 64
HEAD_DIM = 64
N_HEADS_A = 8
BAND_A = 9
REL_CLIP = 128
N_HEADS_B = 8
N_KV_B = 2
WINDOW_B = 128
BAND_B = 1 + WINDOW_B // CHUNK
N_EXPERTS = 16
N_GROUPS = 4
EXPERTS_PER_GROUP = N_EXPERTS // N_GROUPS
PAIRS_PER_GROUP = 6
N_CLASSES = N_GROUPS * PAIRS_PER_GROUP
LN_EPS = 1e-5
RMS_EPS = 1e-6

WIDTH_A = N_HEADS_A * HEAD_DIM
WIDTH_B = N_HEADS_B * HEAD_DIM
KV_WIDTH_B = N_KV_B * HEAD_DIM
D_IN = 3 * WIDTH_A + WIDTH_B + 2 * KV_WIDTH_B
COL_QA, COL_KA, COL_VA = 0, WIDTH_A, 2 * WIDTH_A
COL_QB = 3 * WIDTH_A
COL_KB = COL_QB + WIDTH_B
COL_VB = COL_KB + KV_WIDTH_B
KV_OUT = 2 * WIDTH_A + 2 * KV_WIDTH_B
PAD_A = (BAND_A - 1) * CHUNK
PAD_B = WINDOW_B

V7X_LANES = 128
TM = 512
GATE_LANES = V7X_LANES
V7X_VMEM_LIMIT = 56 * 1024 * 1024

F32 = jnp.float32
BF16 = jnp.bfloat16
NEG_INF = float("-inf")


def _cparams(sem):
    return pltpu.CompilerParams(dimension_semantics=sem, vmem_limit_bytes=V7X_VMEM_LIMIT)


def _inproj_kernel(xp_ref, xt_ref, w_ref, h_ref, kv_ref, *, tiles_per_seq, n_prompt_tiles):
    i = pl.program_id(0)
    x = jnp.where(i < n_prompt_tiles, xp_ref[...], xt_ref[...])
    acc = jnp.dot(x.astype(BF16), w_ref[...], preferred_element_type=F32)
    h_ref[...] = acc.astype(BF16)
    keeps_kv = jnp.logical_or(i % tiles_per_seq == tiles_per_seq - 1, i >= n_prompt_tiles)

    @pl.when(keeps_kv)
    def _():
        kv_ref[:, : 2 * WIDTH_A] = acc[:, COL_KA:COL_QB]
        kv_ref[:, 2 * WIDTH_A:] = acc[:, COL_KB:]


def _inproj(x_prompt, x_tail, w_in, *, seq, n_prompt_tiles):
    d = x_prompt.shape[1]
    t_pad = x_prompt.shape[0] + x_tail.shape[0]
    n_tiles = t_pad // TM
    last_p = n_prompt_tiles - 1
    tiles_per_seq = seq // TM
    n_seq = n_prompt_tiles // tiles_per_seq
    n_kv_blocks = n_seq + (n_tiles - n_prompt_tiles)

    def kv_map(i):
        return (jnp.where(i < n_prompt_tiles, i // tiles_per_seq, n_seq + i - n_prompt_tiles), 0)

    return pl.pallas_call(
        functools.partial(_inproj_kernel, tiles_per_seq=tiles_per_seq, n_prompt_tiles=n_prompt_tiles),
        grid=(n_tiles,),
        in_specs=[pl.BlockSpec((TM, d), lambda i: (jnp.minimum(i, last_p), 0)),
                  pl.BlockSpec((TM, d), lambda i: (jnp.maximum(i - n_prompt_tiles, 0), 0)),
                  pl.BlockSpec((d, D_IN), lambda i: (0, 0))],
        out_specs=[pl.BlockSpec((TM, D_IN), lambda i: (i, 0)),
                   pl.BlockSpec((TM, KV_OUT), kv_map)],
        out_shape=[jax.ShapeDtypeStruct((t_pad, D_IN), BF16),
                   jax.ShapeDtypeStruct((n_kv_blocks * TM, KV_OUT), F32)],
        compiler_params=_cparams(("arbitrary",)),
        name="inproj",
    )(x_prompt, x_tail, w_in)


def _softmax_pv(s_parts, v_parts, sink=None):
    m = s_parts[0].max(axis=-1, keepdims=True)
    for s in s_parts[1:]:
        m = jnp.maximum(m, s.max(axis=-1, keepdims=True))
    if sink is not None:
        m = jnp.maximum(m, sink)
    denom = None
    acc = None
    for s, v in zip(s_parts, v_parts):
        e = jnp.exp(s - m)
        d = e.sum(axis=-1, keepdims=True)
        pv = jnp.dot(e.astype(BF16), v, preferred_element_type=F32)
        denom = d if denom is None else denom + d
        acc = pv if acc is None else acc + pv
    if sink is not None:
        denom = denom + jnp.exp(sink - m)
    return acc / denom


def _nt_dot(a, b):
    return lax.dot_general(a, b, (((1,), (1,)), ((), ())), preferred_element_type=F32)


def _head_pair_lane_mask(rows):
    return lax.broadcasted_iota(jnp.int32, (rows, 2 * HEAD_DIM), 1) < HEAD_DIM


def _rms_store(o_ref, row_slice, col0, pairs, gn_ref):
    ssq = None
    for o in pairs:
        s = (o * o).sum(axis=-1, keepdims=True)
        ssq = s if ssq is None else ssq + s
    width = len(pairs) * 2 * HEAD_DIM
    inv = lax.rsqrt(ssq / width + RMS_EPS)
    for p, o in enumerate(pairs):
        c0 = p * 2 * HEAD_DIM
        g = gn_ref[:, c0:c0 + 2 * HEAD_DIM]
        o_ref[row_slice, col0 + c0:col0 + c0 + 2 * HEAD_DIM] = (o * inv * g).astype(o_ref.dtype)


KEY_BLOCK = V7X_LANES


def _swap_halves(x):
    return jnp.concatenate([x[:, HEAD_DIM:], x[:, :HEAD_DIM]], axis=1)


def _pair_rows(q2, lane_lo, kv_half=None):
    hi_lanes = jnp.logical_not(lane_lo)
    zero = jnp.zeros_like(q2)
    if kv_half is None:
        lo, hi = jnp.where(lane_lo, q2, zero), jnp.where(hi_lanes, q2, zero)
    elif kv_half == 0:
        lo, hi = jnp.where(lane_lo, q2, zero), jnp.where(lane_lo, _swap_halves(q2), zero)
    else:
        lo, hi = jnp.where(hi_lanes, _swap_halves(q2), zero), jnp.where(hi_lanes, q2, zero)
    return jnp.concatenate([lo, hi], axis=0)


def _tn_dot(a, b):
    return lax.dot_general(a, b, (((0,), (0,)), ((), ())), preferred_element_type=F32)


PV_KEYS = 256


def _score_blocks(k_ref, lanes, row0, n_keys, w):
    out = []
    for off in range(0, n_keys, KEY_BLOCK):
        n = min(KEY_BLOCK, n_keys - off)
        start = pl.multiple_of(row0 + off, CHUNK)
        out.append((_nt_dot(k_ref[pl.ds(start, n), lanes], w), start, n, off))
    return out


def _pair_probs(score_blocks, lane_tile, bias_ref, pair, pad, masked, sink_block=None):
    lanes = slice(lane_tile * V7X_LANES, (lane_tile + 1) * V7X_LANES)
    blocks = []
    for s_all, start, n, off in score_blocks:
        s = s_all[:, lanes] + bias_ref[pair, off:off + n, :]
        if masked:
            key_row = lax.broadcasted_iota(jnp.int32, s.shape, 0) + start
            s = jnp.where(key_row >= pad, s, NEG_INF)
        blocks.append((s, start, n))
    m = None
    for s, _, _ in blocks:
        bm = s.max(axis=0, keepdims=True)
        m = bm if m is None else jnp.maximum(m, bm)
    if sink_block is not None:
        m = jnp.maximum(m, sink_block.max(axis=0, keepdims=True))
    probs = [(jnp.exp(s - m).astype(BF16), start, n) for s, start, n in blocks]
    sink_p = None if sink_block is None else jnp.exp(sink_block - m).astype(BF16)
    return probs, sink_p


def _pair_values(probs, sink_p, v_ref, lanes):
    groups, cur, cur_rows = [], [], 0
    for blk in probs:
        if cur and cur_rows + blk[2] > PV_KEYS:
            groups.append(cur)
            cur, cur_rows = [], 0
        cur.append(blk)
        cur_rows += blk[2]
    groups.append(cur)
    acc = None
    for gi, grp in enumerate(groups):
        rows = sum(n for _, _, n in grp)
        p = grp[0][0] if len(grp) == 1 else jnp.concatenate([b[0] for b in grp], axis=0)
        v1 = jnp.concatenate([v_ref[pl.ds(grp[0][1], rows), lanes], jnp.ones((rows, V7X_LANES), BF16)], axis=1)
        if sink_p is not None and gi == len(groups) - 1:
            n = sink_p.shape[0]
            p = jnp.concatenate([p, sink_p], axis=0)
            sink_v = jnp.concatenate([jnp.zeros((n, V7X_LANES), BF16), jnp.ones((n, V7X_LANES), BF16)], axis=1)
            v1 = jnp.concatenate([v1, sink_v], axis=0)
        c = _tn_dot(p, v1)
        acc = c if acc is None else acc + c
    return acc[:, :V7X_LANES] / acc[:, V7X_LANES:]


def _quad_rows(q4):
    head_of_lane = lax.broadcasted_iota(jnp.int32, q4.shape, 1) // HEAD_DIM
    zero = jnp.zeros_like(q4)
    return jnp.concatenate([jnp.where(head_of_lane == h, q4, zero) for h in range(4)], axis=0)


def _attn_prompt_kernel(qa_ref, ka_ref, va_ref, qb_ref, kb_ref, vb_ref,
                        ba_ref, bb_ref, sink_ref, gna_ref, gnb_ref, o_ref,
                        kpa, vpa, kpb, vpb, pa_even, pb_even, pa_odd, pb_odd, *, seq, tq):
    j = pl.program_id(1)

    @pl.when(j == 0)
    def _():
        kpa[0:PAD_A, :] = jnp.zeros((PAD_A, WIDTH_A), BF16)
        vpa[0:PAD_A, :] = jnp.zeros((PAD_A, WIDTH_A), BF16)
        kpb[0:PAD_B, :] = jnp.zeros((PAD_B, KV_WIDTH_B), BF16)
        vpb[0:PAD_B, :] = jnp.zeros((PAD_B, KV_WIDTH_B), BF16)
        kpa[PAD_A:PAD_A + seq, :] = ka_ref[...]
        vpa[PAD_A:PAD_A + seq, :] = va_ref[...]
        kpb[PAD_B:PAD_B + seq, :] = kb_ref[...]
        vpb[PAD_B:PAD_B + seq, :] = vb_ref[...]

    lane_lo = _head_pair_lane_mask(CHUNK)
    scale = HEAD_DIM ** -0.5
    all_lanes = slice(0, KV_WIDTH_B)

    n_pairs_a, n_pairs_b = N_HEADS_A // 2, N_HEADS_B // 2
    band_a, band_b = BAND_A * CHUNK, BAND_B * CHUNK
    n_sink = sink_ref.shape[1]
    even_bufs, odd_bufs = (pa_even, pb_even), (pa_odd, pb_odd)

    def band_start(c):
        return pl.multiple_of(j * tq + c * CHUNK, CHUNK)

    def probs_phase(c, masked, bufs):
        pa_buf, pb_buf = bufs
        rows = pl.ds(pl.multiple_of(c * CHUNK, CHUNK), CHUNK)
        cs = band_start(c)
        scores_a = []
        for quad in range(N_HEADS_A // 4):
            lanes = slice(quad * 4 * HEAD_DIM, (quad + 1) * 4 * HEAD_DIM)
            w = _quad_rows(qa_ref[rows, lanes] * scale)
            scores_a.append(_score_blocks(kpa, lanes, cs, band_a, w))
        w = jnp.concatenate(
            [_pair_rows(qb_ref[rows, p * 2 * HEAD_DIM:(p + 1) * 2 * HEAD_DIM] * scale, lane_lo,
                        kv_half=(2 * p) // (N_HEADS_B // N_KV_B))
             for p in range(n_pairs_b)], axis=0)
        scores_b = _score_blocks(kpb, all_lanes, cs, band_b, w)
        for p in range(n_pairs_a):
            probs, _ = _pair_probs(scores_a[p // 2], p % 2, ba_ref, p, PAD_A, masked)
            for blk, (_, _, n, off) in zip(probs, scores_a[p // 2]):
                pa_buf[p, off:off + n, :] = blk[0]
        for p in range(n_pairs_b):
            probs, sink_p = _pair_probs(scores_b, p, bb_ref, p, PAD_B, masked, sink_block=sink_ref[p])
            for blk, (_, _, n, off) in zip(probs, scores_b):
                pb_buf[p, off:off + n, :] = blk[0]
            pb_buf[p, band_b:band_b + n_sink, :] = sink_p

    def values_phase(c, bufs):
        pa_buf, pb_buf = bufs
        rows = pl.ds(pl.multiple_of(c * CHUNK, CHUNK), CHUNK)
        cs = band_start(c)

        def groups(buf, p, n_keys):
            return [(buf[p, off:off + min(PV_KEYS, n_keys - off), :], pl.multiple_of(cs + off, CHUNK),
                     min(PV_KEYS, n_keys - off)) for off in range(0, n_keys, PV_KEYS)]

        pairs_a = []
        for p in range(n_pairs_a):
            lanes = slice(p * 2 * HEAD_DIM, (p + 1) * 2 * HEAD_DIM)
            o = _pair_values(groups(pa_buf, p, band_a), None, vpa, lanes)
            pairs_a.append(jnp.where(lane_lo, o[:CHUNK], o[CHUNK:]))
        _rms_store(o_ref, rows, 0, pairs_a, gna_ref)

        pairs_b = []
        for p in range(n_pairs_b):
            kv_half = (2 * p) // (N_HEADS_B // N_KV_B)
            o = _pair_values(groups(pb_buf, p, band_b), pb_buf[p, band_b:band_b + n_sink, :], vpb, all_lanes)
            lo, hi = o[:CHUNK], o[CHUNK:]
            if kv_half == 0:
                hi = _swap_halves(hi)
            else:
                lo = _swap_halves(lo)
            pairs_b.append(jnp.where(lane_lo, lo, hi))
        _rms_store(o_ref, rows, WIDTH_A, pairs_b, gnb_ref)

    def run(masked):
        n_chunks = tq // CHUNK
        bufs = (even_bufs, odd_bufs)
        probs_phase(0, masked, even_bufs)
        for c in range(1, n_chunks):
            probs_phase(c, masked, bufs[c % 2])
            values_phase(c - 1, bufs[(c - 1) % 2])
        values_phase(n_chunks - 1, bufs[(n_chunks - 1) % 2])

    @pl.when(j == 0)
    def _():
        run(True)

    @pl.when(j > 0)
    def _():
        run(False)


def _pair_transposed(table):
    n_h, n_q, n_k = table.shape
    return table.reshape(n_h // 2, 2, n_q, n_k).transpose(0, 3, 1, 2).reshape(n_h // 2, n_k, 2 * n_q)


def _sink_blocks(sinks):
    row = jnp.repeat(sinks.astype(F32).reshape(N_HEADS_B // 2, 2), HEAD_DIM, axis=1)
    rest = jnp.full((N_HEADS_B // 2, 15, 2 * HEAD_DIM), NEG_INF, F32)
    return jnp.concatenate([row[:, None, :], rest], axis=1)


def _attn_prompt(h, sinks, bias_a, bias_b, gn_a, gn_b, *, n_seq, seq):
    tq = TM
    assert tq >= PAD_A and tq >= PAD_B and tq % CHUNK == 0
    n_q = seq // tq
    vmem = pltpu.VMEM
    bias_a, bias_b = _pair_transposed(bias_a), _pair_transposed(bias_b)
    sinks = _sink_blocks(sinks)
    return pl.pallas_call(
        functools.partial(_attn_prompt_kernel, seq=seq, tq=tq),
        grid=(n_seq, n_q),
        in_specs=[
            pl.BlockSpec((tq, WIDTH_A), lambda b, j: (b * n_q + j, COL_QA // WIDTH_A)),
            pl.BlockSpec((seq, WIDTH_A), lambda b, j: (b, COL_KA // WIDTH_A)),
            pl.BlockSpec((seq, WIDTH_A), lambda b, j: (b, COL_VA // WIDTH_A)),
            pl.BlockSpec((tq, WIDTH_B), lambda b, j: (b * n_q + j, COL_QB // WIDTH_B)),
            pl.BlockSpec((seq, KV_WIDTH_B), lambda b, j: (b, COL_KB // KV_WIDTH_B)),
            pl.BlockSpec((seq, KV_WIDTH_B), lambda b, j: (b, COL_VB // KV_WIDTH_B)),
            pl.BlockSpec(bias_a.shape, lambda b, j: (0, 0, 0)),
            pl.BlockSpec(bias_b.shape, lambda b, j: (0, 0, 0)),
            pl.BlockSpec(sinks.shape, lambda b, j: (0, 0, 0)),
            pl.BlockSpec(gn_a.shape, lambda b, j: (0, 0)),
            pl.BlockSpec(gn_b.shape, lambda b, j: (0, 0)),
        ],
        out_specs=pl.BlockSpec((tq, WIDTH_A + WIDTH_B), lambda b, j: (b * n_q + j, 0)),
        out_shape=jax.ShapeDtypeStruct((n_seq * seq, WIDTH_A + WIDTH_B), BF16),
        scratch_shapes=[vmem((PAD_A + seq, WIDTH_A), BF16), vmem((PAD_A + seq, WIDTH_A), BF16),
                        vmem((PAD_B + seq, KV_WIDTH_B), BF16), vmem((PAD_B + seq, KV_WIDTH_B), BF16),
                        vmem((N_HEADS_A // 2, BAND_A * CHUNK, V7X_LANES), BF16),
                        vmem((N_HEADS_B // 2, BAND_B * CHUNK + sinks.shape[1], V7X_LANES), BF16),
                        vmem((N_HEADS_A // 2, BAND_A * CHUNK, V7X_LANES), BF16),
                        vmem((N_HEADS_B // 2, BAND_B * CHUNK + sinks.shape[1], V7X_LANES), BF16)],
        compiler_params=_cparams(("arbitrary", "arbitrary")),
        name="attn_prompt",
    )(h, h, h, h, h, h, bias_a, bias_b, sinks, gn_a, gn_b)


def _attn_sample_kernel(sink_ref, qa_ref, ka_ref, va_ref, qb_ref, kb_ref, vb_ref,
                        cka_ref, cva_ref, ckb_ref, cvb_ref,
                        bac_ref, ban_ref, bbc_ref, bbn_ref, gna_ref, gnb_ref, o_ref, *, n_dec, dec_seq):
    b = pl.program_id(0)

    @pl.when(b >= n_dec)
    def _():
        o_ref[...] = jnp.zeros(o_ref.shape, o_ref.dtype)

    @pl.when(b < n_dec)
    def _():
        lane_lo = _head_pair_lane_mask(dec_seq)
        scale = HEAD_DIM ** -0.5
        rows = slice(0, dec_seq)

        pairs_a = []
        for p in range(N_HEADS_A // 2):
            lanes = slice(p * 2 * HEAD_DIM, (p + 1) * 2 * HEAD_DIM)
            q2 = qa_ref[:, lanes] * scale
            kc = cka_ref[0, :, lanes].astype(BF16)
            vc = cva_ref[0, :, lanes].astype(BF16)
            kn = ka_ref[:, lanes]
            vn = va_ref[:, lanes]
            halves = []
            for half in range(2):
                h = 2 * p + half
                keep = lane_lo if half == 0 else jnp.logical_not(lane_lo)
                qm = jnp.where(keep, q2, jnp.zeros_like(q2))
                s_c = _nt_dot(qm, kc) + bac_ref[h]
                s_n = _nt_dot(qm, kn) + ban_ref[h]
                halves.append(_softmax_pv([s_c, s_n], [vc, vn]))
            pairs_a.append(jnp.where(lane_lo, halves[0], halves[1]))
        _rms_store(o_ref, rows, 0, pairs_a, gna_ref)

        kc = ckb_ref[0].astype(BF16)
        vc = cvb_ref[0].astype(BF16)
        kn = kb_ref[...]
        vn = vb_ref[...]
        pairs_b = []
        for p in range(N_HEADS_B // 2):
            lanes = slice(p * 2 * HEAD_DIM, (p + 1) * 2 * HEAD_DIM)
            q2 = qb_ref[:, lanes] * scale
            halves = []
            for half in range(2):
                h = 2 * p + half
                kv_head = h // (N_HEADS_B // N_KV_B)
                keep = lane_lo if half == 0 else jnp.logical_not(lane_lo)
                qm = jnp.where(keep, q2, jnp.zeros_like(q2))
                if kv_head != half:
                    qm = jnp.concatenate([qm[:, HEAD_DIM:], qm[:, :HEAD_DIM]], axis=1)
                s_c = _nt_dot(qm, kc) + bbc_ref[h]
                s_n = _nt_dot(qm, kn) + bbn_ref[h]
                o = _softmax_pv([s_c, s_n], [vc, vn], sink=sink_ref[h])
                if kv_head != half:
                    o = jnp.concatenate([o[:, HEAD_DIM:], o[:, :HEAD_DIM]], axis=1)
                halves.append(o)
            pairs_b.append(jnp.where(lane_lo, halves[0], halves[1]))
        _rms_store(o_ref, rows, WIDTH_A, pairs_b, gnb_ref)


def _attn_sample(h, sinks, cka, cva, ckb, cvb, bias, gn_a, gn_b, *, first_row, n_rows_out, n_dec, dec_seq):
    bac, ban, bbc, bbn = bias
    n_steps = n_rows_out // dec_seq
    rb = first_row // dec_seq
    last = n_dec - 1

    def hrow(b):
        return rb + jnp.minimum(b, last)

    def cache_spec(c):
        return pl.BlockSpec((1,) + c.shape[1:], lambda b: (jnp.minimum(b, last), 0, 0))

    def const_spec(a):
        return pl.BlockSpec(a.shape, lambda b: (0,) * a.ndim)

    return pl.pallas_call(
        functools.partial(_attn_sample_kernel, n_dec=n_dec, dec_seq=dec_seq),
        grid=(n_steps,),
        in_specs=[
            pl.BlockSpec(memory_space=pltpu.SMEM),
            pl.BlockSpec((dec_seq, WIDTH_A), lambda b: (hrow(b), COL_QA // WIDTH_A)),
            pl.BlockSpec((dec_seq, WIDTH_A), lambda b: (hrow(b), COL_KA // WIDTH_A)),
            pl.BlockSpec((dec_seq, WIDTH_A), lambda b: (hrow(b), COL_VA // WIDTH_A)),
            pl.BlockSpec((dec_seq, WIDTH_B), lambda b: (hrow(b), COL_QB // WIDTH_B)),
            pl.BlockSpec((dec_seq, KV_WIDTH_B), lambda b: (hrow(b), COL_KB // KV_WIDTH_B)),
            pl.BlockSpec((dec_seq, KV_WIDTH_B), lambda b: (hrow(b), COL_VB // KV_WIDTH_B)),
            cache_spec(cka), cache_spec(cva), cache_spec(ckb), cache_spec(cvb),
            const_spec(bac), const_spec(ban), const_spec(bbc), const_spec(bbn),
            const_spec(gn_a), const_spec(gn_b),
        ],
        out_specs=pl.BlockSpec((dec_seq, WIDTH_A + WIDTH_B), lambda b: (b, 0)),
        out_shape=jax.ShapeDtypeStruct((n_rows_out, WIDTH_A + WIDTH_B), BF16),
        compiler_params=_cparams(("arbitrary",)),
        name="attn_sample",
    )(sinks, h, h, h, h, h, h, cka, cva, ckb, cvb, bac, ban, bbc, bbn, gn_a, gn_b)


def _layer_norm(r, g, b):
    mu = r.mean(axis=-1, keepdims=True)
    c = r - mu
    var = (c * c).mean(axis=-1, keepdims=True)
    return c * lax.rsqrt(var + LN_EPS) * g + b


def _route_rows(logits):
    rows = [logits[e:e + 1, :] for e in range(N_EXPERTS)]

    def top2_sum(a, b, c, d):
        hi1, lo1 = jnp.maximum(a, b), jnp.minimum(a, b)
        hi2, lo2 = jnp.maximum(c, d), jnp.minimum(c, d)
        return jnp.maximum(hi1, hi2) + jnp.maximum(jnp.minimum(hi1, hi2), jnp.maximum(lo1, lo2))

    scores = [top2_sum(*rows[4 * g:4 * g + 4]) for g in range(N_GROUPS)]
    best = scores[0]
    g_sel = jnp.zeros(best.shape, jnp.int32)
    for g in range(1, N_GROUPS):
        upd = scores[g] > best
        best = jnp.where(upd, scores[g], best)
        g_sel = jnp.where(upd, g, g_sel)

    vals = []
    for k in range(EXPERTS_PER_GROUP):
        v = rows[k]
        for g in range(1, N_GROUPS):
            v = jnp.where(g_sel == g, rows[4 * g + k], v)
        vals.append(v)

    v1 = vals[0]
    i1 = jnp.zeros(v1.shape, jnp.int32)
    for k in range(1, EXPERTS_PER_GROUP):
        upd = vals[k] > v1
        v1 = jnp.where(upd, vals[k], v1)
        i1 = jnp.where(upd, k, i1)
    v2 = jnp.full(v1.shape, NEG_INF, F32)
    i2 = jnp.zeros(v1.shape, jnp.int32)
    for k in range(EXPERTS_PER_GROUP):
        upd = jnp.logical_and(i1 != k, vals[k] > v2)
        v2 = jnp.where(upd, vals[k], v2)
        i2 = jnp.where(upd, k, i2)

    e = jnp.exp(v2 - v1)
    den = 1.0 + e
    w1 = 1.0 / den
    w2 = e / den
    first_is_lo = i1 < i2
    lo = jnp.where(first_is_lo, i1, i2)
    hi = jnp.where(first_is_lo, i2, i1)
    gate_lo = jnp.where(first_is_lo, w1, w2)
    gate_hi = jnp.where(first_is_lo, w2, w1)
    pair_base = jnp.where(lo == 0, 0, jnp.where(lo == 1, 3, 5))
    cls = g_sel * PAIRS_PER_GROUP + pair_base + hi - lo - 1
    return cls, gate_lo, gate_hi


OUTPROJ_SUBTILES = 4


def _outproj_kernel(ap_ref, as_ref, xm_ref, xt_ref, w_ref, g_ref, b_ref, wrh_ref, wrl_ref, br_ref,
                    x1_ref, cls_ref, *, n_prompt_tiles, n_main_tiles, alpha, d):
    i = pl.program_id(0)
    sub = TM // OUTPROJ_SUBTILES
    w = w_ref[...]
    ys = []
    for s in range(OUTPROJ_SUBTILES):
        rows = slice(s * sub, (s + 1) * sub)
        a = jnp.where(i < n_prompt_tiles, ap_ref[rows, :], as_ref[rows, :])
        ys.append(jnp.dot(a, w, preferred_element_type=F32))
    for s in range(OUTPROJ_SUBTILES):
        rows = slice(s * sub, (s + 1) * sub)
        x = jnp.where(i < n_main_tiles, xm_ref[rows, :], xt_ref[rows, :])
        x1 = _layer_norm(alpha * x + ys[s], g_ref[...], b_ref[...])
        x1_ref[rows, :d] = x1
        x_hi = x1.astype(BF16)
        x_lo = (x1 - x_hi.astype(F32)).astype(BF16)
        logits = (_nt_dot(wrh_ref[...], x_hi) + _nt_dot(wrh_ref[...], x_lo) + _nt_dot(wrl_ref[...], x_hi)
                  + br_ref[...])
        cls, gate_lo, gate_hi = _route_rows(logits)
        cls_ref[:, rows] = cls
        gates = jnp.concatenate([gate_lo, gate_hi, jnp.zeros((GATE_LANES - 2, sub), F32)], axis=0)
        x1_ref[rows, d:] = gates.T


def _outproj(a_prompt, a_sample, x_main, x_tail, w_out, g, b, wr_hi, wr_lo, br, *, n_tiles, n_prompt_tiles, alpha):
    d = x_main.shape[1]
    t_pad = n_tiles * TM
    n_main_tiles = x_main.shape[0] // TM
    last_p = n_prompt_tiles - 1
    const = lambda i: (0, 0)
    return pl.pallas_call(
        functools.partial(_outproj_kernel, n_prompt_tiles=n_prompt_tiles, n_main_tiles=n_main_tiles,
                          alpha=alpha, d=d),
        grid=(n_tiles,),
        in_specs=[pl.BlockSpec((TM, d), lambda i: (jnp.minimum(i, last_p), 0)),
                  pl.BlockSpec((TM, d), lambda i: (jnp.maximum(i - n_prompt_tiles, 0), 0)),
                  pl.BlockSpec((TM, d), lambda i: (jnp.minimum(i, n_main_tiles - 1), 0)),
                  pl.BlockSpec((TM, d), lambda i: (jnp.maximum(i - n_main_tiles, 0), 0)),
                  pl.BlockSpec(w_out.shape, const),
                  pl.BlockSpec(g.shape, const), pl.BlockSpec(b.shape, const),
                  pl.BlockSpec(wr_hi.shape, const), pl.BlockSpec(wr_lo.shape, const),
                  pl.BlockSpec(br.shape, const)],
        out_specs=[pl.BlockSpec((TM, d + GATE_LANES), lambda i: (i, 0)),
                   pl.BlockSpec((1, TM), lambda i: (0, i))],
        out_shape=[jax.ShapeDtypeStruct((t_pad, d + GATE_LANES), F32),
                   jax.ShapeDtypeStruct((1, t_pad), jnp.int32)],
        compiler_params=_cparams(("arbitrary",)),
        name="outproj_ln_route",
    )(a_prompt, a_sample, x_main, x_tail, w_out, g, b, wr_hi, wr_lo, br)


def _slots_kernel(cls_ref, pos_ref, ea_ref, eb_ref, nu_ref, padlo_ref, padhi_ref, *, rows_pad):
    lane = lax.broadcasted_iota(jnp.int32, (1, V7X_LANES), 1)
    pad_lo = jnp.zeros((1, V7X_LANES), F32)
    pad_hi = jnp.zeros((1, V7X_LANES), F32)
    cls = cls_ref[...]
    rows = cls.shape[0]
    r_i = lax.broadcasted_iota(jnp.int32, (rows_pad, rows_pad), 0)
    c_i = lax.broadcasted_iota(jnp.int32, (rows_pad, rows_pad), 1)
    lower = jnp.where(c_i < r_i, 1.0, 0.0).astype(BF16)
    l_r = lax.broadcasted_iota(jnp.int32, (V7X_LANES, V7X_LANES), 0)
    l_c = lax.broadcasted_iota(jnp.int32, (V7X_LANES, V7X_LANES), 1)
    upper = jnp.where(l_r < l_c, 1.0, 0.0).astype(BF16)

    n_t = ea_ref.shape[1]
    tile_start = lax.broadcasted_iota(jnp.int32, (1, n_t), 1).astype(F32) * float(TM)
    tile_cls = jnp.zeros((1, n_t), jnp.int32)
    off = jnp.zeros((1, 1), F32)
    pos = jnp.zeros((rows, V7X_LANES), F32)
    for c in range(N_CLASSES):
        m = jnp.where(cls == c, 1.0, 0.0)
        mb = m.astype(BF16)
        if rows_pad > rows:
            mb_pad = jnp.concatenate([mb, jnp.zeros((rows_pad - rows, V7X_LANES), BF16)], axis=0)
        else:
            mb_pad = mb
        before_rows = jnp.dot(lower, mb_pad, preferred_element_type=F32)[:rows]
        before_rows = before_rows.sum(axis=-1, keepdims=True)
        before_lanes = jnp.dot(mb, upper, preferred_element_type=F32)
        pos = pos + m * (off + before_rows + before_lanes)
        if c > 0:
            tile_cls = tile_cls + jnp.where(tile_start >= off, 1, 0)
        count = m.sum(axis=-1, keepdims=True).sum(axis=0, keepdims=True)
        pad_lo = jnp.where(lane == c, off + count, pad_lo)
        off = off + jnp.floor((count + float(TM - 1)) * (1.0 / TM)) * float(TM)
        pad_hi = jnp.where(lane == c, off, pad_hi)
    pos_ref[...] = pos.astype(jnp.int32)
    padlo_ref[...] = pad_lo.astype(jnp.int32)
    padhi_ref[...] = pad_hi.astype(jnp.int32)
    grp = (jnp.where(tile_cls >= 6, 1, 0) + jnp.where(tile_cls >= 12, 1, 0)
           + jnp.where(tile_cls >= 18, 1, 0))
    pair = tile_cls - PAIRS_PER_GROUP * grp
    lo = jnp.where(pair >= 3, 1, 0) + jnp.where(pair >= 5, 1, 0)
    hi = pair + 1 - 2 * jnp.where(pair >= 3, 1, 0) - jnp.where(pair >= 5, 1, 0)
    ea_ref[...] = EXPERTS_PER_GROUP * grp + lo
    eb_ref[...] = EXPERTS_PER_GROUP * grp + hi
    nu_ref[...] = jnp.broadcast_to(off * (1.0 / TM), nu_ref.shape).astype(jnp.int32)


def _slots(cls2d, *, n_slot_tiles):
    rows = cls2d.shape[0]
    rows_pad = -(-rows // V7X_LANES) * V7X_LANES
    n_t = -(-n_slot_tiles // V7X_LANES) * V7X_LANES
    i32 = jnp.int32
    return pl.pallas_call(
        functools.partial(_slots_kernel, rows_pad=rows_pad),
        out_shape=[jax.ShapeDtypeStruct(cls2d.shape, i32),
                   jax.ShapeDtypeStruct((1, n_t), i32), jax.ShapeDtypeStruct((1, n_t), i32),
                   jax.ShapeDtypeStruct((1, V7X_LANES), i32), jax.ShapeDtypeStruct((1, V7X_LANES), i32),
                   jax.ShapeDtypeStruct((1, V7X_LANES), i32)],
        compiler_params=pltpu.CompilerParams(vmem_limit_bytes=V7X_VMEM_LIMIT),
        name="moe_slots",
    )(cls2d)


def _row_copy(src, dst, src_row, dst_row, sem):
    return pltpu.make_async_copy(src.at[pl.ds(src_row, 1)], dst.at[pl.ds(dst_row, 1)], sem)


def _start_tile_rows(make_copy):
    for t in range(TM):
        make_copy(t).start()


def _dispatch_kernel(pos_ref, padlo_ref, padhi_ref, nu_ref, x_ref, xs_hbm, zbuf, sem, *, n_slot_tiles):
    @pl.when(pl.program_id(0) == 0)
    def _():
        zbuf[...] = jnp.zeros(zbuf.shape, zbuf.dtype)

        def zero_tile(t, carry):
            cp = pltpu.make_async_copy(zbuf, xs_hbm.at[pl.ds(pl.multiple_of(t * TM, TM), TM)], sem)
            cp.start()
            cp.wait()
            return carry

        lax.fori_loop(nu_ref[0], n_slot_tiles, zero_tile, 0)

        for c in range(N_CLASSES):
            lo, hi = padlo_ref[c], padhi_ref[c]

            def zero_row(s, carry):
                _row_copy(zbuf, xs_hbm, 0, s, sem).start()
                return carry

            def zero_wait(s, carry):
                _row_copy(zbuf, xs_hbm, 0, 0, sem).wait()
                return carry

            lax.fori_loop(lo, hi, zero_row, 0)
            lax.fori_loop(lo, hi, zero_wait, 0)

    _start_tile_rows(lambda t: _row_copy(x_ref, xs_hbm, t, pos_ref[0, 0, t], sem))
    pltpu.make_async_copy(x_ref, xs_hbm.at[pl.ds(0, TM)], sem).wait()


def _dispatch(pos3, pad_lo, pad_hi, n_used, x1g, *, n_slot_tiles):
    n_tiles = pos3.shape[0]
    width = x1g.shape[1]
    smem = pl.BlockSpec(memory_space=pltpu.SMEM)
    return pl.pallas_call(
        functools.partial(_dispatch_kernel, n_slot_tiles=n_slot_tiles),
        grid=(n_tiles,),
        in_specs=[pl.BlockSpec((1, 1, TM), lambda i: (i, 0, 0), memory_space=pltpu.SMEM),
                  smem, smem, smem,
                  pl.BlockSpec((TM, width), lambda i: (i, 0))],
        out_specs=pl.BlockSpec(memory_space=pl.ANY),
        out_shape=jax.ShapeDtypeStruct((n_slot_tiles * TM, width), x1g.dtype),
        scratch_shapes=[pltpu.VMEM((TM, width), x1g.dtype), pltpu.SemaphoreType.DMA(())],
        compiler_params=pltpu.CompilerParams(dimension_semantics=("arbitrary",), has_side_effects=True,
                                             vmem_limit_bytes=V7X_VMEM_LIMIT),
        name="moe_dispatch",
    )(pos3, pad_lo, pad_hi, n_used, x1g)


def _moe_kernel(ea_ref, eb_ref, nu_ref, xs_ref, wga, wua, wda, wgb, wub, wdb, ys_ref, *, d):
    del ea_ref, eb_ref
    used = pl.program_id(0) < nu_ref[0]

    @pl.when(jnp.logical_not(used))
    def _():
        ys_ref[...] = jnp.zeros(ys_ref.shape, ys_ref.dtype)

    @pl.when(used)
    def _():
        x = xs_ref[:, :d].astype(BF16)
        gate_a = xs_ref[:, d:d + 1]
        gate_b = xs_ref[:, d + 1:d + 2]

        def expert(wg, wu, wd):
            g = jnp.dot(x, wg[0], preferred_element_type=F32)
            u = jnp.dot(x, wu[0], preferred_element_type=F32)
            h = (g * jax.nn.sigmoid(g)) * u
            return jnp.dot(h.astype(BF16), wd[0], preferred_element_type=F32)

        ys_ref[...] = gate_a * expert(wga, wua, wda) + gate_b * expert(wgb, wub, wdb)


def _moe(tile_ea, tile_eb, n_used, xs, w_gate, w_up, w_down):
    n_slots, width = xs.shape
    d = width - GATE_LANES
    n_tiles = n_slots // TM
    d_e = w_gate.shape[-1]

    def row_map(i, ea, eb, nu):
        return (jnp.minimum(i, nu[0] - 1), 0)

    def w_a(i, ea, eb, nu):
        return (ea[i], 0, 0)

    def w_b(i, ea, eb, nu):
        return (eb[i], 0, 0)

    grid_spec = pltpu.PrefetchScalarGridSpec(
        num_scalar_prefetch=3,
        grid=(n_tiles,),
        in_specs=[pl.BlockSpec((TM, width), row_map),
                  pl.BlockSpec((1, d, d_e), w_a), pl.BlockSpec((1, d, d_e), w_a),
                  pl.BlockSpec((1, d_e, d), w_a),
                  pl.BlockSpec((1, d, d_e), w_b), pl.BlockSpec((1, d, d_e), w_b),
                  pl.BlockSpec((1, d_e, d), w_b)],
        out_specs=pl.BlockSpec((TM, d), lambda i, ea, eb, nu: (i, 0)),
    )
    return pl.pallas_call(
        functools.partial(_moe_kernel, d=d),
        grid_spec=grid_spec,
        out_shape=jax.ShapeDtypeStruct((n_slots, d), F32),
        compiler_params=_cparams(("arbitrary",)),
        name="moe_experts",
    )(tile_ea, tile_eb, n_used, xs, w_gate, w_up, w_down, w_gate, w_up, w_down)


def _combine_final_kernel(pos_ref, ys_hbm, x1_ref, g_ref, b_ref, yp_ref, yt_ref, ybuf, sem,
                          *, alpha, d, n_prompt_tiles):
    _start_tile_rows(lambda t: _row_copy(ys_hbm, ybuf, pos_ref[0, 0, t], t, sem))
    pltpu.make_async_copy(ys_hbm.at[pl.ds(0, TM)], ybuf, sem).wait()
    x2 = _layer_norm(alpha * x1_ref[:, :d] + ybuf[...], g_ref[...], b_ref[...])
    is_prompt = pl.program_id(0) < n_prompt_tiles

    @pl.when(is_prompt)
    def _():
        yp_ref[...] = x2

    @pl.when(jnp.logical_not(is_prompt))
    def _():
        yt_ref[...] = x2


def _combine_final(pos3, ys, x1g, g, b, *, alpha, n_prompt_tiles):
    t_pad, width = x1g.shape
    d = width - GATE_LANES
    n_tiles = t_pad // TM
    last_p = n_prompt_tiles - 1
    const = lambda i: (0, 0)
    return pl.pallas_call(
        functools.partial(_combine_final_kernel, alpha=alpha, d=d, n_prompt_tiles=n_prompt_tiles),
        grid=(n_tiles,),
        in_specs=[pl.BlockSpec((1, 1, TM), lambda i: (i, 0, 0), memory_space=pltpu.SMEM),
                  pl.BlockSpec(memory_space=pl.ANY),
                  pl.BlockSpec((TM, width), lambda i: (i, 0)),
                  pl.BlockSpec(g.shape, const), pl.BlockSpec(b.shape, const)],
        out_specs=[pl.BlockSpec((TM, d), lambda i: (jnp.minimum(i, last_p), 0)),
                   pl.BlockSpec((TM, d), lambda i: (jnp.maximum(i - n_prompt_tiles, 0), 0))],
        out_shape=[jax.ShapeDtypeStruct((n_prompt_tiles * TM, d), F32),
                   jax.ShapeDtypeStruct(((n_tiles - n_prompt_tiles) * TM, d), F32)],
        scratch_shapes=[pltpu.VMEM((TM, d), F32), pltpu.SemaphoreType.DMA(())],
        compiler_params=_cparams(("arbitrary",)),
        name="moe_combine_ln_final",
    )(pos3, ys, x1g, g, b)


def _combine_inproj_kernel(pos_ref, posn_ref, ys_hbm, x1_ref, g_ref, b_ref, w_ref,
                           x2_ref, h_ref, kv_ref, ybuf, sems,
                           *, alpha, d, n_tiles):
    i = pl.program_id(0)
    slot = i % 2

    def wait(s):
        pltpu.make_async_copy(ys_hbm.at[pl.ds(0, TM)], ybuf.at[s], sems.at[s]).wait()

    @pl.when(i == 0)
    def _():
        def start(t, carry):
            _row_copy(ys_hbm, ybuf.at[0], pos_ref[0, 0, t], t, sems.at[0]).start()
            return carry
        lax.fori_loop(0, TM, start, 0)

    for s in range(2):
        @pl.when(slot == 1 - s)
        def _():
            _start_tile_rows(lambda t: _row_copy(ys_hbm, ybuf.at[s], posn_ref[0, 0, t], t, sems.at[s]))

    wait(slot)
    sub = TM // OUTPROJ_SUBTILES
    w = w_ref[...]
    for s in range(OUTPROJ_SUBTILES):
        rows = slice(s * sub, (s + 1) * sub)
        x2 = _layer_norm(alpha * x1_ref[rows, :d] + ybuf[slot, rows, :], g_ref[...], b_ref[...])
        x2_ref[rows, :] = x2
        acc = jnp.dot(x2.astype(BF16), w, preferred_element_type=F32)
        h_ref[rows, :] = acc.astype(BF16)
        kv_ref[rows, : 2 * WIDTH_A] = acc[:, COL_KA:COL_QB]
        kv_ref[rows, 2 * WIDTH_A:] = acc[:, COL_KB:]

    @pl.when(i == n_tiles - 1)
    def _():
        wait(1 - slot)


def _combine_inproj(pos3, ys, x1g, g, b, w_in, *, alpha, seq, n_prompt_tiles):
    t_pad, width = x1g.shape
    d = width - GATE_LANES
    n_tiles = t_pad // TM
    tiles_per_seq = seq // TM
    n_seq = n_prompt_tiles // tiles_per_seq
    n_kv_blocks = n_seq + (n_tiles - n_prompt_tiles)
    const = lambda i: (0, 0)

    def kv_map(i):
        return (jnp.where(i < n_prompt_tiles, i // tiles_per_seq, n_seq + i - n_prompt_tiles), 0)

    return pl.pallas_call(
        functools.partial(_combine_inproj_kernel, alpha=alpha, d=d, n_tiles=n_tiles),
        grid=(n_tiles,),
        in_specs=[pl.BlockSpec((1, 1, TM), lambda i: (i, 0, 0), memory_space=pltpu.SMEM),
                  pl.BlockSpec((1, 1, TM), lambda i: (jnp.minimum(i + 1, n_tiles - 1), 0, 0),
                               memory_space=pltpu.SMEM),
                  pl.BlockSpec(memory_space=pl.ANY),
                  pl.BlockSpec((TM, width), lambda i: (i, 0)),
                  pl.BlockSpec(g.shape, const), pl.BlockSpec(b.shape, const),
                  pl.BlockSpec(w_in.shape, const)],
        out_specs=[pl.BlockSpec((TM, d), lambda i: (i, 0)),
                   pl.BlockSpec((TM, D_IN), lambda i: (i, 0)),
                   pl.BlockSpec((TM, KV_OUT), kv_map)],
        out_shape=[jax.ShapeDtypeStruct((t_pad, d), F32),
                   jax.ShapeDtypeStruct((t_pad, D_IN), BF16),
                   jax.ShapeDtypeStruct((n_kv_blocks * TM, KV_OUT), F32)],
        scratch_shapes=[pltpu.VMEM((2, TM, d), F32), pltpu.SemaphoreType.DMA((2,))],
        compiler_params=_cparams(("arbitrary",)),
        name="moe_combine_ln_inproj",
    )(pos3, pos3, ys, x1g, g, b, w_in)


def _alibi_slopes():
    return 2.0 ** (-8.0 * jnp.arange(1, N_HEADS_B + 1, dtype=F32) / N_HEADS_B)


def _rel_bias_table(rel_bias, q0, n_q, n_k):
    n_diag = n_q + n_k - 1
    dist = q0 + (n_q - 1) - jnp.arange(n_diag, dtype=jnp.int32)
    diag = jnp.take(rel_bias.astype(F32), jnp.clip(dist, -REL_CLIP, REL_CLIP) + REL_CLIP, axis=1)
    n_h = diag.shape[0]
    ext = jnp.concatenate([diag, jnp.zeros((n_h, 1), F32)], axis=1)
    shifted = jnp.tile(ext, (1, n_q))[:, :n_q * n_diag].reshape(n_h, n_q, n_diag)
    return shifted[:, :, n_q - 1:n_q - 1 + n_k]


def _alibi_table(q0, n_q, n_k):
    dist = q0 + jnp.arange(n_q, dtype=jnp.int32)[:, None] - jnp.arange(n_k, dtype=jnp.int32)[None, :]
    return -_alibi_slopes()[:, None, None] * jnp.abs(dist).astype(F32)[None]


def kernel(x_prompt, x_sample, cache_a_k, cache_a_v, cache_b_k, cache_b_v, w_in, rel_bias, attn_sinks,
           gn_a, gn_b, w_out, ln1_g, ln1_b, w_router, b_router, w_gate, w_up, w_down, ln2_g, ln2_b):
    n_seq, seq, d = x_prompt.shape
    n_dec, dec_seq, _ = x_sample.shape
    depth = w_in.shape[0]
    rows_ca, rows_cb = cache_a_k.shape[2], cache_b_k.shape[2]
    alpha = (2.0 * depth) ** 0.25
    assert seq % TM == 0 and PAD_A == TM and n_dec * dec_seq <= TM and TM % dec_seq == 0
    assert dec_seq % 16 == 0 and seq >= PAD_A

    t_prompt = n_seq * seq
    t_real = t_prompt + n_dec * dec_seq
    t_pad = -(-t_real // (2 * TM)) * (2 * TM)
    n_prompt_tiles = t_prompt // TM
    n_slot_tiles = t_pad // TM + N_CLASSES

    x = x_prompt.reshape(t_prompt, d)
    x_tail = jnp.concatenate([x_sample.reshape(n_dec * dec_seq, d), jnp.zeros((t_pad - t_real, d), F32)], axis=0)

    w_in_b = w_in.astype(BF16)
    w_out_b = w_out.astype(BF16)
    w_gate_b, w_up_b, w_down_b = w_gate.astype(BF16), w_up.astype(BF16), w_down.astype(BF16)
    wr_t = w_router.astype(F32).T
    wr_hi = wr_t.astype(BF16)
    wr_lo = (wr_t - wr_hi.astype(F32)).astype(BF16)
    br = b_router.astype(F32).reshape(N_EXPERTS, 1)

    alibi_p = _alibi_table(PAD_B, CHUNK, BAND_B * CHUNK)
    alibi_s = _alibi_table(rows_cb, dec_seq, rows_cb + dec_seq)

    pa_k, pa_v, pb_k, pb_v, sa_k, sa_v, sb_k, sb_v = ([] for _ in range(8))
    h, kv = _inproj(x, x_tail, w_in_b[0], seq=seq, n_prompt_tiles=n_prompt_tiles)
    for l in range(depth):
        sinks = attn_sinks[l].astype(F32)
        gna = gn_a[l].astype(F32).reshape(1, WIDTH_A)
        gnb = gn_b[l].astype(F32).reshape(1, WIDTH_B)
        bias_pa = _rel_bias_table(rel_bias[l], PAD_A, CHUNK, BAND_A * CHUNK)
        a_prompt = _attn_prompt(h, sinks, bias_pa, alibi_p, gna, gnb, n_seq=n_seq, seq=seq)
        bias_sa = _rel_bias_table(rel_bias[l], rows_ca, dec_seq, rows_ca + dec_seq)
        bias_s = (bias_sa[:, :, :rows_ca], bias_sa[:, :, rows_ca:],
                  alibi_s[:, :, :rows_cb], alibi_s[:, :, rows_cb:])
        a_sample = _attn_sample(
            h, sinks,
            cache_a_k[l].reshape(n_dec, rows_ca, WIDTH_A), cache_a_v[l].reshape(n_dec, rows_ca, WIDTH_A),
            cache_b_k[l].reshape(n_dec, rows_cb, KV_WIDTH_B), cache_b_v[l].reshape(n_dec, rows_cb, KV_WIDTH_B),
            bias_s, gna, gnb, first_row=t_prompt, n_rows_out=t_pad - t_prompt, n_dec=n_dec, dec_seq=dec_seq)

        x1g, cls = _outproj(a_prompt, a_sample, x, x_tail, w_out_b[l],
                            ln1_g[l].astype(F32).reshape(1, d), ln1_b[l].astype(F32).reshape(1, d),
                            wr_hi, wr_lo, br, n_tiles=t_pad // TM, n_prompt_tiles=n_prompt_tiles, alpha=alpha)
        pos, tile_ea, tile_eb, n_used, pad_lo, pad_hi = _slots(
            cls.reshape(t_pad // V7X_LANES, V7X_LANES), n_slot_tiles=n_slot_tiles)
        pos3 = pos.reshape(t_pad // TM, 1, TM)
        xs = _dispatch(pos3, pad_lo[0], pad_hi[0], n_used[0], x1g, n_slot_tiles=n_slot_tiles)
        ys = _moe(tile_ea[0, :n_slot_tiles], tile_eb[0, :n_slot_tiles], n_used[0, :1],
                  xs, w_gate_b[l], w_up_b[l], w_down_b[l])
        kv_l = kv
        g2, b2 = ln2_g[l].astype(F32).reshape(1, d), ln2_b[l].astype(F32).reshape(1, d)
        if l + 1 < depth:
            x, h, kv = _combine_inproj(pos3, ys, x1g, g2, b2, w_in_b[l + 1], alpha=alpha, seq=seq,
                                       n_prompt_tiles=n_prompt_tiles)
        else:
            y_p, y_t = _combine_final(pos3, ys, x1g, g2, b2, alpha=alpha, n_prompt_tiles=n_prompt_tiles)

        kv_p = kv_l[:n_seq * TM].reshape(n_seq, TM, KV_OUT)
        kv_s = kv_l[n_seq * TM:n_seq * TM + n_dec * dec_seq].reshape(n_dec, dec_seq, KV_OUT)
        ra, rb = min(PAD_A, seq), min(WINDOW_B, seq)
        pa_k.append(kv_p[:, TM - ra:, :WIDTH_A].reshape(n_seq, ra, N_HEADS_A, HEAD_DIM))
        pa_v.append(kv_p[:, TM - ra:, WIDTH_A:2 * WIDTH_A].reshape(n_seq, ra, N_HEADS_A, HEAD_DIM))
        pb_k.append(kv_p[:, TM - rb:, 2 * WIDTH_A:2 * WIDTH_A + KV_WIDTH_B].reshape(n_seq, rb, N_KV_B, HEAD_DIM))
        pb_v.append(kv_p[:, TM - rb:, 2 * WIDTH_A + KV_WIDTH_B:].reshape(n_seq, rb, N_KV_B, HEAD_DIM))
        sa_k.append(kv_s[:, :, :WIDTH_A].reshape(n_dec, dec_seq, N_HEADS_A, HEAD_DIM))
        sa_v.append(kv_s[:, :, WIDTH_A:2 * WIDTH_A].reshape(n_dec, dec_seq, N_HEADS_A, HEAD_DIM))
        sb_k.append(kv_s[:, :, 2 * WIDTH_A:2 * WIDTH_A + KV_WIDTH_B].reshape(n_dec, dec_seq, N_KV_B, HEAD_DIM))
        sb_v.append(kv_s[:, :, 2 * WIDTH_A + KV_WIDTH_B:].reshape(n_dec, dec_seq, N_KV_B, HEAD_DIM))

    y_prompt = y_p.reshape(n_seq, seq, d)
    y_sample = y_t[:n_dec * dec_seq].reshape(n_dec, dec_seq, d)
    return (y_prompt, y_sample,
            jnp.stack(pa_k), jnp.stack(pa_v), jnp.stack(pb_k), jnp.stack(pb_v),
            jnp.stack(sa_k), jnp.stack(sa_v), jnp.stack(sb_k), jnp.stack(sb_v))
```

```python
import functools

import jax
import jax.numpy as jnp
import numpy as np
from jax import lax
from jax.experimental import pallas as pl
from jax.experimental.pallas import tpu as pltpu

CHUNK = 64
HEAD_DIM = 64
N_HEADS_A = 8
BAND_A = 9
REL_CLIP = 128
N_HEADS_B = 8
N_KV_B = 2
WINDOW_B = 128
BAND_B = 1 + WINDOW_B // CHUNK
N_EXPERTS = 16
N_GROUPS = 4
EXPERTS_PER_GROUP = N_EXPERTS // N_GROUPS
PAIRS_PER_GROUP = 6
N_CLASSES = N_GROUPS * PAIRS_PER_GROUP
LN_EPS = 1e-5
RMS_EPS = 1e-6

WIDTH_A = N_HEADS_A * HEAD_DIM
WIDTH_B = N_HEADS_B * HEAD_DIM
KV_WIDTH_B = N_KV_B * HEAD_DIM
D_IN = 3 * WIDTH_A + WIDTH_B + 2 * KV_WIDTH_B
COL_QA, COL_KA, COL_VA = 0, WIDTH_A, 2 * WIDTH_A
COL_QB = 3 * WIDTH_A
COL_KB = COL_QB + WIDTH_B
COL_VB = COL_KB + KV_WIDTH_B
KV_OUT = 2 * WIDTH_A + 2 * KV_WIDTH_B
PAD_A = (BAND_A - 1) * CHUNK
PAD_B = WINDOW_B

V7X_LANES = 128
TM = 512
GATE_LANES = V7X_LANES
U32 = jnp.uint32


def _pack_bf16_pairs(lo, hi):
    lo_bits = lax.bitcast_convert_type(lo.astype(jnp.bfloat16).astype(jnp.float32), U32)
    hi_bits = lax.bitcast_convert_type(hi.astype(jnp.bfloat16).astype(jnp.float32), U32)
    return (lo_bits >> 16) | hi_bits


def _unpack_bf16_pairs(packed):
    lo = lax.bitcast_convert_type(packed << 16, jnp.float32)
    hi = lax.bitcast_convert_type(packed & jnp.uint32(0xFFFF0000), jnp.float32)
    return lo.astype(jnp.bfloat16), hi.astype(jnp.bfloat16)
V7X_VMEM_LIMIT = 56 * 1024 * 1024

F32 = jnp.float32
BF16 = jnp.bfloat16
NEG_INF = float("-inf")


def _cparams(sem):
    return pltpu.CompilerParams(dimension_semantics=sem, vmem_limit_bytes=V7X_VMEM_LIMIT)


def _inproj_kernel(xp_ref, xt_ref, w_ref, h_ref, kv_ref, *, tiles_per_seq, n_prompt_tiles):
    i = pl.program_id(0)
    x = jnp.where(i < n_prompt_tiles, xp_ref[...], xt_ref[...])
    acc = jnp.dot(x.astype(BF16), w_ref[...], preferred_element_type=F32)
    h_ref[...] = acc.astype(BF16)
    keeps_kv = jnp.logical_or(i % tiles_per_seq == tiles_per_seq - 1, i >= n_prompt_tiles)

    @pl.when(keeps_kv)
    def _():
        kv_ref[:, : 2 * WIDTH_A] = acc[:, COL_KA:COL_QB]
        kv_ref[:, 2 * WIDTH_A:] = acc[:, COL_KB:]


def _inproj(x_prompt, x_tail, w_in, *, seq, n_prompt_tiles):
    d = x_prompt.shape[1]
    t_pad = x_prompt.shape[0] + x_tail.shape[0]
    n_tiles = t_pad // TM
    last_p = n_prompt_tiles - 1
    tiles_per_seq = seq // TM
    n_seq = n_prompt_tiles // tiles_per_seq
    n_kv_blocks = n_seq + (n_tiles - n_prompt_tiles)

    def kv_map(i):
        return (jnp.where(i < n_prompt_tiles, i // tiles_per_seq, n_seq + i - n_prompt_tiles), 0)

    return pl.pallas_call(
        functools.partial(_inproj_kernel, tiles_per_seq=tiles_per_seq, n_prompt_tiles=n_prompt_tiles),
        grid=(n_tiles,),
        in_specs=[pl.BlockSpec((TM, d), lambda i: (jnp.minimum(i, last_p), 0)),
                  pl.BlockSpec((TM, d), lambda i: (jnp.maximum(i - n_prompt_tiles, 0), 0)),
                  pl.BlockSpec((d, D_IN), lambda i: (0, 0))],
        out_specs=[pl.BlockSpec((TM, D_IN), lambda i: (i, 0)),
                   pl.BlockSpec((TM, KV_OUT), kv_map)],
        out_shape=[jax.ShapeDtypeStruct((t_pad, D_IN), BF16),
                   jax.ShapeDtypeStruct((n_kv_blocks * TM, KV_OUT), F32)],
        compiler_params=_cparams(("arbitrary",)),
        name="inproj",
    )(x_prompt, x_tail, w_in)


def _softmax_pv(s_parts, v_parts, sink=None):
    m = s_parts[0].max(axis=-1, keepdims=True)
    for s in s_parts[1:]:
        m = jnp.maximum(m, s.max(axis=-1, keepdims=True))
    if sink is not None:
        m = jnp.maximum(m, sink)
    denom = None
    acc = None
    for s, v in zip(s_parts, v_parts):
        e = jnp.exp(s - m)
        d = e.sum(axis=-1, keepdims=True)
        pv = jnp.dot(e.astype(BF16), v, preferred_element_type=F32)
        denom = d if denom is None else denom + d
        acc = pv if acc is None else acc + pv
    if sink is not None:
        denom = denom + jnp.exp(sink - m)
    return acc / denom


def _nt_dot(a, b):
    return lax.dot_general(a, b, (((1,), (1,)), ((), ())), preferred_element_type=F32)


def _head_pair_lane_mask(rows):
    return lax.broadcasted_iota(jnp.int32, (rows, 2 * HEAD_DIM), 1) < HEAD_DIM


def _rms_store(o_ref, row_slice, col0, pairs, gn_ref):
    ssq = None
    for o in pairs:
        s = (o * o).sum(axis=-1, keepdims=True)
        ssq = s if ssq is None else ssq + s
    width = len(pairs) * 2 * HEAD_DIM
    inv = lax.rsqrt(ssq / width + RMS_EPS)
    for p, o in enumerate(pairs):
        c0 = p * 2 * HEAD_DIM
        g = gn_ref[:, c0:c0 + 2 * HEAD_DIM]
        o_ref[row_slice, col0 + c0:col0 + c0 + 2 * HEAD_DIM] = (o * inv * g).astype(o_ref.dtype)


KEY_BLOCK = V7X_LANES


def _swap_halves(x):
    return jnp.concatenate([x[:, HEAD_DIM:], x[:, :HEAD_DIM]], axis=1)


def _pair_rows(q2, lane_lo, kv_half=None):
    hi_lanes = jnp.logical_not(lane_lo)
    zero = jnp.zeros_like(q2)
    if kv_half is None:
        lo, hi = jnp.where(lane_lo, q2, zero), jnp.where(hi_lanes, q2, zero)
    elif kv_half == 0:
        lo, hi = jnp.where(lane_lo, q2, zero), jnp.where(lane_lo, _swap_halves(q2), zero)
    else:
        lo, hi = jnp.where(hi_lanes, _swap_halves(q2), zero), jnp.where(hi_lanes, q2, zero)
    return jnp.concatenate([lo, hi], axis=0)


def _tn_dot(a, b):
    return lax.dot_general(a, b, (((0,), (0,)), ((), ())), preferred_element_type=F32)


PV_KEYS = 256


def _score_blocks(k_ref, lanes, row0, n_keys, w):
    out = []
    for off in range(0, n_keys, KEY_BLOCK):
        n = min(KEY_BLOCK, n_keys - off)
        start = pl.multiple_of(row0 + off, CHUNK)
        out.append((_nt_dot(k_ref[pl.ds(start, n), lanes], w), start, n, off))
    return out


def _pair_probs(score_blocks, lane_tile, bias_ref, pair, pad, masked, sink_block=None):
    lanes = slice(lane_tile * V7X_LANES, (lane_tile + 1) * V7X_LANES)
    blocks = []
    for s_all, start, n, off in score_blocks:
        s = s_all[:, lanes] + bias_ref[pair, off:off + n, :]
        if masked:
            key_row = lax.broadcasted_iota(jnp.int32, s.shape, 0) + start
            s = jnp.where(key_row >= pad, s, NEG_INF)
        blocks.append((s, start, n))
    m = None
    for s, _, _ in blocks:
        bm = s.max(axis=0, keepdims=True)
        m = bm if m is None else jnp.maximum(m, bm)
    if sink_block is not None:
        m = jnp.maximum(m, sink_block.max(axis=0, keepdims=True))
    probs = [(jnp.exp(s - m).astype(BF16), start, n) for s, start, n in blocks]
    sink_p = None if sink_block is None else jnp.exp(sink_block - m).astype(BF16)
    return probs, sink_p


def _pair_values(probs, sink_p, v_ref, lanes):
    groups, cur, cur_rows = [], [], 0
    for blk in probs:
        if cur and cur_rows + blk[2] > PV_KEYS:
            groups.append(cur)
            cur, cur_rows = [], 0
        cur.append(blk)
        cur_rows += blk[2]
    groups.append(cur)
    acc = None
    for gi, grp in enumerate(groups):
        rows = sum(n for _, _, n in grp)
        p = grp[0][0] if len(grp) == 1 else jnp.concatenate([b[0] for b in grp], axis=0)
        v1 = jnp.concatenate([v_ref[pl.ds(grp[0][1], rows), lanes], jnp.ones((rows, V7X_LANES), BF16)], axis=1)
        if sink_p is not None and gi == len(groups) - 1:
            n = sink_p.shape[0]
            p = jnp.concatenate([p, sink_p], axis=0)
            sink_v = jnp.concatenate([jnp.zeros((n, V7X_LANES), BF16), jnp.ones((n, V7X_LANES), BF16)], axis=1)
            v1 = jnp.concatenate([v1, sink_v], axis=0)
        c = _tn_dot(p, v1)
        acc = c if acc is None else acc + c
    return acc[:, :V7X_LANES] / acc[:, V7X_LANES:]


def _quad_rows(q4):
    head_of_lane = lax.broadcasted_iota(jnp.int32, q4.shape, 1) // HEAD_DIM
    zero = jnp.zeros_like(q4)
    return jnp.concatenate([jnp.where(head_of_lane == h, q4, zero) for h in range(4)], axis=0)


def _attn_prompt_kernel(qa_ref, ka_ref, va_ref, qb_ref, kb_ref, vb_ref,
                        ba_ref, bb_ref, sink_ref, gna_ref, gnb_ref, o_ref,
                        kpa, vpa, kpb, vpb, pa_even, pb_even, pa_odd, pb_odd, *, seq, tq):
    j = pl.program_id(1)

    @pl.when(j == 0)
    def _():
        kpa[0:PAD_A, :] = jnp.zeros((PAD_A, WIDTH_A), BF16)
        vpa[0:PAD_A, :] = jnp.zeros((PAD_A, WIDTH_A), BF16)
        kpb[0:PAD_B, :] = jnp.zeros((PAD_B, KV_WIDTH_B), BF16)
        vpb[0:PAD_B, :] = jnp.zeros((PAD_B, KV_WIDTH_B), BF16)
        kpa[PAD_A:PAD_A + seq, :] = ka_ref[...]
        vpa[PAD_A:PAD_A + seq, :] = va_ref[...]
        kpb[PAD_B:PAD_B + seq, :] = kb_ref[...]
        vpb[PAD_B:PAD_B + seq, :] = vb_ref[...]

    lane_lo = _head_pair_lane_mask(CHUNK)
    scale = HEAD_DIM ** -0.5
    all_lanes = slice(0, KV_WIDTH_B)

    n_pairs_a, n_pairs_b = N_HEADS_A // 2, N_HEADS_B // 2
    band_a, band_b = BAND_A * CHUNK, BAND_B * CHUNK
    n_sink = sink_ref.shape[1]
    even_bufs, odd_bufs = (pa_even, pb_even), (pa_odd, pb_odd)

    def band_start(c):
        return pl.multiple_of(j * tq + c * CHUNK, CHUNK)

    def probs_phase(c, masked, bufs):
        pa_buf, pb_buf = bufs
        rows = pl.ds(pl.multiple_of(c * CHUNK, CHUNK), CHUNK)
        cs = band_start(c)
        scores_a = []
        for quad in range(N_HEADS_A // 4):
            lanes = slice(quad * 4 * HEAD_DIM, (quad + 1) * 4 * HEAD_DIM)
            w = _quad_rows(qa_ref[rows, lanes] * scale)
            scores_a.append(_score_blocks(kpa, lanes, cs, band_a, w))
        w = jnp.concatenate(
            [_pair_rows(qb_ref[rows, p * 2 * HEAD_DIM:(p + 1) * 2 * HEAD_DIM] * scale, lane_lo,
                        kv_half=(2 * p) // (N_HEADS_B // N_KV_B))
             for p in range(n_pairs_b)], axis=0)
        scores_b = _score_blocks(kpb, all_lanes, cs, band_b, w)
        for p in range(n_pairs_a):
            probs, _ = _pair_probs(scores_a[p // 2], p % 2, ba_ref, p, PAD_A, masked)
            for blk, (_, _, n, off) in zip(probs, scores_a[p // 2]):
                pa_buf[p, off:off + n, :] = blk[0]
        for p in range(n_pairs_b):
            probs, sink_p = _pair_probs(scores_b, p, bb_ref, p, PAD_B, masked, sink_block=sink_ref[p])
            for blk, (_, _, n, off) in zip(probs, scores_b):
                pb_buf[p, off:off + n, :] = blk[0]
            pb_buf[p, band_b:band_b + n_sink, :] = sink_p

    def values_phase(c, bufs):
        pa_buf, pb_buf = bufs
        rows = pl.ds(pl.multiple_of(c * CHUNK, CHUNK), CHUNK)
        cs = band_start(c)

        def groups(buf, p, n_keys):
            return [(buf[p, off:off + min(PV_KEYS, n_keys - off), :], pl.multiple_of(cs + off, CHUNK),
                     min(PV_KEYS, n_keys - off)) for off in range(0, n_keys, PV_KEYS)]

        pairs_a = []
        for p in range(n_pairs_a):
            lanes = slice(p * 2 * HEAD_DIM, (p + 1) * 2 * HEAD_DIM)
            o = _pair_values(groups(pa_buf, p, band_a), None, vpa, lanes)
            pairs_a.append(jnp.where(lane_lo, o[:CHUNK], o[CHUNK:]))
        _rms_store(o_ref, rows, 0, pairs_a, gna_ref)

        pairs_b = []
        for p in range(n_pairs_b):
            kv_half = (2 * p) // (N_HEADS_B // N_KV_B)
            o = _pair_values(groups(pb_buf, p, band_b), pb_buf[p, band_b:band_b + n_sink, :], vpb, all_lanes)
            lo, hi = o[:CHUNK], o[CHUNK:]
            if kv_half == 0:
                hi = _swap_halves(hi)
            else:
                lo = _swap_halves(lo)
            pairs_b.append(jnp.where(lane_lo, lo, hi))
        _rms_store(o_ref, rows, WIDTH_A, pairs_b, gnb_ref)

    def run(masked):
        n_chunks = tq // CHUNK
        bufs = (even_bufs, odd_bufs)
        probs_phase(0, masked, even_bufs)
        for c in range(1, n_chunks):
            probs_phase(c, masked, bufs[c % 2])
            values_phase(c - 1, bufs[(c - 1) % 2])
        values_phase(n_chunks - 1, bufs[(n_chunks - 1) % 2])

    @pl.when(j == 0)
    def _():
        run(True)

    @pl.when(j > 0)
    def _():
        run(False)


def _pair_transposed(table):
    n_h, n_q, n_k = table.shape
    return table.reshape(n_h // 2, 2, n_q, n_k).transpose(0, 3, 1, 2).reshape(n_h // 2, n_k, 2 * n_q)


def _sink_blocks(sinks):
    row = jnp.repeat(sinks.astype(F32).reshape(N_HEADS_B // 2, 2), HEAD_DIM, axis=1)
    rest = jnp.full((N_HEADS_B // 2, 15, 2 * HEAD_DIM), NEG_INF, F32)
    return jnp.concatenate([row[:, None, :], rest], axis=1)


def _attn_prompt(h, sinks, bias_a, bias_b, gn_a, gn_b, *, n_seq, seq):
    tq = TM
    assert tq >= PAD_A and tq >= PAD_B and tq % CHUNK == 0
    n_q = seq // tq
    vmem = pltpu.VMEM
    bias_a, bias_b = _pair_transposed(bias_a), _pair_transposed(bias_b)
    sinks = _sink_blocks(sinks)
    return pl.pallas_call(
        functools.partial(_attn_prompt_kernel, seq=seq, tq=tq),
        grid=(n_seq, n_q),
        in_specs=[
            pl.BlockSpec((tq, WIDTH_A), lambda b, j: (b * n_q + j, COL_QA // WIDTH_A)),
            pl.BlockSpec((seq, WIDTH_A), lambda b, j: (b, COL_KA // WIDTH_A)),
            pl.BlockSpec((seq, WIDTH_A), lambda b, j: (b, COL_VA // WIDTH_A)),
            pl.BlockSpec((tq, WIDTH_B), lambda b, j: (b * n_q + j, COL_QB // WIDTH_B)),
            pl.BlockSpec((seq, KV_WIDTH_B), lambda b, j: (b, COL_KB // KV_WIDTH_B)),
            pl.BlockSpec((seq, KV_WIDTH_B), lambda b, j: (b, COL_VB // KV_WIDTH_B)),
            pl.BlockSpec(bias_a.shape, lambda b, j: (0, 0, 0)),
            pl.BlockSpec(bias_b.shape, lambda b, j: (0, 0, 0)),
            pl.BlockSpec(sinks.shape, lambda b, j: (0, 0, 0)),
            pl.BlockSpec(gn_a.shape, lambda b, j: (0, 0)),
            pl.BlockSpec(gn_b.shape, lambda b, j: (0, 0)),
        ],
        out_specs=pl.BlockSpec((tq, WIDTH_A + WIDTH_B), lambda b, j: (b * n_q + j, 0)),
        out_shape=jax.ShapeDtypeStruct((n_seq * seq, WIDTH_A + WIDTH_B), BF16),
        scratch_shapes=[vmem((PAD_A + seq, WIDTH_A), BF16), vmem((PAD_A + seq, WIDTH_A), BF16),
                        vmem((PAD_B + seq, KV_WIDTH_B), BF16), vmem((PAD_B + seq, KV_WIDTH_B), BF16),
                        vmem((N_HEADS_A // 2, BAND_A * CHUNK, V7X_LANES), BF16),
                        vmem((N_HEADS_B // 2, BAND_B * CHUNK + sinks.shape[1], V7X_LANES), BF16),
                        vmem((N_HEADS_A // 2, BAND_A * CHUNK, V7X_LANES), BF16),
                        vmem((N_HEADS_B // 2, BAND_B * CHUNK + sinks.shape[1], V7X_LANES), BF16)],
        compiler_params=_cparams(("arbitrary", "arbitrary")),
        name="attn_prompt",
    )(h, h, h, h, h, h, bias_a, bias_b, sinks, gn_a, gn_b)


def _attn_sample_kernel(sink_ref, qa_ref, ka_ref, va_ref, qb_ref, kb_ref, vb_ref,
                        cka_ref, cva_ref, ckb_ref, cvb_ref,
                        bac_ref, ban_ref, bbc_ref, bbn_ref, gna_ref, gnb_ref, o_ref, *, n_dec, dec_seq):
    b = pl.program_id(0)

    @pl.when(b >= n_dec)
    def _():
        o_ref[...] = jnp.zeros(o_ref.shape, o_ref.dtype)

    @pl.when(b < n_dec)
    def _():
        lane_lo = _head_pair_lane_mask(dec_seq)
        scale = HEAD_DIM ** -0.5
        rows = slice(0, dec_seq)

        pairs_a = []
        for p in range(N_HEADS_A // 2):
            lanes = slice(p * 2 * HEAD_DIM, (p + 1) * 2 * HEAD_DIM)
            q2 = qa_ref[:, lanes] * scale
            kc = cka_ref[0, :, lanes].astype(BF16)
            vc = cva_ref[0, :, lanes].astype(BF16)
            kn = ka_ref[:, lanes]
            vn = va_ref[:, lanes]
            halves = []
            for half in range(2):
                h = 2 * p + half
                keep = lane_lo if half == 0 else jnp.logical_not(lane_lo)
                qm = jnp.where(keep, q2, jnp.zeros_like(q2))
                s_c = _nt_dot(qm, kc) + bac_ref[h]
                s_n = _nt_dot(qm, kn) + ban_ref[h]
                halves.append(_softmax_pv([s_c, s_n], [vc, vn]))
            pairs_a.append(jnp.where(lane_lo, halves[0], halves[1]))
        _rms_store(o_ref, rows, 0, pairs_a, gna_ref)

        kc = ckb_ref[0].astype(BF16)
        vc = cvb_ref[0].astype(BF16)
        kn = kb_ref[...]
        vn = vb_ref[...]
        pairs_b = []
        for p in range(N_HEADS_B // 2):
            lanes = slice(p * 2 * HEAD_DIM, (p + 1) * 2 * HEAD_DIM)
            q2 = qb_ref[:, lanes] * scale
            halves = []
            for half in range(2):
                h = 2 * p + half
                kv_head = h // (N_HEADS_B // N_KV_B)
                keep = lane_lo if half == 0 else jnp.logical_not(lane_lo)
                qm = jnp.where(keep, q2, jnp.zeros_like(q2))
                if kv_head != half:
                    qm = jnp.concatenate([qm[:, HEAD_DIM:], qm[:, :HEAD_DIM]], axis=1)
                s_c = _nt_dot(qm, kc) + bbc_ref[h]
                s_n = _nt_dot(qm, kn) + bbn_ref[h]
                o = _softmax_pv([s_c, s_n], [vc, vn], sink=sink_ref[h])
                if kv_head != half:
                    o = jnp.concatenate([o[:, HEAD_DIM:], o[:, :HEAD_DIM]], axis=1)
                halves.append(o)
            pairs_b.append(jnp.where(lane_lo, halves[0], halves[1]))
        _rms_store(o_ref, rows, WIDTH_A, pairs_b, gnb_ref)


def _attn_sample(h, sinks, cka, cva, ckb, cvb, bias, gn_a, gn_b, *, first_row, n_rows_out, n_dec, dec_seq):
    bac, ban, bbc, bbn = bias
    n_steps = n_rows_out // dec_seq
    rb = first_row // dec_seq
    last = n_dec - 1

    def hrow(b):
        return rb + jnp.minimum(b, last)

    def cache_spec(c):
        return pl.BlockSpec((1,) + c.shape[1:], lambda b: (jnp.minimum(b, last), 0, 0))

    def const_spec(a):
        return pl.BlockSpec(a.shape, lambda b: (0,) * a.ndim)

    return pl.pallas_call(
        functools.partial(_attn_sample_kernel, n_dec=n_dec, dec_seq=dec_seq),
        grid=(n_steps,),
        in_specs=[
            pl.BlockSpec(memory_space=pltpu.SMEM),
            pl.BlockSpec((dec_seq, WIDTH_A), lambda b: (hrow(b), COL_QA // WIDTH_A)),
            pl.BlockSpec((dec_seq, WIDTH_A), lambda b: (hrow(b), COL_KA // WIDTH_A)),
            pl.BlockSpec((dec_seq, WIDTH_A), lambda b: (hrow(b), COL_VA // WIDTH_A)),
            pl.BlockSpec((dec_seq, WIDTH_B), lambda b: (hrow(b), COL_QB // WIDTH_B)),
            pl.BlockSpec((dec_seq, KV_WIDTH_B), lambda b: (hrow(b), COL_KB // KV_WIDTH_B)),
            pl.BlockSpec((dec_seq, KV_WIDTH_B), lambda b: (hrow(b), COL_VB // KV_WIDTH_B)),
            cache_spec(cka), cache_spec(cva), cache_spec(ckb), cache_spec(cvb),
            const_spec(bac), const_spec(ban), const_spec(bbc), const_spec(bbn),
            const_spec(gn_a), const_spec(gn_b),
        ],
        out_specs=pl.BlockSpec((dec_seq, WIDTH_A + WIDTH_B), lambda b: (b, 0)),
        out_shape=jax.ShapeDtypeStruct((n_rows_out, WIDTH_A + WIDTH_B), BF16),
        compiler_params=_cparams(("arbitrary",)),
        name="attn_sample",
    )(sinks, h, h, h, h, h, h, cka, cva, ckb, cvb, bac, ban, bbc, bbn, gn_a, gn_b)


def _layer_norm(r, g, b):
    mu = r.mean(axis=-1, keepdims=True)
    c = r - mu
    var = (c * c).mean(axis=-1, keepdims=True)
    return c * lax.rsqrt(var + LN_EPS) * g + b


def _route_rows(logits):
    rows = [logits[e:e + 1, :] for e in range(N_EXPERTS)]

    def top2_sum(a, b, c, d):
        hi1, lo1 = jnp.maximum(a, b), jnp.minimum(a, b)
        hi2, lo2 = jnp.maximum(c, d), jnp.minimum(c, d)
        return jnp.maximum(hi1, hi2) + jnp.maximum(jnp.minimum(hi1, hi2), jnp.maximum(lo1, lo2))

    scores = [top2_sum(*rows[4 * g:4 * g + 4]) for g in range(N_GROUPS)]
    best = scores[0]
    g_sel = jnp.zeros(best.shape, jnp.int32)
    for g in range(1, N_GROUPS):
        upd = scores[g] > best
        best = jnp.where(upd, scores[g], best)
        g_sel = jnp.where(upd, g, g_sel)

    vals = []
    for k in range(EXPERTS_PER_GROUP):
        v = rows[k]
        for g in range(1, N_GROUPS):
            v = jnp.where(g_sel == g, rows[4 * g + k], v)
        vals.append(v)

    v1 = vals[0]
    i1 = jnp.zeros(v1.shape, jnp.int32)
    for k in range(1, EXPERTS_PER_GROUP):
        upd = vals[k] > v1
        v1 = jnp.where(upd, vals[k], v1)
        i1 = jnp.where(upd, k, i1)
    v2 = jnp.full(v1.shape, NEG_INF, F32)
    i2 = jnp.zeros(v1.shape, jnp.int32)
    for k in range(EXPERTS_PER_GROUP):
        upd = jnp.logical_and(i1 != k, vals[k] > v2)
        v2 = jnp.where(upd, vals[k], v2)
        i2 = jnp.where(upd, k, i2)

    e = jnp.exp(v2 - v1)
    den = 1.0 + e
    w1 = 1.0 / den
    w2 = e / den
    first_is_lo = i1 < i2
    lo = jnp.where(first_is_lo, i1, i2)
    hi = jnp.where(first_is_lo, i2, i1)
    gate_lo = jnp.where(first_is_lo, w1, w2)
    gate_hi = jnp.where(first_is_lo, w2, w1)
    pair_base = jnp.where(lo == 0, 0, jnp.where(lo == 1, 3, 5))
    cls = g_sel * PAIRS_PER_GROUP + pair_base + hi - lo - 1
    return cls, gate_lo, gate_hi


OUTPROJ_SUBTILES = 4


def _outproj_kernel(ap_ref, as_ref, xm_ref, xt_ref, w_ref, g_ref, b_ref, wrh_ref, wrl_ref, br_ref,
                    x1_ref, xpk_ref, cls_ref, *, n_prompt_tiles, n_main_tiles, alpha, d):
    i = pl.program_id(0)
    sub = TM // OUTPROJ_SUBTILES
    w = w_ref[...]
    ys = []
    for s in range(OUTPROJ_SUBTILES):
        rows = slice(s * sub, (s + 1) * sub)
        a = jnp.where(i < n_prompt_tiles, ap_ref[rows, :], as_ref[rows, :])
        ys.append(jnp.dot(a, w, preferred_element_type=F32))
    for s in range(OUTPROJ_SUBTILES):
        rows = slice(s * sub, (s + 1) * sub)
        x = jnp.where(i < n_main_tiles, xm_ref[rows, :], xt_ref[rows, :])
        x1 = _layer_norm(alpha * x + ys[s], g_ref[...], b_ref[...])
        x1_ref[rows, :] = x1
        xpk_ref[rows, :d // 2] = _pack_bf16_pairs(x1[:, :d // 2], x1[:, d // 2:])
        x_hi = x1.astype(BF16)
        x_lo = (x1 - x_hi.astype(F32)).astype(BF16)
        logits = (_nt_dot(wrh_ref[...], x_hi) + _nt_dot(wrh_ref[...], x_lo) + _nt_dot(wrl_ref[...], x_hi)
                  + br_ref[...])
        cls, gate_lo, gate_hi = _route_rows(logits)
        cls_ref[:, rows] = cls
        gates = jnp.concatenate([gate_lo, gate_hi, jnp.zeros((GATE_LANES - 2, sub), F32)], axis=0)
        xpk_ref[rows, d // 2:] = lax.bitcast_convert_type(gates.T, U32)


def _outproj(a_prompt, a_sample, x_main, x_tail, w_out, g, b, wr_hi, wr_lo, br, *, n_tiles, n_prompt_tiles, alpha):
    d = x_main.shape[1]
    t_pad = n_tiles * TM
    n_main_tiles = x_main.shape[0] // TM
    last_p = n_prompt_tiles - 1
    const = lambda i: (0, 0)
    return pl.pallas_call(
        functools.partial(_outproj_kernel, n_prompt_tiles=n_prompt_tiles, n_main_tiles=n_main_tiles,
                          alpha=alpha, d=d),
        grid=(n_tiles,),
        in_specs=[pl.BlockSpec((TM, d), lambda i: (jnp.minimum(i, last_p), 0)),
                  pl.BlockSpec((TM, d), lambda i: (jnp.maximum(i - n_prompt_tiles, 0), 0)),
                  pl.BlockSpec((TM, d), lambda i: (jnp.minimum(i, n_main_tiles - 1), 0)),
                  pl.BlockSpec((TM, d), lambda i: (jnp.maximum(i - n_main_tiles, 0), 0)),
                  pl.BlockSpec(w_out.shape, const),
                  pl.BlockSpec(g.shape, const), pl.BlockSpec(b.shape, const),
                  pl.BlockSpec(wr_hi.shape, const), pl.BlockSpec(wr_lo.shape, const),
                  pl.BlockSpec(br.shape, const)],
        out_specs=[pl.BlockSpec((TM, d), lambda i: (i, 0)),
                   pl.BlockSpec((TM, d // 2 + GATE_LANES), lambda i: (i, 0)),
                   pl.BlockSpec((1, TM), lambda i: (0, i))],
        out_shape=[jax.ShapeDtypeStruct((t_pad, d), F32),
                   jax.ShapeDtypeStruct((t_pad, d // 2 + GATE_LANES), U32),
                   jax.ShapeDtypeStruct((1, t_pad), jnp.int32)],
        compiler_params=_cparams(("arbitrary",)),
        name="outproj_ln_route",
    )(a_prompt, a_sample, x_main, x_tail, w_out, g, b, wr_hi, wr_lo, br)


def _slots_kernel(cls_ref, pos_ref, ea_ref, eb_ref, nu_ref, padlo_ref, padhi_ref, *, rows_pad):
    lane = lax.broadcasted_iota(jnp.int32, (1, V7X_LANES), 1)
    pad_lo = jnp.zeros((1, V7X_LANES), F32)
    pad_hi = jnp.zeros((1, V7X_LANES), F32)
    cls = cls_ref[...]
    rows = cls.shape[0]
    r_i = lax.broadcasted_iota(jnp.int32, (rows_pad, rows_pad), 0)
    c_i = lax.broadcasted_iota(jnp.int32, (rows_pad, rows_pad), 1)
    lower = jnp.where(c_i < r_i, 1.0, 0.0).astype(BF16)
    l_r = lax.broadcasted_iota(jnp.int32, (V7X_LANES, V7X_LANES), 0)
    l_c = lax.broadcasted_iota(jnp.int32, (V7X_LANES, V7X_LANES), 1)
    upper = jnp.where(l_r < l_c, 1.0, 0.0).astype(BF16)

    n_t = ea_ref.shape[1]
    tile_start = lax.broadcasted_iota(jnp.int32, (1, n_t), 1).astype(F32) * float(TM)
    tile_cls = jnp.zeros((1, n_t), jnp.int32)
    off = jnp.zeros((1, 1), F32)
    pos = jnp.zeros((rows, V7X_LANES), F32)
    for c in range(N_CLASSES):
        m = jnp.where(cls == c, 1.0, 0.0)
        mb = m.astype(BF16)
        if rows_pad > rows:
            mb_pad = jnp.concatenate([mb, jnp.zeros((rows_pad - rows, V7X_LANES), BF16)], axis=0)
        else:
            mb_pad = mb
        before_rows = jnp.dot(lower, mb_pad, preferred_element_type=F32)[:rows]
        before_rows = before_rows.sum(axis=-1, keepdims=True)
        before_lanes = jnp.dot(mb, upper, preferred_element_type=F32)
        pos = pos + m * (off + before_rows + before_lanes)
        if c > 0:
            tile_cls = tile_cls + jnp.where(tile_start >= off, 1, 0)
        count = m.sum(axis=-1, keepdims=True).sum(axis=0, keepdims=True)
        pad_lo = jnp.where(lane == c, off + count, pad_lo)
        off = off + jnp.floor((count + float(TM - 1)) * (1.0 / TM)) * float(TM)
        pad_hi = jnp.where(lane == c, off, pad_hi)
    pos_ref[...] = pos.astype(jnp.int32)
    padlo_ref[...] = pad_lo.astype(jnp.int32)
    padhi_ref[...] = pad_hi.astype(jnp.int32)
    grp = (jnp.where(tile_cls >= 6, 1, 0) + jnp.where(tile_cls >= 12, 1, 0)
           + jnp.where(tile_cls >= 18, 1, 0))
    pair = tile_cls - PAIRS_PER_GROUP * grp
    lo = jnp.where(pair >= 3, 1, 0) + jnp.where(pair >= 5, 1, 0)
    hi = pair + 1 - 2 * jnp.where(pair >= 3, 1, 0) - jnp.where(pair >= 5, 1, 0)
    ea_ref[...] = EXPERTS_PER_GROUP * grp + lo
    eb_ref[...] = EXPERTS_PER_GROUP * grp + hi
    nu_ref[...] = jnp.broadcast_to(off * (1.0 / TM), nu_ref.shape).astype(jnp.int32)


def _slots(cls2d, *, n_slot_tiles):
    rows = cls2d.shape[0]
    rows_pad = -(-rows // V7X_LANES) * V7X_LANES
    n_t = -(-n_slot_tiles // V7X_LANES) * V7X_LANES
    i32 = jnp.int32
    return pl.pallas_call(
        functools.partial(_slots_kernel, rows_pad=rows_pad),
        out_shape=[jax.ShapeDtypeStruct(cls2d.shape, i32),
                   jax.ShapeDtypeStruct((1, n_t), i32), jax.ShapeDtypeStruct((1, n_t), i32),
                   jax.ShapeDtypeStruct((1, V7X_LANES), i32), jax.ShapeDtypeStruct((1, V7X_LANES), i32),
                   jax.ShapeDtypeStruct((1, V7X_LANES), i32)],
        compiler_params=pltpu.CompilerParams(vmem_limit_bytes=V7X_VMEM_LIMIT),
        name="moe_slots",
    )(cls2d)


def _row_copy(src, dst, src_row, dst_row, sem):
    return pltpu.make_async_copy(src.at[pl.ds(src_row, 1)], dst.at[pl.ds(dst_row, 1)], sem)


def _start_tile_rows(make_copy):
    for t in range(TM):
        make_copy(t).start()


def _dispatch_kernel(pos_ref, padlo_ref, padhi_ref, nu_ref, x_ref, xs_hbm, zbuf, sem, *, n_slot_tiles):
    @pl.when(pl.program_id(0) == 0)
    def _():
        zbuf[...] = jnp.zeros(zbuf.shape, zbuf.dtype)

        def zero_tile(t, carry):
            cp = pltpu.make_async_copy(zbuf, xs_hbm.at[pl.ds(pl.multiple_of(t * TM, TM), TM)], sem)
            cp.start()
            cp.wait()
            return carry

        lax.fori_loop(nu_ref[0], n_slot_tiles, zero_tile, 0)

        for c in range(N_CLASSES):
            lo, hi = padlo_ref[c], padhi_ref[c]

            def zero_row(s, carry):
                _row_copy(zbuf, xs_hbm, 0, s, sem).start()
                return carry

            def zero_wait(s, carry):
                _row_copy(zbuf, xs_hbm, 0, 0, sem).wait()
                return carry

            lax.fori_loop(lo, hi, zero_row, 0)
            lax.fori_loop(lo, hi, zero_wait, 0)

    _start_tile_rows(lambda t: _row_copy(x_ref, xs_hbm, t, pos_ref[0, 0, t], sem))
    pltpu.make_async_copy(x_ref, xs_hbm.at[pl.ds(0, TM)], sem).wait()


def _dispatch(pos3, pad_lo, pad_hi, n_used, x1g, *, n_slot_tiles):
    n_tiles = pos3.shape[0]
    width = x1g.shape[1]
    smem = pl.BlockSpec(memory_space=pltpu.SMEM)
    return pl.pallas_call(
        functools.partial(_dispatch_kernel, n_slot_tiles=n_slot_tiles),
        grid=(n_tiles,),
        in_specs=[pl.BlockSpec((1, 1, TM), lambda i: (i, 0, 0), memory_space=pltpu.SMEM),
                  smem, smem, smem,
                  pl.BlockSpec((TM, width), lambda i: (i, 0))],
        out_specs=pl.BlockSpec(memory_space=pl.ANY),
        out_shape=jax.ShapeDtypeStruct((n_slot_tiles * TM, width), x1g.dtype),
        scratch_shapes=[pltpu.VMEM((TM, width), x1g.dtype), pltpu.SemaphoreType.DMA(())],
        compiler_params=pltpu.CompilerParams(dimension_semantics=("arbitrary",), has_side_effects=True,
                                             vmem_limit_bytes=V7X_VMEM_LIMIT),
        name="moe_dispatch",
    )(pos3, pad_lo, pad_hi, n_used, x1g)


def _moe_kernel(ea_ref, eb_ref, nu_ref, xs_ref, wga, wua, wda, wgb, wub, wdb, ys_ref, *, d):
    del ea_ref, eb_ref
    used = pl.program_id(0) < nu_ref[0]

    @pl.when(jnp.logical_not(used))
    def _():
        ys_ref[...] = jnp.zeros(ys_ref.shape, ys_ref.dtype)

    @pl.when(used)
    def _():
        x = jnp.concatenate(_unpack_bf16_pairs(xs_ref[:, :d // 2]), axis=1)
        gates = lax.bitcast_convert_type(xs_ref[:, d // 2:], F32)
        gate_a = gates[:, 0:1]
        gate_b = gates[:, 1:2]

        def expert(wg, wu, wd):
            g = jnp.dot(x, wg[0], preferred_element_type=F32)
            u = jnp.dot(x, wu[0], preferred_element_type=F32)
            h = (g * jax.nn.sigmoid(g)) * u
            return jnp.dot(h.astype(BF16), wd[0], preferred_element_type=F32)

        ys_ref[...] = gate_a * expert(wga, wua, wda) + gate_b * expert(wgb, wub, wdb)


def _moe(tile_ea, tile_eb, n_used, xs, w_gate, w_up, w_down):
    n_slots, width = xs.shape
    d = 2 * (width - GATE_LANES)
    n_tiles = n_slots // TM
    d_e = w_gate.shape[-1]

    def row_map(i, ea, eb, nu):
        return (jnp.minimum(i, nu[0] - 1), 0)

    def w_a(i, ea, eb, nu):
        return (ea[i], 0, 0)

    def w_b(i, ea, eb, nu):
        return (eb[i], 0, 0)

    grid_spec = pltpu.PrefetchScalarGridSpec(
        num_scalar_prefetch=3,
        grid=(n_tiles,),
        in_specs=[pl.BlockSpec((TM, width), row_map),
                  pl.BlockSpec((1, d, d_e), w_a), pl.BlockSpec((1, d, d_e), w_a),
                  pl.BlockSpec((1, d_e, d), w_a),
                  pl.BlockSpec((1, d, d_e), w_b), pl.BlockSpec((1, d, d_e), w_b),
                  pl.BlockSpec((1, d_e, d), w_b)],
        out_specs=pl.BlockSpec((TM, d), lambda i, ea, eb, nu: (i, 0)),
    )
    return pl.pallas_call(
        functools.partial(_moe_kernel, d=d),
        grid_spec=grid_spec,
        out_shape=jax.ShapeDtypeStruct((n_slots, d), F32),
        compiler_params=_cparams(("arbitrary",)),
        name="moe_experts",
    )(tile_ea, tile_eb, n_used, xs, w_gate, w_up, w_down, w_gate, w_up, w_down)


def _combine_final_kernel(pos_ref, ys_hbm, x1_ref, g_ref, b_ref, yp_ref, yt_ref, ybuf, sem,
                          *, alpha, d, n_prompt_tiles):
    _start_tile_rows(lambda t: _row_copy(ys_hbm, ybuf, pos_ref[0, 0, t], t, sem))
    pltpu.make_async_copy(ys_hbm.at[pl.ds(0, TM)], ybuf, sem).wait()
    x2 = _layer_norm(alpha * x1_ref[:, :d] + ybuf[...], g_ref[...], b_ref[...])
    is_prompt = pl.program_id(0) < n_prompt_tiles

    @pl.when(is_prompt)
    def _():
        yp_ref[...] = x2

    @pl.when(jnp.logical_not(is_prompt))
    def _():
        yt_ref[...] = x2


def _combine_final(pos3, ys, x1g, g, b, *, alpha, n_prompt_tiles):
    t_pad, width = x1g.shape
    d = width
    n_tiles = t_pad // TM
    last_p = n_prompt_tiles - 1
    const = lambda i: (0, 0)
    return pl.pallas_call(
        functools.partial(_combine_final_kernel, alpha=alpha, d=d, n_prompt_tiles=n_prompt_tiles),
        grid=(n_tiles,),
        in_specs=[pl.BlockSpec((1, 1, TM), lambda i: (i, 0, 0), memory_space=pltpu.SMEM),
                  pl.BlockSpec(memory_space=pl.ANY),
                  pl.BlockSpec((TM, width), lambda i: (i, 0)),
                  pl.BlockSpec(g.shape, const), pl.BlockSpec(b.shape, const)],
        out_specs=[pl.BlockSpec((TM, d), lambda i: (jnp.minimum(i, last_p), 0)),
                   pl.BlockSpec((TM, d), lambda i: (jnp.maximum(i - n_prompt_tiles, 0), 0))],
        out_shape=[jax.ShapeDtypeStruct((n_prompt_tiles * TM, d), F32),
                   jax.ShapeDtypeStruct(((n_tiles - n_prompt_tiles) * TM, d), F32)],
        scratch_shapes=[pltpu.VMEM((TM, d), F32), pltpu.SemaphoreType.DMA(())],
        compiler_params=_cparams(("arbitrary",)),
        name="moe_combine_ln_final",
    )(pos3, ys, x1g, g, b)


def _combine_inproj_kernel(pos_ref, posn_ref, ys_hbm, x1_ref, g_ref, b_ref, w_ref,
                           x2_ref, h_ref, kv_ref, ybuf, sems,
                           *, alpha, d, n_tiles):
    i = pl.program_id(0)
    slot = i % 2

    def wait(s):
        pltpu.make_async_copy(ys_hbm.at[pl.ds(0, TM)], ybuf.at[s], sems.at[s]).wait()

    @pl.when(i == 0)
    def _():
        def start(t, carry):
            _row_copy(ys_hbm, ybuf.at[0], pos_ref[0, 0, t], t, sems.at[0]).start()
            return carry
        lax.fori_loop(0, TM, start, 0)

    for s in range(2):
        @pl.when(slot == 1 - s)
        def _():
            _start_tile_rows(lambda t: _row_copy(ys_hbm, ybuf.at[s], posn_ref[0, 0, t], t, sems.at[s]))

    wait(slot)
    sub = TM // OUTPROJ_SUBTILES
    w = w_ref[...]
    for s in range(OUTPROJ_SUBTILES):
        rows = slice(s * sub, (s + 1) * sub)
        x2 = _layer_norm(alpha * x1_ref[rows, :d] + ybuf[slot, rows, :], g_ref[...], b_ref[...])
        x2_ref[rows, :] = x2
        acc = jnp.dot(x2.astype(BF16), w, preferred_element_type=F32)
        h_ref[rows, :] = acc.astype(BF16)
        kv_ref[rows, : 2 * WIDTH_A] = acc[:, COL_KA:COL_QB]
        kv_ref[rows, 2 * WIDTH_A:] = acc[:, COL_KB:]

    @pl.when(i == n_tiles - 1)
    def _():
        wait(1 - slot)


def _combine_inproj(pos3, ys, x1g, g, b, w_in, *, alpha, seq, n_prompt_tiles):
    t_pad, width = x1g.shape
    d = width
    n_tiles = t_pad // TM
    tiles_per_seq = seq // TM
    n_seq = n_prompt_tiles // tiles_per_seq
    n_kv_blocks = n_seq + (n_tiles - n_prompt_tiles)
    const = lambda i: (0, 0)

    def kv_map(i):
        return (jnp.where(i < n_prompt_tiles, i // tiles_per_seq, n_seq + i - n_prompt_tiles), 0)

    return pl.pallas_call(
        functools.partial(_combine_inproj_kernel, alpha=alpha, d=d, n_tiles=n_tiles),
        grid=(n_tiles,),
        in_specs=[pl.BlockSpec((1, 1, TM), lambda i: (i, 0, 0), memory_space=pltpu.SMEM),
                  pl.BlockSpec((1, 1, TM), lambda i: (jnp.minimum(i + 1, n_tiles - 1), 0, 0),
                               memory_space=pltpu.SMEM),
                  pl.BlockSpec(memory_space=pl.ANY),
                  pl.BlockSpec((TM, width), lambda i: (i, 0)),
                  pl.BlockSpec(g.shape, const), pl.BlockSpec(b.shape, const),
                  pl.BlockSpec(w_in.shape, const)],
        out_specs=[pl.BlockSpec((TM, d), lambda i: (i, 0)),
                   pl.BlockSpec((TM, D_IN), lambda i: (i, 0)),
                   pl.BlockSpec((TM, KV_OUT), kv_map)],
        out_shape=[jax.ShapeDtypeStruct((t_pad, d), F32),
                   jax.ShapeDtypeStruct((t_pad, D_IN), BF16),
                   jax.ShapeDtypeStruct((n_kv_blocks * TM, KV_OUT), F32)],
        scratch_shapes=[pltpu.VMEM((2, TM, d), F32), pltpu.SemaphoreType.DMA((2,))],
        compiler_params=_cparams(("arbitrary",)),
        name="moe_combine_ln_inproj",
    )(pos3, pos3, ys, x1g, g, b, w_in)


def _alibi_slopes():
    return 2.0 ** (-8.0 * jnp.arange(1, N_HEADS_B + 1, dtype=F32) / N_HEADS_B)


def _rel_bias_table(rel_bias, q0, n_q, n_k):
    n_diag = n_q + n_k - 1
    dist = q0 + (n_q - 1) - jnp.arange(n_diag, dtype=jnp.int32)
    diag = jnp.take(rel_bias.astype(F32), jnp.clip(dist, -REL_CLIP, REL_CLIP) + REL_CLIP, axis=1)
    n_h = diag.shape[0]
    ext = jnp.concatenate([diag, jnp.zeros((n_h, 1), F32)], axis=1)
    shifted = jnp.tile(ext, (1, n_q))[:, :n_q * n_diag].reshape(n_h, n_q, n_diag)
    return shifted[:, :, n_q - 1:n_q - 1 + n_k]


def _alibi_table(q0, n_q, n_k):
    dist = q0 + jnp.arange(n_q, dtype=jnp.int32)[:, None] - jnp.arange(n_k, dtype=jnp.int32)[None, :]
    return -_alibi_slopes()[:, None, None] * jnp.abs(dist).astype(F32)[None]


def kernel(x_prompt, x_sample, cache_a_k, cache_a_v, cache_b_k, cache_b_v, w_in, rel_bias, attn_sinks,
           gn_a, gn_b, w_out, ln1_g, ln1_b, w_router, b_router, w_gate, w_up, w_down, ln2_g, ln2_b):
    n_seq, seq, d = x_prompt.shape
    n_dec, dec_seq, _ = x_sample.shape
    depth = w_in.shape[0]
    rows_ca, rows_cb = cache_a_k.shape[2], cache_b_k.shape[2]
    alpha = (2.0 * depth) ** 0.25
    assert seq % TM == 0 and PAD_A == TM and n_dec * dec_seq <= TM and TM % dec_seq == 0
    assert dec_seq % 16 == 0 and seq >= PAD_A

    t_prompt = n_seq * seq
    t_real = t_prompt + n_dec * dec_seq
    t_pad = -(-t_real // (2 * TM)) * (2 * TM)
    n_prompt_tiles = t_prompt // TM
    n_slot_tiles = t_pad // TM + N_CLASSES

    x = x_prompt.reshape(t_prompt, d)
    x_tail = jnp.concatenate([x_sample.reshape(n_dec * dec_seq, d), jnp.zeros((t_pad - t_real, d), F32)], axis=0)

    w_in_b = w_in.astype(BF16)
    w_out_b = w_out.astype(BF16)
    w_gate_b, w_up_b, w_down_b = w_gate.astype(BF16), w_up.astype(BF16), w_down.astype(BF16)
    wr_t = w_router.astype(F32).T
    wr_hi = wr_t.astype(BF16)
    wr_lo = (wr_t - wr_hi.astype(F32)).astype(BF16)
    br = b_router.astype(F32).reshape(N_EXPERTS, 1)

    alibi_p = _alibi_table(PAD_B, CHUNK, BAND_B * CHUNK)
    alibi_s = _alibi_table(rows_cb, dec_seq, rows_cb + dec_seq)

    pa_k, pa_v, pb_k, pb_v, sa_k, sa_v, sb_k, sb_v = ([] for _ in range(8))
    h, kv = _inproj(x, x_tail, w_in_b[0], seq=seq, n_prompt_tiles=n_prompt_tiles)
    for l in range(depth):
        sinks = attn_sinks[l].astype(F32)
        gna = gn_a[l].astype(F32).reshape(1, WIDTH_A)
        gnb = gn_b[l].astype(F32).reshape(1, WIDTH_B)
        bias_pa = _rel_bias_table(rel_bias[l], PAD_A, CHUNK, BAND_A * CHUNK)
        a_prompt = _attn_prompt(h, sinks, bias_pa, alibi_p, gna, gnb, n_seq=n_seq, seq=seq)
        bias_sa = _rel_bias_table(rel_bias[l], rows_ca, dec_seq, rows_ca + dec_seq)
        bias_s = (bias_sa[:, :, :rows_ca], bias_sa[:, :, rows_ca:],
                  alibi_s[:, :, :rows_cb], alibi_s[:, :, rows_cb:])
        a_sample = _attn_sample(
            h, sinks,
            cache_a_k[l].reshape(n_dec, rows_ca, WIDTH_A), cache_a_v[l].reshape(n_dec, rows_ca, WIDTH_A),
            cache_b_k[l].reshape(n_dec, rows_cb, KV_WIDTH_B), cache_b_v[l].reshape(n_dec, rows_cb, KV_WIDTH_B),
            bias_s, gna, gnb, first_row=t_prompt, n_rows_out=t_pad - t_prompt, n_dec=n_dec, dec_seq=dec_seq)

        x1g, xpk, cls = _outproj(a_prompt, a_sample, x, x_tail, w_out_b[l],
                            ln1_g[l].astype(F32).reshape(1, d), ln1_b[l].astype(F32).reshape(1, d),
                            wr_hi, wr_lo, br, n_tiles=t_pad // TM, n_prompt_tiles=n_prompt_tiles, alpha=alpha)
        pos, tile_ea, tile_eb, n_used, pad_lo, pad_hi = _slots(
            cls.reshape(t_pad // V7X_LANES, V7X_LANES), n_slot_tiles=n_slot_tiles)
        pos3 = pos.reshape(t_pad // TM, 1, TM)
        xs = _dispatch(pos3, pad_lo[0], pad_hi[0], n_used[0], xpk, n_slot_tiles=n_slot_tiles)
        ys = _moe(tile_ea[0, :n_slot_tiles], tile_eb[0, :n_slot_tiles], n_used[0, :1],
                  xs, w_gate_b[l], w_up_b[l], w_down_b[l])
        kv_l = kv
        g2, b2 = ln2_g[l].astype(F32).reshape(1, d), ln2_b[l].astype(F32).reshape(1, d)
        if l + 1 < depth:
            x, h, kv = _combine_inproj(pos3, ys, x1g, g2, b2, w_in_b[l + 1], alpha=alpha, seq=seq,
                                       n_prompt_tiles=n_prompt_tiles)
        else:
            y_p, y_t = _combine_final(pos3, ys, x1g, g2, b2, alpha=alpha, n_prompt_tiles=n_prompt_tiles)

        kv_p = kv_l[:n_seq * TM].reshape(n_seq, TM, KV_OUT)
        kv_s = kv_l[n_seq * TM:n_seq * TM + n_dec * dec_seq].reshape(n_dec, dec_seq, KV_OUT)
        ra, rb = min(PAD_A, seq), min(WINDOW_B, seq)
        pa_k.append(kv_p[:, TM - ra:, :WIDTH_A].reshape(n_seq, ra, N_HEADS_A, HEAD_DIM))
        pa_v.append(kv_p[:, TM - ra:, WIDTH_A:2 * WIDTH_A].reshape(n_seq, ra, N_HEADS_A, HEAD_DIM))
        pb_k.append(kv_p[:, TM - rb:, 2 * WIDTH_A:2 * WIDTH_A + KV_WIDTH_B].reshape(n_seq, rb, N_KV_B, HEAD_DIM))
        pb_v.append(kv_p[:, TM - rb:, 2 * WIDTH_A + KV_WIDTH_B:].reshape(n_seq, rb, N_KV_B, HEAD_DIM))
        sa_k.append(kv_s[:, :, :WIDTH_A].reshape(n_dec, dec_seq, N_HEADS_A, HEAD_DIM))
        sa_v.append(kv_s[:, :, WIDTH_A:2 * WIDTH_A].reshape(n_dec, dec_seq, N_HEADS_A, HEAD_DIM))
        sb_k.append(kv_s[:, :, 2 * WIDTH_A:2 * WIDTH_A + KV_WIDTH_B].reshape(n_dec, dec_seq, N_KV_B, HEAD_DIM))
        sb_v.append(kv_s[:, :, 2 * WIDTH_A + KV_WIDTH_B:].reshape(n_dec, dec_seq, N_KV_B, HEAD_DIM))

    y_prompt = y_p.reshape(n_seq, seq, d)
    y_sample = y_t[:n_dec * dec_seq].reshape(n_dec, dec_seq, d)
    return (y_prompt, y_sample,
            jnp.stack(pa_k), jnp.stack(pa_v), jnp.stack(pb_k), jnp.stack(pb_v),
            jnp.stack(sa_k), jnp.stack(sa_v), jnp.stack(sb_k), jnp.stack(sb_v))
```

```python
import functools

import jax
import jax.numpy as jnp
import numpy as np
from jax import lax
from jax.experimental import pallas as pl
from jax.experimental.pallas import tpu as pltpu

CHUNK = 64
HEAD_DIM = 64
N_HEADS_A = 8
BAND_A = 9
REL_CLIP = 128
N_HEADS_B = 8
N_KV_B = 2
WINDOW_B = 128
BAND_B = 1 + WINDOW_B // CHUNK
N_EXPERTS = 16
N_GROUPS = 4
EXPERTS_PER_GROUP = N_EXPERTS // N_GROUPS
PAIRS_PER_GROUP = 6
N_CLASSES = N_GROUPS * PAIRS_PER_GROUP
LN_EPS = 1e-5
RMS_EPS = 1e-6

WIDTH_A = N_HEADS_A * HEAD_DIM
WIDTH_B = N_HEADS_B * HEAD_DIM
KV_WIDTH_B = N_KV_B * HEAD_DIM
D_IN = 3 * WIDTH_A + WIDTH_B + 2 * KV_WIDTH_B
COL_QA, COL_KA, COL_VA = 0, WIDTH_A, 2 * WIDTH_A
COL_QB = 3 * WIDTH_A
COL_KB = COL_QB + WIDTH_B
COL_VB = COL_KB + KV_WIDTH_B
KV_OUT = 2 * WIDTH_A + 2 * KV_WIDTH_B
PAD_A = (BAND_A - 1) * CHUNK
PAD_B = WINDOW_B

V7X_LANES = 128
TM = 512
GATE_LANES = V7X_LANES
U32 = jnp.uint32


def _pack_bf16_pairs(lo, hi):
    lo_bits = lax.bitcast_convert_type(lo.astype(jnp.bfloat16).astype(jnp.float32), U32)
    hi_bits = lax.bitcast_convert_type(hi.astype(jnp.bfloat16).astype(jnp.float32), U32)
    return (lo_bits >> 16) | hi_bits


def _unpack_bf16_pairs(packed):
    lo = lax.bitcast_convert_type(packed << 16, jnp.float32)
    hi = lax.bitcast_convert_type(packed & jnp.uint32(0xFFFF0000), jnp.float32)
    return lo.astype(jnp.bfloat16), hi.astype(jnp.bfloat16)
V7X_VMEM_LIMIT = 56 * 1024 * 1024

F32 = jnp.float32
BF16 = jnp.bfloat16
NEG_INF = float("-inf")


def _cparams(sem):
    return pltpu.CompilerParams(dimension_semantics=sem, vmem_limit_bytes=V7X_VMEM_LIMIT)


def _inproj_kernel(xp_ref, xt_ref, w_ref, h_ref, kv_ref, *, tiles_per_seq, n_prompt_tiles):
    i = pl.program_id(0)
    x = jnp.where(i < n_prompt_tiles, xp_ref[...], xt_ref[...])
    acc = jnp.dot(x.astype(BF16), w_ref[...], preferred_element_type=F32)
    h_ref[...] = acc.astype(BF16)
    keeps_kv = jnp.logical_or(i % tiles_per_seq == tiles_per_seq - 1, i >= n_prompt_tiles)

    @pl.when(keeps_kv)
    def _():
        kv_ref[:, : 2 * WIDTH_A] = acc[:, COL_KA:COL_QB]
        kv_ref[:, 2 * WIDTH_A:] = acc[:, COL_KB:]


def _inproj(x_prompt, x_tail, w_in, *, seq, n_prompt_tiles):
    d = x_prompt.shape[1]
    t_pad = x_prompt.shape[0] + x_tail.shape[0]
    n_tiles = t_pad // TM
    last_p = n_prompt_tiles - 1
    tiles_per_seq = seq // TM
    n_seq = n_prompt_tiles // tiles_per_seq
    n_kv_blocks = n_seq + (n_tiles - n_prompt_tiles)

    def kv_map(i):
        return (jnp.where(i < n_prompt_tiles, i // tiles_per_seq, n_seq + i - n_prompt_tiles), 0)

    return pl.pallas_call(
        functools.partial(_inproj_kernel, tiles_per_seq=tiles_per_seq, n_prompt_tiles=n_prompt_tiles),
        grid=(n_tiles,),
        in_specs=[pl.BlockSpec((TM, d), lambda i: (jnp.minimum(i, last_p), 0)),
                  pl.BlockSpec((TM, d), lambda i: (jnp.maximum(i - n_prompt_tiles, 0), 0)),
                  pl.BlockSpec((d, D_IN), lambda i: (0, 0))],
        out_specs=[pl.BlockSpec((TM, D_IN), lambda i: (i, 0)),
                   pl.BlockSpec((TM, KV_OUT), kv_map)],
        out_shape=[jax.ShapeDtypeStruct((t_pad, D_IN), BF16),
                   jax.ShapeDtypeStruct((n_kv_blocks * TM, KV_OUT), F32)],
        compiler_params=_cparams(("arbitrary",)),
        name="inproj",
    )(x_prompt, x_tail, w_in)


def _softmax_pv(s_parts, v_parts, sink=None):
    m = s_parts[0].max(axis=-1, keepdims=True)
    for s in s_parts[1:]:
        m = jnp.maximum(m, s.max(axis=-1, keepdims=True))
    if sink is not None:
        m = jnp.maximum(m, sink)
    denom = None
    acc = None
    for s, v in zip(s_parts, v_parts):
        e = jnp.exp(s - m)
        d = e.sum(axis=-1, keepdims=True)
        pv = jnp.dot(e.astype(BF16), v, preferred_element_type=F32)
        denom = d if denom is None else denom + d
        acc = pv if acc is None else acc + pv
    if sink is not None:
        denom = denom + jnp.exp(sink - m)
    return acc / denom


def _nt_dot(a, b):
    return lax.dot_general(a, b, (((1,), (1,)), ((), ())), preferred_element_type=F32)


def _head_pair_lane_mask(rows):
    return lax.broadcasted_iota(jnp.int32, (rows, 2 * HEAD_DIM), 1) < HEAD_DIM


def _rms_store(o_ref, row_slice, col0, pairs, gn_ref):
    ssq = None
    for o in pairs:
        s = (o * o).sum(axis=-1, keepdims=True)
        ssq = s if ssq is None else ssq + s
    width = len(pairs) * 2 * HEAD_DIM
    inv = lax.rsqrt(ssq / width + RMS_EPS)
    for p, o in enumerate(pairs):
        c0 = p * 2 * HEAD_DIM
        g = gn_ref[:, c0:c0 + 2 * HEAD_DIM]
        o_ref[row_slice, col0 + c0:col0 + c0 + 2 * HEAD_DIM] = (o * inv * g).astype(o_ref.dtype)


KEY_BLOCK = V7X_LANES


def _swap_halves(x):
    return jnp.concatenate([x[:, HEAD_DIM:], x[:, :HEAD_DIM]], axis=1)


def _pair_rows(q2, lane_lo, kv_half=None):
    hi_lanes = jnp.logical_not(lane_lo)
    zero = jnp.zeros_like(q2)
    if kv_half is None:
        lo, hi = jnp.where(lane_lo, q2, zero), jnp.where(hi_lanes, q2, zero)
    elif kv_half == 0:
        lo, hi = jnp.where(lane_lo, q2, zero), jnp.where(lane_lo, _swap_halves(q2), zero)
    else:
        lo, hi = jnp.where(hi_lanes, _swap_halves(q2), zero), jnp.where(hi_lanes, q2, zero)
    return jnp.concatenate([lo, hi], axis=0)


def _tn_dot(a, b):
    return lax.dot_general(a, b, (((0,), (0,)), ((), ())), preferred_element_type=F32)


PV_KEYS = 256


def _score_blocks(k_ref, lanes, row0, n_keys, w):
    out = []
    for off in range(0, n_keys, KEY_BLOCK):
        n = min(KEY_BLOCK, n_keys - off)
        start = pl.multiple_of(row0 + off, CHUNK)
        out.append((_nt_dot(k_ref[pl.ds(start, n), lanes], w), start, n, off))
    return out


def _pair_probs(score_blocks, lane_tile, bias_ref, pair, pad, masked, sink_block=None):
    lanes = slice(lane_tile * V7X_LANES, (lane_tile + 1) * V7X_LANES)
    blocks = []
    for s_all, start, n, off in score_blocks:
        s = s_all[:, lanes] + bias_ref[pair, off:off + n, :]
        if masked:
            key_row = lax.broadcasted_iota(jnp.int32, s.shape, 0) + start
            s = jnp.where(key_row >= pad, s, NEG_INF)
        blocks.append((s, start, n))
    m = None
    for s, _, _ in blocks:
        bm = s.max(axis=0, keepdims=True)
        m = bm if m is None else jnp.maximum(m, bm)
    if sink_block is not None:
        m = jnp.maximum(m, sink_block.max(axis=0, keepdims=True))
    probs = [(jnp.exp(s - m).astype(BF16), start, n) for s, start, n in blocks]
    sink_p = None if sink_block is None else jnp.exp(sink_block - m).astype(BF16)
    return probs, sink_p


def _pair_values(probs, sink_p, v_ref, lanes):
    groups, cur, cur_rows = [], [], 0
    for blk in probs:
        if cur and cur_rows + blk[2] > PV_KEYS:
            groups.append(cur)
            cur, cur_rows = [], 0
        cur.append(blk)
        cur_rows += blk[2]
    groups.append(cur)
    acc = None
    for gi, grp in enumerate(groups):
        rows = sum(n for _, _, n in grp)
        p = grp[0][0] if len(grp) == 1 else jnp.concatenate([b[0] for b in grp], axis=0)
        v1 = jnp.concatenate([v_ref[pl.ds(grp[0][1], rows), lanes], jnp.ones((rows, V7X_LANES), BF16)], axis=1)
        if sink_p is not None and gi == len(groups) - 1:
            n = sink_p.shape[0]
            p = jnp.concatenate([p, sink_p], axis=0)
            sink_v = jnp.concatenate([jnp.zeros((n, V7X_LANES), BF16), jnp.ones((n, V7X_LANES), BF16)], axis=1)
            v1 = jnp.concatenate([v1, sink_v], axis=0)
        c = _tn_dot(p, v1)
        acc = c if acc is None else acc + c
    return acc[:, :V7X_LANES] / acc[:, V7X_LANES:]


def _quad_rows(q4):
    head_of_lane = lax.broadcasted_iota(jnp.int32, q4.shape, 1) // HEAD_DIM
    zero = jnp.zeros_like(q4)
    return jnp.concatenate([jnp.where(head_of_lane == h, q4, zero) for h in range(4)], axis=0)


def _attn_prompt_kernel(qa_ref, ka_ref, va_ref, qb_ref, kb_ref, vb_ref,
                        ba_ref, bb_ref, sink_ref, gna_ref, gnb_ref, o_ref,
                        kpa, vpa, kpb, vpb, pa_even, pb_even, pa_odd, pb_odd, *, seq, tq):
    j = pl.program_id(1)

    @pl.when(j == 0)
    def _():
        kpa[0:PAD_A, :] = jnp.zeros((PAD_A, WIDTH_A), BF16)
        vpa[0:PAD_A, :] = jnp.zeros((PAD_A, WIDTH_A), BF16)
        kpb[0:PAD_B, :] = jnp.zeros((PAD_B, KV_WIDTH_B), BF16)
        vpb[0:PAD_B, :] = jnp.zeros((PAD_B, KV_WIDTH_B), BF16)
        kpa[PAD_A:PAD_A + seq, :] = ka_ref[...]
        vpa[PAD_A:PAD_A + seq, :] = va_ref[...]
        kpb[PAD_B:PAD_B + seq, :] = kb_ref[...]
        vpb[PAD_B:PAD_B + seq, :] = vb_ref[...]

    lane_lo = _head_pair_lane_mask(CHUNK)
    scale = HEAD_DIM ** -0.5
    all_lanes = slice(0, KV_WIDTH_B)

    n_pairs_a, n_pairs_b = N_HEADS_A // 2, N_HEADS_B // 2
    band_a, band_b = BAND_A * CHUNK, BAND_B * CHUNK
    n_sink = sink_ref.shape[1]
    even_bufs, odd_bufs = (pa_even, pb_even), (pa_odd, pb_odd)

    def band_start(c):
        return pl.multiple_of(j * tq + c * CHUNK, CHUNK)

    def probs_phase(c, masked, bufs):
        pa_buf, pb_buf = bufs
        rows = pl.ds(pl.multiple_of(c * CHUNK, CHUNK), CHUNK)
        cs = band_start(c)
        scores_a = []
        for quad in range(N_HEADS_A // 4):
            lanes = slice(quad * 4 * HEAD_DIM, (quad + 1) * 4 * HEAD_DIM)
            w = _quad_rows(qa_ref[rows, lanes] * scale)
            scores_a.append(_score_blocks(kpa, lanes, cs, band_a, w))
        w = jnp.concatenate(
            [_pair_rows(qb_ref[rows, p * 2 * HEAD_DIM:(p + 1) * 2 * HEAD_DIM] * scale, lane_lo,
                        kv_half=(2 * p) // (N_HEADS_B // N_KV_B))
             for p in range(n_pairs_b)], axis=0)
        scores_b = _score_blocks(kpb, all_lanes, cs, band_b, w)
        for p in range(n_pairs_a):
            probs, _ = _pair_probs(scores_a[p // 2], p % 2, ba_ref, p, PAD_A, masked)
            for blk, (_, _, n, off) in zip(probs, scores_a[p // 2]):
                pa_buf[p, off:off + n, :] = blk[0]
        for p in range(n_pairs_b):
            probs, sink_p = _pair_probs(scores_b, p, bb_ref, p, PAD_B, masked, sink_block=sink_ref[p])
            for blk, (_, _, n, off) in zip(probs, scores_b):
                pb_buf[p, off:off + n, :] = blk[0]
            pb_buf[p, band_b:band_b + n_sink, :] = sink_p

    def values_phase(c, bufs):
        pa_buf, pb_buf = bufs
        rows = pl.ds(pl.multiple_of(c * CHUNK, CHUNK), CHUNK)
        cs = band_start(c)

        def groups(buf, p, n_keys):
            return [(buf[p, off:off + min(PV_KEYS, n_keys - off), :], pl.multiple_of(cs + off, CHUNK),
                     min(PV_KEYS, n_keys - off)) for off in range(0, n_keys, PV_KEYS)]

        pairs_a = []
        for p in range(n_pairs_a):
            lanes = slice(p * 2 * HEAD_DIM, (p + 1) * 2 * HEAD_DIM)
            o = _pair_values(groups(pa_buf, p, band_a), None, vpa, lanes)
            pairs_a.append(jnp.where(lane_lo, o[:CHUNK], o[CHUNK:]))
        _rms_store(o_ref, rows, 0, pairs_a, gna_ref)

        pairs_b = []
        for p in range(n_pairs_b):
            kv_half = (2 * p) // (N_HEADS_B // N_KV_B)
            o = _pair_values(groups(pb_buf, p, band_b), pb_buf[p, band_b:band_b + n_sink, :], vpb, all_lanes)
            lo, hi = o[:CHUNK], o[CHUNK:]
            if kv_half == 0:
                hi = _swap_halves(hi)
            else:
                lo = _swap_halves(lo)
            pairs_b.append(jnp.where(lane_lo, lo, hi))
        _rms_store(o_ref, rows, WIDTH_A, pairs_b, gnb_ref)

    def run(masked):
        n_chunks = tq // CHUNK
        bufs = (even_bufs, odd_bufs)
        probs_phase(0, masked, even_bufs)
        for c in range(1, n_chunks):
            probs_phase(c, masked, bufs[c % 2])
            values_phase(c - 1, bufs[(c - 1) % 2])
        values_phase(n_chunks - 1, bufs[(n_chunks - 1) % 2])

    @pl.when(j == 0)
    def _():
        run(True)

    @pl.when(j > 0)
    def _():
        run(False)


def _pair_transposed(table):
    n_h, n_q, n_k = table.shape
    return table.reshape(n_h // 2, 2, n_q, n_k).transpose(0, 3, 1, 2).reshape(n_h // 2, n_k, 2 * n_q)


def _sink_blocks(sinks):
    row = jnp.repeat(sinks.astype(F32).reshape(N_HEADS_B // 2, 2), HEAD_DIM, axis=1)
    rest = jnp.full((N_HEADS_B // 2, 15, 2 * HEAD_DIM), NEG_INF, F32)
    return jnp.concatenate([row[:, None, :], rest], axis=1)


def _attn_prompt(h, sinks, bias_a, bias_b, gn_a, gn_b, *, n_seq, seq):
    tq = TM
    assert tq >= PAD_A and tq >= PAD_B and tq % CHUNK == 0
    n_q = seq // tq
    vmem = pltpu.VMEM
    bias_a, bias_b = _pair_transposed(bias_a), _pair_transposed(bias_b)
    sinks = _sink_blocks(sinks)
    return pl.pallas_call(
        functools.partial(_attn_prompt_kernel, seq=seq, tq=tq),
        grid=(n_seq, n_q),
        in_specs=[
            pl.BlockSpec((tq, WIDTH_A), lambda b, j: (b * n_q + j, COL_QA // WIDTH_A)),
            pl.BlockSpec((seq, WIDTH_A), lambda b, j: (b, COL_KA // WIDTH_A)),
            pl.BlockSpec((seq, WIDTH_A), lambda b, j: (b, COL_VA // WIDTH_A)),
            pl.BlockSpec((tq, WIDTH_B), lambda b, j: (b * n_q + j, COL_QB // WIDTH_B)),
            pl.BlockSpec((seq, KV_WIDTH_B), lambda b, j: (b, COL_KB // KV_WIDTH_B)),
            pl.BlockSpec((seq, KV_WIDTH_B), lambda b, j: (b, COL_VB // KV_WIDTH_B)),
            pl.BlockSpec(bias_a.shape, lambda b, j: (0, 0, 0)),
            pl.BlockSpec(bias_b.shape, lambda b, j: (0, 0, 0)),
            pl.BlockSpec(sinks.shape, lambda b, j: (0, 0, 0)),
            pl.BlockSpec(gn_a.shape, lambda b, j: (0, 0)),
            pl.BlockSpec(gn_b.shape, lambda b, j: (0, 0)),
        ],
        out_specs=pl.BlockSpec((tq, WIDTH_A + WIDTH_B), lambda b, j: (b * n_q + j, 0)),
        out_shape=jax.ShapeDtypeStruct((n_seq * seq, WIDTH_A + WIDTH_B), BF16),
        scratch_shapes=[vmem((PAD_A + seq, WIDTH_A), BF16), vmem((PAD_A + seq, WIDTH_A), BF16),
                        vmem((PAD_B + seq, KV_WIDTH_B), BF16), vmem((PAD_B + seq, KV_WIDTH_B), BF16),
                        vmem((N_HEADS_A // 2, BAND_A * CHUNK, V7X_LANES), BF16),
                        vmem((N_HEADS_B // 2, BAND_B * CHUNK + sinks.shape[1], V7X_LANES), BF16),
                        vmem((N_HEADS_A // 2, BAND_A * CHUNK, V7X_LANES), BF16),
                        vmem((N_HEADS_B // 2, BAND_B * CHUNK + sinks.shape[1], V7X_LANES), BF16)],
        compiler_params=_cparams(("arbitrary", "arbitrary")),
        name="attn_prompt",
    )(h, h, h, h, h, h, bias_a, bias_b, sinks, gn_a, gn_b)


def _attn_sample_kernel(sink_ref, qa_ref, ka_ref, va_ref, qb_ref, kb_ref, vb_ref,
                        cka_ref, cva_ref, ckb_ref, cvb_ref,
                        bac_ref, ban_ref, bbc_ref, bbn_ref, gna_ref, gnb_ref, o_ref, *, n_dec, dec_seq):
    b = pl.program_id(0)

    @pl.when(b >= n_dec)
    def _():
        o_ref[...] = jnp.zeros(o_ref.shape, o_ref.dtype)

    @pl.when(b < n_dec)
    def _():
        lane_lo = _head_pair_lane_mask(dec_seq)
        scale = HEAD_DIM ** -0.5
        rows = slice(0, dec_seq)

        pairs_a = []
        for p in range(N_HEADS_A // 2):
            lanes = slice(p * 2 * HEAD_DIM, (p + 1) * 2 * HEAD_DIM)
            q2 = qa_ref[:, lanes] * scale
            kc = cka_ref[0, :, lanes].astype(BF16)
            vc = cva_ref[0, :, lanes].astype(BF16)
            kn = ka_ref[:, lanes]
            vn = va_ref[:, lanes]
            halves = []
            for half in range(2):
                h = 2 * p + half
                keep = lane_lo if half == 0 else jnp.logical_not(lane_lo)
                qm = jnp.where(keep, q2, jnp.zeros_like(q2))
                s_c = _nt_dot(qm, kc) + bac_ref[h]
                s_n = _nt_dot(qm, kn) + ban_ref[h]
                halves.append(_softmax_pv([s_c, s_n], [vc, vn]))
            pairs_a.append(jnp.where(lane_lo, halves[0], halves[1]))
        _rms_store(o_ref, rows, 0, pairs_a, gna_ref)

        kc = ckb_ref[0].astype(BF16)
        vc = cvb_ref[0].astype(BF16)
        kn = kb_ref[...]
        vn = vb_ref[...]
        pairs_b = []
        for p in range(N_HEADS_B // 2):
            lanes = slice(p * 2 * HEAD_DIM, (p + 1) * 2 * HEAD_DIM)
            q2 = qb_ref[:, lanes] * scale
            halves = []
            for half in range(2):
                h = 2 * p + half
                kv_head = h // (N_HEADS_B // N_KV_B)
                keep = lane_lo if half == 0 else jnp.logical_not(lane_lo)
                qm = jnp.where(keep, q2, jnp.zeros_like(q2))
                if kv_head != half:
                    qm = jnp.concatenate([qm[:, HEAD_DIM:], qm[:, :HEAD_DIM]], axis=1)
                s_c = _nt_dot(qm, kc) + bbc_ref[h]
                s_n = _nt_dot(qm, kn) + bbn_ref[h]
                o = _softmax_pv([s_c, s_n], [vc, vn], sink=sink_ref[h])
                if kv_head != half:
                    o = jnp.concatenate([o[:, HEAD_DIM:], o[:, :HEAD_DIM]], axis=1)
                halves.append(o)
            pairs_b.append(jnp.where(lane_lo, halves[0], halves[1]))
        _rms_store(o_ref, rows, WIDTH_A, pairs_b, gnb_ref)


def _attn_sample(h, sinks, cka, cva, ckb, cvb, bias, gn_a, gn_b, *, first_row, n_rows_out, n_dec, dec_seq):
    bac, ban, bbc, bbn = bias
    n_steps = n_rows_out // dec_seq
    rb = first_row // dec_seq
    last = n_dec - 1

    def hrow(b):
        return rb + jnp.minimum(b, last)

    def cache_spec(c):
        return pl.BlockSpec((1,) + c.shape[1:], lambda b: (jnp.minimum(b, last), 0, 0))

    def const_spec(a):
        return pl.BlockSpec(a.shape, lambda b: (0,) * a.ndim)

    return pl.pallas_call(
        functools.partial(_attn_sample_kernel, n_dec=n_dec, dec_seq=dec_seq),
        grid=(n_steps,),
        in_specs=[
            pl.BlockSpec(memory_space=pltpu.SMEM),
            pl.BlockSpec((dec_seq, WIDTH_A), lambda b: (hrow(b), COL_QA // WIDTH_A)),
            pl.BlockSpec((dec_seq, WIDTH_A), lambda b: (hrow(b), COL_KA // WIDTH_A)),
            pl.BlockSpec((dec_seq, WIDTH_A), lambda b: (hrow(b), COL_VA // WIDTH_A)),
            pl.BlockSpec((dec_seq, WIDTH_B), lambda b: (hrow(b), COL_QB // WIDTH_B)),
            pl.BlockSpec((dec_seq, KV_WIDTH_B), lambda b: (hrow(b), COL_KB // KV_WIDTH_B)),
            pl.BlockSpec((dec_seq, KV_WIDTH_B), lambda b: (hrow(b), COL_VB // KV_WIDTH_B)),
            cache_spec(cka), cache_spec(cva), cache_spec(ckb), cache_spec(cvb),
            const_spec(bac), const_spec(ban), const_spec(bbc), const_spec(bbn),
            const_spec(gn_a), const_spec(gn_b),
        ],
        out_specs=pl.BlockSpec((dec_seq, WIDTH_A + WIDTH_B), lambda b: (b, 0)),
        out_shape=jax.ShapeDtypeStruct((n_rows_out, WIDTH_A + WIDTH_B), BF16),
        compiler_params=_cparams(("arbitrary",)),
        name="attn_sample",
    )(sinks, h, h, h, h, h, h, cka, cva, ckb, cvb, bac, ban, bbc, bbn, gn_a, gn_b)


def _layer_norm(r, g, b):
    mu = r.mean(axis=-1, keepdims=True)
    c = r - mu
    var = (c * c).mean(axis=-1, keepdims=True)
    return c * lax.rsqrt(var + LN_EPS) * g + b


def _route_rows(logits):
    rows = [logits[e:e + 1, :] for e in range(N_EXPERTS)]

    def top2_sum(a, b, c, d):
        hi1, lo1 = jnp.maximum(a, b), jnp.minimum(a, b)
        hi2, lo2 = jnp.maximum(c, d), jnp.minimum(c, d)
        return jnp.maximum(hi1, hi2) + jnp.maximum(jnp.minimum(hi1, hi2), jnp.maximum(lo1, lo2))

    scores = [top2_sum(*rows[4 * g:4 * g + 4]) for g in range(N_GROUPS)]
    best = scores[0]
    g_sel = jnp.zeros(best.shape, jnp.int32)
    for g in range(1, N_GROUPS):
        upd = scores[g] > best
        best = jnp.where(upd, scores[g], best)
        g_sel = jnp.where(upd, g, g_sel)

    vals = []
    for k in range(EXPERTS_PER_GROUP):
        v = rows[k]
        for g in range(1, N_GROUPS):
            v = jnp.where(g_sel == g, rows[4 * g + k], v)
        vals.append(v)

    v1 = vals[0]
    i1 = jnp.zeros(v1.shape, jnp.int32)
    for k in range(1, EXPERTS_PER_GROUP):
        upd = vals[k] > v1
        v1 = jnp.where(upd, vals[k], v1)
        i1 = jnp.where(upd, k, i1)
    v2 = jnp.full(v1.shape, NEG_INF, F32)
    i2 = jnp.zeros(v1.shape, jnp.int32)
    for k in range(EXPERTS_PER_GROUP):
        upd = jnp.logical_and(i1 != k, vals[k] > v2)
        v2 = jnp.where(upd, vals[k], v2)
        i2 = jnp.where(upd, k, i2)

    e = jnp.exp(v2 - v1)
    den = 1.0 + e
    w1 = 1.0 / den
    w2 = e / den
    first_is_lo = i1 < i2
    lo = jnp.where(first_is_lo, i1, i2)
    hi = jnp.where(first_is_lo, i2, i1)
    gate_lo = jnp.where(first_is_lo, w1, w2)
    gate_hi = jnp.where(first_is_lo, w2, w1)
    pair_base = jnp.where(lo == 0, 0, jnp.where(lo == 1, 3, 5))
    cls = g_sel * PAIRS_PER_GROUP + pair_base + hi - lo - 1
    return cls, gate_lo, gate_hi


OUTPROJ_SUBTILES = 4


def _outproj_kernel(ap_ref, as_ref, xm_ref, xt_ref, w_ref, g_ref, b_ref, wrh_ref, wrl_ref, br_ref,
                    x1_ref, xpk_ref, cls_ref, *, n_prompt_tiles, n_main_tiles, alpha, d):
    i = pl.program_id(0)
    sub = TM // OUTPROJ_SUBTILES
    w = w_ref[...]
    ys = []
    for s in range(OUTPROJ_SUBTILES):
        rows = slice(s * sub, (s + 1) * sub)
        a = jnp.where(i < n_prompt_tiles, ap_ref[rows, :], as_ref[rows, :])
        ys.append(jnp.dot(a, w, preferred_element_type=F32))
    for s in range(OUTPROJ_SUBTILES):
        rows = slice(s * sub, (s + 1) * sub)
        x = jnp.where(i < n_main_tiles, xm_ref[rows, :], xt_ref[rows, :])
        x1 = _layer_norm(alpha * x + ys[s], g_ref[...], b_ref[...])
        x1_ref[rows, :] = x1
        xpk_ref[rows, :d // 2] = _pack_bf16_pairs(x1[:, :d // 2], x1[:, d // 2:])
        x_hi = x1.astype(BF16)
        x_lo = (x1 - x_hi.astype(F32)).astype(BF16)
        logits = (_nt_dot(wrh_ref[...], x_hi) + _nt_dot(wrh_ref[...], x_lo) + _nt_dot(wrl_ref[...], x_hi)
                  + br_ref[...])
        cls, gate_lo, gate_hi = _route_rows(logits)
        cls_ref[:, rows] = cls
        gates = jnp.concatenate([gate_lo, gate_hi, jnp.zeros((GATE_LANES - 2, sub), F32)], axis=0)
        xpk_ref[rows, d // 2:] = lax.bitcast_convert_type(gates.T, U32)


def _outproj(a_prompt, a_sample, x_main, x_tail, w_out, g, b, wr_hi, wr_lo, br, *, n_tiles, n_prompt_tiles, alpha):
    d = x_main.shape[1]
    t_pad = n_tiles * TM
    n_main_tiles = x_main.shape[0] // TM
    last_p = n_prompt_tiles - 1
    const = lambda i: (0, 0)
    return pl.pallas_call(
        functools.partial(_outproj_kernel, n_prompt_tiles=n_prompt_tiles, n_main_tiles=n_main_tiles,
                          alpha=alpha, d=d),
        grid=(n_tiles,),
        in_specs=[pl.BlockSpec((TM, d), lambda i: (jnp.minimum(i, last_p), 0)),
                  pl.BlockSpec((TM, d), lambda i: (jnp.maximum(i - n_prompt_tiles, 0), 0)),
                  pl.BlockSpec((TM, d), lambda i: (jnp.minimum(i, n_main_tiles - 1), 0)),
                  pl.BlockSpec((TM, d), lambda i: (jnp.maximum(i - n_main_tiles, 0), 0)),
                  pl.BlockSpec(w_out.shape, const),
                  pl.BlockSpec(g.shape, const), pl.BlockSpec(b.shape, const),
                  pl.BlockSpec(wr_hi.shape, const), pl.BlockSpec(wr_lo.shape, const),
                  pl.BlockSpec(br.shape, const)],
        out_specs=[pl.BlockSpec((TM, d), lambda i: (i, 0)),
                   pl.BlockSpec((TM, d // 2 + GATE_LANES), lambda i: (i, 0)),
                   pl.BlockSpec((1, TM), lambda i: (0, i))],
        out_shape=[jax.ShapeDtypeStruct((t_pad, d), F32),
                   jax.ShapeDtypeStruct((t_pad, d // 2 + GATE_LANES), U32),
                   jax.ShapeDtypeStruct((1, t_pad), jnp.int32)],
        compiler_params=_cparams(("arbitrary",)),
        name="outproj_ln_route",
    )(a_prompt, a_sample, x_main, x_tail, w_out, g, b, wr_hi, wr_lo, br)


def _slots_kernel(cls_ref, pos_ref, ea_ref, eb_ref, nu_ref, padlo_ref, padhi_ref, *, rows_pad):
    lane = lax.broadcasted_iota(jnp.int32, (1, V7X_LANES), 1)
    pad_lo = jnp.zeros((1, V7X_LANES), F32)
    pad_hi = jnp.zeros((1, V7X_LANES), F32)
    cls = cls_ref[...]
    rows = cls.shape[0]
    r_i = lax.broadcasted_iota(jnp.int32, (rows_pad, rows_pad), 0)
    c_i = lax.broadcasted_iota(jnp.int32, (rows_pad, rows_pad), 1)
    lower = jnp.where(c_i < r_i, 1.0, 0.0).astype(BF16)
    l_r = lax.broadcasted_iota(jnp.int32, (V7X_LANES, V7X_LANES), 0)
    l_c = lax.broadcasted_iota(jnp.int32, (V7X_LANES, V7X_LANES), 1)
    upper = jnp.where(l_r < l_c, 1.0, 0.0).astype(BF16)

    n_t = ea_ref.shape[1]
    tile_start = lax.broadcasted_iota(jnp.int32, (1, n_t), 1).astype(F32) * float(TM)
    tile_cls = jnp.zeros((1, n_t), jnp.int32)
    off = jnp.zeros((1, 1), F32)
    pos = jnp.zeros((rows, V7X_LANES), F32)
    for c in range(N_CLASSES):
        m = jnp.where(cls == c, 1.0, 0.0)
        mb = m.astype(BF16)
        if rows_pad > rows:
            mb_pad = jnp.concatenate([mb, jnp.zeros((rows_pad - rows, V7X_LANES), BF16)], axis=0)
        else:
            mb_pad = mb
        before_rows = jnp.dot(lower, mb_pad, preferred_element_type=F32)[:rows]
        before_rows = before_rows.sum(axis=-1, keepdims=True)
        before_lanes = jnp.dot(mb, upper, preferred_element_type=F32)
        pos = pos + m * (off + before_rows + before_lanes)
        if c > 0:
            tile_cls = tile_cls + jnp.where(tile_start >= off, 1, 0)
        count = m.sum(axis=-1, keepdims=True).sum(axis=0, keepdims=True)
        pad_lo = jnp.where(lane == c, off + count, pad_lo)
        off = off + jnp.floor((count + float(TM - 1)) * (1.0 / TM)) * float(TM)
        pad_hi = jnp.where(lane == c, off, pad_hi)
    pos_ref[...] = pos.astype(jnp.int32)
    padlo_ref[...] = pad_lo.astype(jnp.int32)
    padhi_ref[...] = pad_hi.astype(jnp.int32)
    grp = (jnp.where(tile_cls >= 6, 1, 0) + jnp.where(tile_cls >= 12, 1, 0)
           + jnp.where(tile_cls >= 18, 1, 0))
    pair = tile_cls - PAIRS_PER_GROUP * grp
    lo = jnp.where(pair >= 3, 1, 0) + jnp.where(pair >= 5, 1, 0)
    hi = pair + 1 - 2 * jnp.where(pair >= 3, 1, 0) - jnp.where(pair >= 5, 1, 0)
    ea_ref[...] = EXPERTS_PER_GROUP * grp + lo
    eb_ref[...] = EXPERTS_PER_GROUP * grp + hi
    nu_ref[...] = jnp.broadcast_to(off * (1.0 / TM), nu_ref.shape).astype(jnp.int32)


def _slots(cls2d, *, n_slot_tiles):
    rows = cls2d.shape[0]
    rows_pad = -(-rows // V7X_LANES) * V7X_LANES
    n_t = -(-n_slot_tiles // V7X_LANES) * V7X_LANES
    i32 = jnp.int32
    return pl.pallas_call(
        functools.partial(_slots_kernel, rows_pad=rows_pad),
        out_shape=[jax.ShapeDtypeStruct(cls2d.shape, i32),
                   jax.ShapeDtypeStruct((1, n_t), i32), jax.ShapeDtypeStruct((1, n_t), i32),
                   jax.ShapeDtypeStruct((1, V7X_LANES), i32), jax.ShapeDtypeStruct((1, V7X_LANES), i32),
                   jax.ShapeDtypeStruct((1, V7X_LANES), i32)],
        compiler_params=pltpu.CompilerParams(vmem_limit_bytes=V7X_VMEM_LIMIT),
        name="moe_slots",
    )(cls2d)


def _row_copy(src, dst, src_row, dst_row, sem):
    return pltpu.make_async_copy(src.at[pl.ds(src_row, 1)], dst.at[pl.ds(dst_row, 1)], sem)


def _start_tile_rows(make_copy):
    for t in range(TM):
        make_copy(t).start()


def _dispatch_kernel(pos_ref, padlo_ref, padhi_ref, nu_ref, x_ref, xs_hbm, zbuf, stage, sem, sems,
                     *, n_slot_tiles, n_tiles):
    @pl.when(pl.program_id(0) == 0)
    def _():
        zbuf[...] = jnp.zeros(zbuf.shape, zbuf.dtype)

        def zero_tile(t, carry):
            cp = pltpu.make_async_copy(zbuf, xs_hbm.at[pl.ds(pl.multiple_of(t * TM, TM), TM)], sem)
            cp.start()
            cp.wait()
            return carry

        lax.fori_loop(nu_ref[0], n_slot_tiles, zero_tile, 0)

        for c in range(N_CLASSES):
            lo, hi = padlo_ref[c], padhi_ref[c]

            def zero_row(s, carry):
                _row_copy(zbuf, xs_hbm, 0, s, sem).start()
                return carry

            def zero_wait(s, carry):
                _row_copy(zbuf, xs_hbm, 0, 0, sem).wait()
                return carry

            lax.fori_loop(lo, hi, zero_row, 0)
            lax.fori_loop(lo, hi, zero_wait, 0)

    i = pl.program_id(0)
    slot = i % 2

    def wait(s):
        pltpu.make_async_copy(stage.at[s], xs_hbm.at[pl.ds(0, TM)], sems.at[s]).wait()

    for s in range(2):
        @pl.when(slot == s)
        def _():
            stage[s] = x_ref[...]
            _start_tile_rows(lambda t: _row_copy(stage.at[s], xs_hbm, t, pos_ref[0, 0, t], sems.at[s]))

    @pl.when(i > 0)
    def _():
        wait(1 - slot)

    @pl.when(i == n_tiles - 1)
    def _():
        wait(slot)


def _dispatch(pos3, pad_lo, pad_hi, n_used, x1g, *, n_slot_tiles):
    n_tiles = pos3.shape[0]
    width = x1g.shape[1]
    smem = pl.BlockSpec(memory_space=pltpu.SMEM)
    return pl.pallas_call(
        functools.partial(_dispatch_kernel, n_slot_tiles=n_slot_tiles, n_tiles=n_tiles),
        grid=(n_tiles,),
        in_specs=[pl.BlockSpec((1, 1, TM), lambda i: (i, 0, 0), memory_space=pltpu.SMEM),
                  smem, smem, smem,
                  pl.BlockSpec((TM, width), lambda i: (i, 0))],
        out_specs=pl.BlockSpec(memory_space=pl.ANY),
        out_shape=jax.ShapeDtypeStruct((n_slot_tiles * TM, width), x1g.dtype),
        scratch_shapes=[pltpu.VMEM((TM, width), x1g.dtype), pltpu.VMEM((2, TM, width), x1g.dtype),
                        pltpu.SemaphoreType.DMA(()), pltpu.SemaphoreType.DMA((2,))],
        compiler_params=pltpu.CompilerParams(dimension_semantics=("arbitrary",), has_side_effects=True,
                                             vmem_limit_bytes=V7X_VMEM_LIMIT),
        name="moe_dispatch",
    )(pos3, pad_lo, pad_hi, n_used, x1g)


def _moe_kernel(ea_ref, eb_ref, nu_ref, xs_ref, wga, wua, wda, wgb, wub, wdb, ys_ref, *, d):
    del ea_ref, eb_ref
    used = pl.program_id(0) < nu_ref[0]

    @pl.when(jnp.logical_not(used))
    def _():
        ys_ref[...] = jnp.zeros(ys_ref.shape, ys_ref.dtype)

    @pl.when(used)
    def _():
        x = jnp.concatenate(_unpack_bf16_pairs(xs_ref[:, :d // 2]), axis=1)
        gates = lax.bitcast_convert_type(xs_ref[:, d // 2:], F32)
        gate_a = gates[:, 0:1]
        gate_b = gates[:, 1:2]

        def expert(wg, wu, wd):
            g = jnp.dot(x, wg[0], preferred_element_type=F32)
            u = jnp.dot(x, wu[0], preferred_element_type=F32)
            h = (g * jax.nn.sigmoid(g)) * u
            return jnp.dot(h.astype(BF16), wd[0], preferred_element_type=F32)

        ys_ref[...] = gate_a * expert(wga, wua, wda) + gate_b * expert(wgb, wub, wdb)


def _moe(tile_ea, tile_eb, n_used, xs, w_gate, w_up, w_down):
    n_slots, width = xs.shape
    d = 2 * (width - GATE_LANES)
    n_tiles = n_slots // TM
    d_e = w_gate.shape[-1]

    def row_map(i, ea, eb, nu):
        return (jnp.minimum(i, nu[0] - 1), 0)

    def w_a(i, ea, eb, nu):
        return (ea[i], 0, 0)

    def w_b(i, ea, eb, nu):
        return (eb[i], 0, 0)

    grid_spec = pltpu.PrefetchScalarGridSpec(
        num_scalar_prefetch=3,
        grid=(n_tiles,),
        in_specs=[pl.BlockSpec((TM, width), row_map),
                  pl.BlockSpec((1, d, d_e), w_a), pl.BlockSpec((1, d, d_e), w_a),
                  pl.BlockSpec((1, d_e, d), w_a),
                  pl.BlockSpec((1, d, d_e), w_b), pl.BlockSpec((1, d, d_e), w_b),
                  pl.BlockSpec((1, d_e, d), w_b)],
        out_specs=pl.BlockSpec((TM, d), lambda i, ea, eb, nu: (i, 0)),
    )
    return pl.pallas_call(
        functools.partial(_moe_kernel, d=d),
        grid_spec=grid_spec,
        out_shape=jax.ShapeDtypeStruct((n_slots, d), F32),
        compiler_params=_cparams(("arbitrary",)),
        name="moe_experts",
    )(tile_ea, tile_eb, n_used, xs, w_gate, w_up, w_down, w_gate, w_up, w_down)


def _gather_ahead(pos_ref, posn_ref, ys_hbm, ybuf, sems, n_tiles):
    i = pl.program_id(0)
    slot = i % 2

    def wait(s):
        pltpu.make_async_copy(ys_hbm.at[pl.ds(0, TM)], ybuf.at[s], sems.at[s]).wait()

    @pl.when(i == 0)
    def _():
        def start(t, carry):
            _row_copy(ys_hbm, ybuf.at[0], pos_ref[0, 0, t], t, sems.at[0]).start()
            return carry
        lax.fori_loop(0, TM, start, 0)

    for s in range(2):
        @pl.when(slot == 1 - s)
        def _():
            _start_tile_rows(lambda t: _row_copy(ys_hbm, ybuf.at[s], posn_ref[0, 0, t], t, sems.at[s]))

    wait(slot)

    def drain():
        @pl.when(i == n_tiles - 1)
        def _():
            wait(1 - slot)

    return slot, drain


def _combine_final_kernel(pos_ref, posn_ref, ys_hbm, x1_ref, g_ref, b_ref, yp_ref, yt_ref, ybuf, sems,
                          *, alpha, d, n_tiles, n_prompt_tiles):
    slot, drain = _gather_ahead(pos_ref, posn_ref, ys_hbm, ybuf, sems, n_tiles)
    x2 = _layer_norm(alpha * x1_ref[:, :d] + ybuf[slot], g_ref[...], b_ref[...])
    is_prompt = pl.program_id(0) < n_prompt_tiles

    @pl.when(is_prompt)
    def _():
        yp_ref[...] = x2

    @pl.when(jnp.logical_not(is_prompt))
    def _():
        yt_ref[...] = x2

    drain()


def _combine_final(pos3, ys, x1g, g, b, *, alpha, n_prompt_tiles):
    t_pad, width = x1g.shape
    d = width
    n_tiles = t_pad // TM
    last_p = n_prompt_tiles - 1
    const = lambda i: (0, 0)
    return pl.pallas_call(
        functools.partial(_combine_final_kernel, alpha=alpha, d=d, n_tiles=n_tiles,
                          n_prompt_tiles=n_prompt_tiles),
        grid=(n_tiles,),
        in_specs=[pl.BlockSpec((1, 1, TM), lambda i: (i, 0, 0), memory_space=pltpu.SMEM),
                  pl.BlockSpec((1, 1, TM), lambda i: (jnp.minimum(i + 1, n_tiles - 1), 0, 0),
                               memory_space=pltpu.SMEM),
                  pl.BlockSpec(memory_space=pl.ANY),
                  pl.BlockSpec((TM, width), lambda i: (i, 0)),
                  pl.BlockSpec(g.shape, const), pl.BlockSpec(b.shape, const)],
        out_specs=[pl.BlockSpec((TM, d), lambda i: (jnp.minimum(i, last_p), 0)),
                   pl.BlockSpec((TM, d), lambda i: (jnp.maximum(i - n_prompt_tiles, 0), 0))],
        out_shape=[jax.ShapeDtypeStruct((n_prompt_tiles * TM, d), F32),
                   jax.ShapeDtypeStruct(((n_tiles - n_prompt_tiles) * TM, d), F32)],
        scratch_shapes=[pltpu.VMEM((2, TM, d), F32), pltpu.SemaphoreType.DMA((2,))],
        compiler_params=_cparams(("arbitrary",)),
        name="moe_combine_ln_final",
    )(pos3, pos3, ys, x1g, g, b)


def _combine_inproj_kernel(pos_ref, posn_ref, ys_hbm, x1_ref, g_ref, b_ref, w_ref,
                           x2_ref, h_ref, kv_ref, ybuf, sems,
                           *, alpha, d, n_tiles):
    slot, drain = _gather_ahead(pos_ref, posn_ref, ys_hbm, ybuf, sems, n_tiles)
    sub = TM // OUTPROJ_SUBTILES
    w = w_ref[...]
    for s in range(OUTPROJ_SUBTILES):
        rows = slice(s * sub, (s + 1) * sub)
        x2 = _layer_norm(alpha * x1_ref[rows, :d] + ybuf[slot, rows, :], g_ref[...], b_ref[...])
        x2_ref[rows, :] = x2
        acc = jnp.dot(x2.astype(BF16), w, preferred_element_type=F32)
        h_ref[rows, :] = acc.astype(BF16)
        kv_ref[rows, : 2 * WIDTH_A] = acc[:, COL_KA:COL_QB]
        kv_ref[rows, 2 * WIDTH_A:] = acc[:, COL_KB:]

    drain()


def _combine_inproj(pos3, ys, x1g, g, b, w_in, *, alpha, seq, n_prompt_tiles):
    t_pad, width = x1g.shape
    d = width
    n_tiles = t_pad // TM
    tiles_per_seq = seq // TM
    n_seq = n_prompt_tiles // tiles_per_seq
    n_kv_blocks = n_seq + (n_tiles - n_prompt_tiles)
    const = lambda i: (0, 0)

    def kv_map(i):
        return (jnp.where(i < n_prompt_tiles, i // tiles_per_seq, n_seq + i - n_prompt_tiles), 0)

    return pl.pallas_call(
        functools.partial(_combine_inproj_kernel, alpha=alpha, d=d, n_tiles=n_tiles),
        grid=(n_tiles,),
        in_specs=[pl.BlockSpec((1, 1, TM), lambda i: (i, 0, 0), memory_space=pltpu.SMEM),
                  pl.BlockSpec((1, 1, TM), lambda i: (jnp.minimum(i + 1, n_tiles - 1), 0, 0),
                               memory_space=pltpu.SMEM),
                  pl.BlockSpec(memory_space=pl.ANY),
                  pl.BlockSpec((TM, width), lambda i: (i, 0)),
                  pl.BlockSpec(g.shape, const), pl.BlockSpec(b.shape, const),
                  pl.BlockSpec(w_in.shape, const)],
        out_specs=[pl.BlockSpec((TM, d), lambda i: (i, 0)),
                   pl.BlockSpec((TM, D_IN), lambda i: (i, 0)),
                   pl.BlockSpec((TM, KV_OUT), kv_map)],
        out_shape=[jax.ShapeDtypeStruct((t_pad, d), F32),
                   jax.ShapeDtypeStruct((t_pad, D_IN), BF16),
                   jax.ShapeDtypeStruct((n_kv_blocks * TM, KV_OUT), F32)],
        scratch_shapes=[pltpu.VMEM((2, TM, d), F32), pltpu.SemaphoreType.DMA((2,))],
        compiler_params=_cparams(("arbitrary",)),
        name="moe_combine_ln_inproj",
    )(pos3, pos3, ys, x1g, g, b, w_in)


def _alibi_slopes():
    return 2.0 ** (-8.0 * jnp.arange(1, N_HEADS_B + 1, dtype=F32) / N_HEADS_B)


def _rel_bias_table(rel_bias, q0, n_q, n_k):
    n_diag = n_q + n_k - 1
    dist = q0 + (n_q - 1) - jnp.arange(n_diag, dtype=jnp.int32)
    diag = jnp.take(rel_bias.astype(F32), jnp.clip(dist, -REL_CLIP, REL_CLIP) + REL_CLIP, axis=1)
    n_h = diag.shape[0]
    ext = jnp.concatenate([diag, jnp.zeros((n_h, 1), F32)], axis=1)
    shifted = jnp.tile(ext, (1, n_q))[:, :n_q * n_diag].reshape(n_h, n_q, n_diag)
    return shifted[:, :, n_q - 1:n_q - 1 + n_k]


def _alibi_table(q0, n_q, n_k):
    dist = q0 + jnp.arange(n_q, dtype=jnp.int32)[:, None] - jnp.arange(n_k, dtype=jnp.int32)[None, :]
    return -_alibi_slopes()[:, None, None] * jnp.abs(dist).astype(F32)[None]


def kernel(x_prompt, x_sample, cache_a_k, cache_a_v, cache_b_k, cache_b_v, w_in, rel_bias, attn_sinks,
           gn_a, gn_b, w_out, ln1_g, ln1_b, w_router, b_router, w_gate, w_up, w_down, ln2_g, ln2_b):
    n_seq, seq, d = x_prompt.shape
    n_dec, dec_seq, _ = x_sample.shape
    depth = w_in.shape[0]
    rows_ca, rows_cb = cache_a_k.shape[2], cache_b_k.shape[2]
    alpha = (2.0 * depth) ** 0.25
    assert seq % TM == 0 and PAD_A == TM and n_dec * dec_seq <= TM and TM % dec_seq == 0
    assert dec_seq % 16 == 0 and seq >= PAD_A

    t_prompt = n_seq * seq
    t_real = t_prompt + n_dec * dec_seq
    t_pad = -(-t_real // (2 * TM)) * (2 * TM)
    n_prompt_tiles = t_prompt // TM
    n_slot_tiles = t_pad // TM + N_CLASSES

    x = x_prompt.reshape(t_prompt, d)
    x_tail = jnp.concatenate([x_sample.reshape(n_dec * dec_seq, d), jnp.zeros((t_pad - t_real, d), F32)], axis=0)

    w_in_b = w_in.astype(BF16)
    w_out_b = w_out.astype(BF16)
    w_gate_b, w_up_b, w_down_b = w_gate.astype(BF16), w_up.astype(BF16), w_down.astype(BF16)
    wr_t = w_router.astype(F32).T
    wr_hi = wr_t.astype(BF16)
    wr_lo = (wr_t - wr_hi.astype(F32)).astype(BF16)
    br = b_router.astype(F32).reshape(N_EXPERTS, 1)

    alibi_p = _alibi_table(PAD_B, CHUNK, BAND_B * CHUNK)
    alibi_s = _alibi_table(rows_cb, dec_seq, rows_cb + dec_seq)

    pa_k, pa_v, pb_k, pb_v, sa_k, sa_v, sb_k, sb_v = ([] for _ in range(8))
    h, kv = _inproj(x, x_tail, w_in_b[0], seq=seq, n_prompt_tiles=n_prompt_tiles)
    for l in range(depth):
        sinks = attn_sinks[l].astype(F32)
        gna = gn_a[l].astype(F32).reshape(1, WIDTH_A)
        gnb = gn_b[l].astype(F32).reshape(1, WIDTH_B)
        bias_pa = _rel_bias_table(rel_bias[l], PAD_A, CHUNK, BAND_A * CHUNK)
        a_prompt = _attn_prompt(h, sinks, bias_pa, alibi_p, gna, gnb, n_seq=n_seq, seq=seq)
        bias_sa = _rel_bias_table(rel_bias[l], rows_ca, dec_seq, rows_ca + dec_seq)
        bias_s = (bias_sa[:, :, :rows_ca], bias_sa[:, :, rows_ca:],
                  alibi_s[:, :, :rows_cb], alibi_s[:, :, rows_cb:])
        a_sample = _attn_sample(
            h, sinks,
            cache_a_k[l].reshape(n_dec, rows_ca, WIDTH_A), cache_a_v[l].reshape(n_dec, rows_ca, WIDTH_A),
            cache_b_k[l].reshape(n_dec, rows_cb, KV_WIDTH_B), cache_b_v[l].reshape(n_dec, rows_cb, KV_WIDTH_B),
            bias_s, gna, gnb, first_row=t_prompt, n_rows_out=t_pad - t_prompt, n_dec=n_dec, dec_seq=dec_seq)

        x1g, xpk, cls = _outproj(a_prompt, a_sample, x, x_tail, w_out_b[l],
                            ln1_g[l].astype(F32).reshape(1, d), ln1_b[l].astype(F32).reshape(1, d),
                            wr_hi, wr_lo, br, n_tiles=t_pad // TM, n_prompt_tiles=n_prompt_tiles, alpha=alpha)
        pos, tile_ea, tile_eb, n_used, pad_lo, pad_hi = _slots(
            cls.reshape(t_pad // V7X_LANES, V7X_LANES), n_slot_tiles=n_slot_tiles)
        pos3 = pos.reshape(t_pad // TM, 1, TM)
        xs = _dispatch(pos3, pad_lo[0], pad_hi[0], n_used[0], xpk, n_slot_tiles=n_slot_tiles)
        ys = _moe(tile_ea[0, :n_slot_tiles], tile_eb[0, :n_slot_tiles], n_used[0, :1],
                  xs, w_gate_b[l], w_up_b[l], w_down_b[l])
        kv_l = kv
        g2, b2 = ln2_g[l].astype(F32).reshape(1, d), ln2_b[l].astype(F32).reshape(1, d)
        if l + 1 < depth:
            x, h, kv = _combine_inproj(pos3, ys, x1g, g2, b2, w_in_b[l + 1], alpha=alpha, seq=seq,
                                       n_prompt_tiles=n_prompt_tiles)
        else:
            y_p, y_t = _combine_final(pos3, ys, x1g, g2, b2, alpha=alpha, n_prompt_tiles=n_prompt_tiles)

        kv_p = kv_l[:n_seq * TM].reshape(n_seq, TM, KV_OUT)
        kv_s = kv_l[n_seq * TM:n_seq * TM + n_dec * dec_seq].reshape(n_dec, dec_seq, KV_OUT)
        ra, rb = min(PAD_A, seq), min(WINDOW_B, seq)
        pa_k.append(kv_p[:, TM - ra:, :WIDTH_A].reshape(n_seq, ra, N_HEADS_A, HEAD_DIM))
        pa_v.append(kv_p[:, TM - ra:, WIDTH_A:2 * WIDTH_A].reshape(n_seq, ra, N_HEADS_A, HEAD_DIM))
        pb_k.append(kv_p[:, TM - rb:, 2 * WIDTH_A:2 * WIDTH_A + KV_WIDTH_B].reshape(n_seq, rb, N_KV_B, HEAD_DIM))
        pb_v.append(kv_p[:, TM - rb:, 2 * WIDTH_A + KV_WIDTH_B:].reshape(n_seq, rb, N_KV_B, HEAD_DIM))
        sa_k.append(kv_s[:, :, :WIDTH_A].reshape(n_dec, dec_seq, N_HEADS_A, HEAD_DIM))
        sa_v.append(kv_s[:, :, WIDTH_A:2 * WIDTH_A].reshape(n_dec, dec_seq, N_HEADS_A, HEAD_DIM))
        sb_k.append(kv_s[:, :, 2 * WIDTH_A:2 * WIDTH_A + KV_WIDTH_B].reshape(n_dec, dec_seq, N_KV_B, HEAD_DIM))
        sb_v.append(kv_s[:, :, 2 * WIDTH_A + KV_WIDTH_B:].reshape(n_dec, dec_seq, N_KV_B, HEAD_DIM))

    y_prompt = y_p.reshape(n_seq, seq, d)
    y_sample = y_t[:n_dec * dec_seq].reshape(n_dec, dec_seq, d)
    return (y_prompt, y_sample,
            jnp.stack(pa_k), jnp.stack(pa_v), jnp.stack(pb_k), jnp.stack(pb_v),
            jnp.stack(sa_k), jnp.stack(sa_v), jnp.stack(sb_k), jnp.stack(sb_v))
```

```python
import functools

import jax
import jax.numpy as jnp
import numpy as np
from jax import lax
from jax.experimental import pallas as pl
from jax.experimental.pallas import tpu as pltpu

CHUNK = 64
HEAD_DIM = 64
N_HEADS_A = 8
BAND_A = 9
REL_CLIP = 128
N_HEADS_B = 8
N_KV_B = 2
WINDOW_B = 128
BAND_B = 1 + WINDOW_B // CHUNK
N_EXPERTS = 16
N_GROUPS = 4
EXPERTS_PER_GROUP = N_EXPERTS // N_GROUPS
PAIRS_PER_GROUP = 6
N_CLASSES = N_GROUPS * PAIRS_PER_GROUP
LN_EPS = 1e-5
RMS_EPS = 1e-6

WIDTH_A = N_HEADS_A * HEAD_DIM
WIDTH_B = N_HEADS_B * HEAD_DIM
KV_WIDTH_B = N_KV_B * HEAD_DIM
D_IN = 3 * WIDTH_A + WIDTH_B + 2 * KV_WIDTH_B
COL_QA, COL_KA, COL_VA = 0, WIDTH_A, 2 * WIDTH_A
COL_QB = 3 * WIDTH_A
COL_KB = COL_QB + WIDTH_B
COL_VB = COL_KB + KV_WIDTH_B
PAD_A = (BAND_A - 1) * CHUNK
PAD_B = WINDOW_B

V7X_LANES = 128
TM = 512
GATE_LANES = V7X_LANES
U32 = jnp.uint32


def _pack_bf16_pairs(lo, hi):
    lo_bits = lax.bitcast_convert_type(lo.astype(jnp.bfloat16).astype(jnp.float32), U32)
    hi_bits = lax.bitcast_convert_type(hi.astype(jnp.bfloat16).astype(jnp.float32), U32)
    return (lo_bits >> 16) | hi_bits


def _unpack_bf16_pairs(packed):
    lo = lax.bitcast_convert_type(packed << 16, jnp.float32)
    hi = lax.bitcast_convert_type(packed & jnp.uint32(0xFFFF0000), jnp.float32)
    return lo.astype(jnp.bfloat16), hi.astype(jnp.bfloat16)
V7X_VMEM_LIMIT = 56 * 1024 * 1024

F32 = jnp.float32
BF16 = jnp.bfloat16
NEG_INF = float("-inf")


def _cparams(sem):
    return pltpu.CompilerParams(dimension_semantics=sem, vmem_limit_bytes=V7X_VMEM_LIMIT)


def _store_kv(kv_refs, rows, acc):
    ka_ref, va_ref, kvb_ref = kv_refs
    ka_ref[rows, :] = acc[:, COL_KA:COL_VA]
    va_ref[rows, :] = acc[:, COL_VA:COL_QB]
    kvb_ref[rows, :] = acc[:, COL_KB:]


def _kv_out(n_kv_blocks, kv_map):
    widths = (WIDTH_A, WIDTH_A, 2 * KV_WIDTH_B)
    return ([pl.BlockSpec((TM, w), kv_map) for w in widths],
            [jax.ShapeDtypeStruct((n_kv_blocks * TM, w), F32) for w in widths])


def _inproj_kernel(xp_ref, xt_ref, w_ref, h_ref, *kv_refs, tiles_per_seq, n_prompt_tiles):
    i = pl.program_id(0)
    x = jnp.where(i < n_prompt_tiles, xp_ref[...], xt_ref[...])
    acc = jnp.dot(x.astype(BF16), w_ref[...], preferred_element_type=F32)
    h_ref[...] = acc.astype(BF16)
    keeps_kv = jnp.logical_or(i % tiles_per_seq == tiles_per_seq - 1, i >= n_prompt_tiles)

    @pl.when(keeps_kv)
    def _():
        _store_kv(kv_refs, slice(None), acc)


def _inproj(x_prompt, x_tail, w_in, *, seq, n_prompt_tiles):
    d = x_prompt.shape[1]
    t_pad = x_prompt.shape[0] + x_tail.shape[0]
    n_tiles = t_pad // TM
    last_p = n_prompt_tiles - 1
    tiles_per_seq = seq // TM
    n_seq = n_prompt_tiles // tiles_per_seq
    n_kv_blocks = n_seq + (n_tiles - n_prompt_tiles)

    def kv_map(i):
        return (jnp.where(i < n_prompt_tiles, i // tiles_per_seq, n_seq + i - n_prompt_tiles), 0)

    kv_specs, kv_shapes = _kv_out(n_kv_blocks, kv_map)
    h, *kv = pl.pallas_call(
        functools.partial(_inproj_kernel, tiles_per_seq=tiles_per_seq, n_prompt_tiles=n_prompt_tiles),
        grid=(n_tiles,),
        in_specs=[pl.BlockSpec((TM, d), lambda i: (jnp.minimum(i, last_p), 0)),
                  pl.BlockSpec((TM, d), lambda i: (jnp.maximum(i - n_prompt_tiles, 0), 0)),
                  pl.BlockSpec((d, D_IN), lambda i: (0, 0))],
        out_specs=[pl.BlockSpec((TM, D_IN), lambda i: (i, 0))] + kv_specs,
        out_shape=[jax.ShapeDtypeStruct((t_pad, D_IN), BF16)] + kv_shapes,
        compiler_params=_cparams(("arbitrary",)),
        name="inproj",
    )(x_prompt, x_tail, w_in)
    return h, kv


def _softmax_pv(s_parts, v_parts, sink=None):
    m = s_parts[0].max(axis=-1, keepdims=True)
    for s in s_parts[1:]:
        m = jnp.maximum(m, s.max(axis=-1, keepdims=True))
    if sink is not None:
        m = jnp.maximum(m, sink)
    denom = None
    acc = None
    for s, v in zip(s_parts, v_parts):
        e = jnp.exp(s - m)
        d = e.sum(axis=-1, keepdims=True)
        pv = jnp.dot(e.astype(BF16), v, preferred_element_type=F32)
        denom = d if denom is None else denom + d
        acc = pv if acc is None else acc + pv
    if sink is not None:
        denom = denom + jnp.exp(sink - m)
    return acc / denom


def _nt_dot(a, b):
    return lax.dot_general(a, b, (((1,), (1,)), ((), ())), preferred_element_type=F32)


def _head_pair_lane_mask(rows):
    return lax.broadcasted_iota(jnp.int32, (rows, 2 * HEAD_DIM), 1) < HEAD_DIM


def _rms_store(o_ref, row_slice, col0, pairs, gn_ref):
    ssq = None
    for o in pairs:
        s = (o * o).sum(axis=-1, keepdims=True)
        ssq = s if ssq is None else ssq + s
    width = len(pairs) * 2 * HEAD_DIM
    inv = lax.rsqrt(ssq / width + RMS_EPS)
    for p, o in enumerate(pairs):
        c0 = p * 2 * HEAD_DIM
        g = gn_ref[:, c0:c0 + 2 * HEAD_DIM]
        o_ref[row_slice, col0 + c0:col0 + c0 + 2 * HEAD_DIM] = (o * inv * g).astype(o_ref.dtype)


KEY_BLOCK = V7X_LANES


def _swap_halves(x):
    return jnp.concatenate([x[:, HEAD_DIM:], x[:, :HEAD_DIM]], axis=1)


def _pair_rows(q2, lane_lo, kv_half=None):
    hi_lanes = jnp.logical_not(lane_lo)
    zero = jnp.zeros_like(q2)
    if kv_half is None:
        lo, hi = jnp.where(lane_lo, q2, zero), jnp.where(hi_lanes, q2, zero)
    elif kv_half == 0:
        lo, hi = jnp.where(lane_lo, q2, zero), jnp.where(lane_lo, _swap_halves(q2), zero)
    else:
        lo, hi = jnp.where(hi_lanes, _swap_halves(q2), zero), jnp.where(hi_lanes, q2, zero)
    return jnp.concatenate([lo, hi], axis=0)


def _tn_dot(a, b):
    return lax.dot_general(a, b, (((0,), (0,)), ((), ())), preferred_element_type=F32)


PV_KEYS = 256


def _score_blocks(k_ref, lanes, row0, n_keys, w):
    out = []
    for off in range(0, n_keys, KEY_BLOCK):
        n = min(KEY_BLOCK, n_keys - off)
        start = pl.multiple_of(row0 + off, CHUNK)
        out.append((_nt_dot(k_ref[pl.ds(start, n), lanes], w), start, n, off))
    return out


def _pair_probs(score_blocks, lane_tile, bias_ref, pair, pad, masked, sink_block=None):
    lanes = slice(lane_tile * V7X_LANES, (lane_tile + 1) * V7X_LANES)
    blocks = []
    for s_all, start, n, off in score_blocks:
        s = s_all[:, lanes] + bias_ref[pair, off:off + n, :]
        if masked:
            key_row = lax.broadcasted_iota(jnp.int32, s.shape, 0) + start
            s = jnp.where(key_row >= pad, s, NEG_INF)
        blocks.append((s, start, n))
    m = None
    for s, _, _ in blocks:
        bm = s.max(axis=0, keepdims=True)
        m = bm if m is None else jnp.maximum(m, bm)
    if sink_block is not None:
        m = jnp.maximum(m, sink_block.max(axis=0, keepdims=True))
    probs = [(jnp.exp(s - m).astype(BF16), start, n) for s, start, n in blocks]
    sink_p = None if sink_block is None else jnp.exp(sink_block - m).astype(BF16)
    return probs, sink_p


def _pair_values(probs, sink_p, v_ref, lanes):
    groups, cur, cur_rows = [], [], 0
    for blk in probs:
        if cur and cur_rows + blk[2] > PV_KEYS:
            groups.append(cur)
            cur, cur_rows = [], 0
        cur.append(blk)
        cur_rows += blk[2]
    groups.append(cur)
    acc = None
    for gi, grp in enumerate(groups):
        rows = sum(n for _, _, n in grp)
        p = grp[0][0] if len(grp) == 1 else jnp.concatenate([b[0] for b in grp], axis=0)
        v1 = jnp.concatenate([v_ref[pl.ds(grp[0][1], rows), lanes], jnp.ones((rows, V7X_LANES), BF16)], axis=1)
        if sink_p is not None and gi == len(groups) - 1:
            n = sink_p.shape[0]
            p = jnp.concatenate([p, sink_p], axis=0)
            sink_v = jnp.concatenate([jnp.zeros((n, V7X_LANES), BF16), jnp.ones((n, V7X_LANES), BF16)], axis=1)
            v1 = jnp.concatenate([v1, sink_v], axis=0)
        c = _tn_dot(p, v1)
        acc = c if acc is None else acc + c
    return acc[:, :V7X_LANES] / acc[:, V7X_LANES:]


def _quad_rows(q4):
    head_of_lane = lax.broadcasted_iota(jnp.int32, q4.shape, 1) // HEAD_DIM
    zero = jnp.zeros_like(q4)
    return jnp.concatenate([jnp.where(head_of_lane == h, q4, zero) for h in range(4)], axis=0)


def _attn_prompt_kernel(qa_ref, ka_ref, va_ref, qb_ref, kb_ref, vb_ref,
                        ba_ref, bb_ref, sink_ref, gna_ref, gnb_ref, o_ref,
                        kpa, vpa, kpb, vpb, pa_even, pb_even, pa_odd, pb_odd, *, seq, tq):
    j = pl.program_id(1)

    @pl.when(j == 0)
    def _():
        kpa[0:PAD_A, :] = jnp.zeros((PAD_A, WIDTH_A), BF16)
        vpa[0:PAD_A, :] = jnp.zeros((PAD_A, WIDTH_A), BF16)
        vpa[PAD_A + seq:, :] = jnp.zeros((CHUNK, WIDTH_A), BF16)
        kpb[0:PAD_B, :] = jnp.zeros((PAD_B, KV_WIDTH_B), BF16)
        vpb[0:PAD_B, :] = jnp.zeros((PAD_B, KV_WIDTH_B), BF16)
        kpa[PAD_A:PAD_A + seq, :] = ka_ref[...]
        vpa[PAD_A:PAD_A + seq, :] = va_ref[...]
        kpb[PAD_B:PAD_B + seq, :] = kb_ref[...]
        vpb[PAD_B:PAD_B + seq, :] = vb_ref[...]

    lane_lo = _head_pair_lane_mask(CHUNK)
    scale = HEAD_DIM ** -0.5
    all_lanes = slice(0, KV_WIDTH_B)

    n_pairs_a, n_pairs_b = N_HEADS_A // 2, N_HEADS_B // 2
    band_a, band_b = BAND_A * CHUNK, BAND_B * CHUNK
    band_a_pad = -(-band_a // V7X_LANES) * V7X_LANES
    n_sink = sink_ref.shape[1]
    even_bufs, odd_bufs = (pa_even, pb_even), (pa_odd, pb_odd)

    def band_start(c):
        return pl.multiple_of(j * tq + c * CHUNK, CHUNK)

    def probs_phase(c, masked, bufs):
        pa_buf, pb_buf = bufs
        rows = pl.ds(pl.multiple_of(c * CHUNK, CHUNK), CHUNK)
        cs = band_start(c)
        scores_a = []
        for quad in range(N_HEADS_A // 4):
            lanes = slice(quad * 4 * HEAD_DIM, (quad + 1) * 4 * HEAD_DIM)
            w = _quad_rows(qa_ref[rows, lanes] * scale)
            scores_a.append(_score_blocks(kpa, lanes, cs, band_a, w))
        w = jnp.concatenate(
            [_pair_rows(qb_ref[rows, p * 2 * HEAD_DIM:(p + 1) * 2 * HEAD_DIM] * scale, lane_lo,
                        kv_half=(2 * p) // (N_HEADS_B // N_KV_B))
             for p in range(n_pairs_b)], axis=0)
        scores_b = _score_blocks(kpb, all_lanes, cs, band_b, w)
        for p in range(n_pairs_a):
            probs, _ = _pair_probs(scores_a[p // 2], p % 2, ba_ref, p, PAD_A, masked)
            for blk, (_, _, n, off) in zip(probs, scores_a[p // 2]):
                pa_buf[p, :, off:off + n] = blk[0].T
            pa_buf[p, :, band_a:] = jnp.zeros((V7X_LANES, band_a_pad - band_a), BF16)
        for p in range(n_pairs_b):
            probs, sink_p = _pair_probs(scores_b, p, bb_ref, p, PAD_B, masked, sink_block=sink_ref[p])
            for blk, (_, _, n, off) in zip(probs, scores_b):
                pb_buf[p, off:off + n, :] = blk[0]
            pb_buf[p, band_b:band_b + n_sink, :] = sink_p

    def values_phase(c, bufs):
        pa_buf, pb_buf = bufs
        rows = pl.ds(pl.multiple_of(c * CHUNK, CHUNK), CHUNK)
        cs = band_start(c)

        def groups(buf, p, n_keys):
            return [(buf[p, off:off + min(PV_KEYS, n_keys - off), :], pl.multiple_of(cs + off, CHUNK),
                     min(PV_KEYS, n_keys - off)) for off in range(0, n_keys, PV_KEYS)]

        pairs_a = []
        for p in range(n_pairs_a):
            lanes = slice(p * 2 * HEAD_DIM, (p + 1) * 2 * HEAD_DIM)
            v1 = jnp.concatenate([vpa[pl.ds(cs, band_a_pad), lanes], jnp.ones((band_a_pad, V7X_LANES), BF16)], axis=1)
            acc = jnp.dot(pa_buf[p], v1, preferred_element_type=F32)
            o = acc[:, :V7X_LANES] / acc[:, V7X_LANES:]
            pairs_a.append(jnp.where(lane_lo, o[:CHUNK], o[CHUNK:]))
        _rms_store(o_ref, rows, 0, pairs_a, gna_ref)

        pairs_b = []
        for p in range(n_pairs_b):
            kv_half = (2 * p) // (N_HEADS_B // N_KV_B)
            o = _pair_values(groups(pb_buf, p, band_b), pb_buf[p, band_b:band_b + n_sink, :], vpb, all_lanes)
            lo, hi = o[:CHUNK], o[CHUNK:]
            if kv_half == 0:
                hi = _swap_halves(hi)
            else:
                lo = _swap_halves(lo)
            pairs_b.append(jnp.where(lane_lo, lo, hi))
        _rms_store(o_ref, rows, WIDTH_A, pairs_b, gnb_ref)

    def run(masked):
        n_chunks = tq // CHUNK
        bufs = (even_bufs, odd_bufs)
        probs_phase(0, masked, even_bufs)
        for c in range(1, n_chunks):
            probs_phase(c, masked, bufs[c % 2])
            values_phase(c - 1, bufs[(c - 1) % 2])
        values_phase(n_chunks - 1, bufs[(n_chunks - 1) % 2])

    @pl.when(j == 0)
    def _():
        run(True)

    @pl.when(j > 0)
    def _():
        run(False)


def _pair_transposed(table):
    n_h, n_q, n_k = table.shape
    return table.reshape(n_h // 2, 2, n_q, n_k).transpose(0, 3, 1, 2).reshape(n_h // 2, n_k, 2 * n_q)


def _sink_blocks(sinks):
    row = jnp.repeat(sinks.astype(F32).reshape(N_HEADS_B // 2, 2), HEAD_DIM, axis=1)
    rest = jnp.full((N_HEADS_B // 2, 15, 2 * HEAD_DIM), NEG_INF, F32)
    return jnp.concatenate([row[:, None, :], rest], axis=1)


def _attn_prompt(h, sinks, bias_a, bias_b, gn_a, gn_b, *, n_seq, seq):
    tq = TM
    assert tq >= PAD_A and tq >= PAD_B and tq % CHUNK == 0
    n_q = seq // tq
    vmem = pltpu.VMEM
    bias_a, bias_b = _pair_transposed(bias_a), _pair_transposed(bias_b)
    sinks = _sink_blocks(sinks)
    return pl.pallas_call(
        functools.partial(_attn_prompt_kernel, seq=seq, tq=tq),
        grid=(n_seq, n_q),
        in_specs=[
            pl.BlockSpec((tq, WIDTH_A), lambda b, j: (b * n_q + j, COL_QA // WIDTH_A)),
            pl.BlockSpec((seq, WIDTH_A), lambda b, j: (b, COL_KA // WIDTH_A)),
            pl.BlockSpec((seq, WIDTH_A), lambda b, j: (b, COL_VA // WIDTH_A)),
            pl.BlockSpec((tq, WIDTH_B), lambda b, j: (b * n_q + j, COL_QB // WIDTH_B)),
            pl.BlockSpec((seq, KV_WIDTH_B), lambda b, j: (b, COL_KB // KV_WIDTH_B)),
            pl.BlockSpec((seq, KV_WIDTH_B), lambda b, j: (b, COL_VB // KV_WIDTH_B)),
            pl.BlockSpec(bias_a.shape, lambda b, j: (0, 0, 0)),
            pl.BlockSpec(bias_b.shape, lambda b, j: (0, 0, 0)),
            pl.BlockSpec(sinks.shape, lambda b, j: (0, 0, 0)),
            pl.BlockSpec(gn_a.shape, lambda b, j: (0, 0)),
            pl.BlockSpec(gn_b.shape, lambda b, j: (0, 0)),
        ],
        out_specs=pl.BlockSpec((tq, WIDTH_A + WIDTH_B), lambda b, j: (b * n_q + j, 0)),
        out_shape=jax.ShapeDtypeStruct((n_seq * seq, WIDTH_A + WIDTH_B), BF16),
        scratch_shapes=[vmem((PAD_A + seq, WIDTH_A), BF16), vmem((PAD_A + seq + CHUNK, WIDTH_A), BF16),
                        vmem((PAD_B + seq, KV_WIDTH_B), BF16), vmem((PAD_B + seq, KV_WIDTH_B), BF16),
                        vmem((N_HEADS_A // 2, V7X_LANES, -(-BAND_A * CHUNK // V7X_LANES) * V7X_LANES), BF16),
                        vmem((N_HEADS_B // 2, BAND_B * CHUNK + sinks.shape[1], V7X_LANES), BF16),
                        vmem((N_HEADS_A // 2, V7X_LANES, -(-BAND_A * CHUNK // V7X_LANES) * V7X_LANES), BF16),
                        vmem((N_HEADS_B // 2, BAND_B * CHUNK + sinks.shape[1], V7X_LANES), BF16)],
        compiler_params=_cparams(("arbitrary", "arbitrary")),
        name="attn_prompt",
    )(h, h, h, h, h, h, bias_a, bias_b, sinks, gn_a, gn_b)


def _attn_sample_kernel(sink_ref, qa_ref, ka_ref, va_ref, qb_ref, kb_ref, vb_ref,
                        cka_ref, cva_ref, ckb_ref, cvb_ref,
                        bac_ref, ban_ref, bbc_ref, bbn_ref, gna_ref, gnb_ref, o_ref, *, n_dec, dec_seq):
    b = pl.program_id(0)

    @pl.when(b >= n_dec)
    def _():
        o_ref[...] = jnp.zeros(o_ref.shape, o_ref.dtype)

    @pl.when(b < n_dec)
    def _():
        lane_lo = _head_pair_lane_mask(dec_seq)
        scale = HEAD_DIM ** -0.5
        rows = slice(0, dec_seq)

        pairs_a = []
        for p in range(N_HEADS_A // 2):
            lanes = slice(p * 2 * HEAD_DIM, (p + 1) * 2 * HEAD_DIM)
            q2 = qa_ref[:, lanes] * scale
            kc = cka_ref[0, :, lanes].astype(BF16)
            vc = cva_ref[0, :, lanes].astype(BF16)
            kn = ka_ref[:, lanes]
            vn = va_ref[:, lanes]
            halves = []
            for half in range(2):
                h = 2 * p + half
                keep = lane_lo if half == 0 else jnp.logical_not(lane_lo)
                qm = jnp.where(keep, q2, jnp.zeros_like(q2))
                s_c = _nt_dot(qm, kc) + bac_ref[h]
                s_n = _nt_dot(qm, kn) + ban_ref[h]
                halves.append(_softmax_pv([s_c, s_n], [vc, vn]))
            pairs_a.append(jnp.where(lane_lo, halves[0], halves[1]))
        _rms_store(o_ref, rows, 0, pairs_a, gna_ref)

        kc = ckb_ref[0].astype(BF16)
        vc = cvb_ref[0].astype(BF16)
        kn = kb_ref[...]
        vn = vb_ref[...]
        pairs_b = []
        for p in range(N_HEADS_B // 2):
            lanes = slice(p * 2 * HEAD_DIM, (p + 1) * 2 * HEAD_DIM)
            q2 = qb_ref[:, lanes] * scale
            halves = []
            for half in range(2):
                h = 2 * p + half
                kv_head = h // (N_HEADS_B // N_KV_B)
                keep = lane_lo if half == 0 else jnp.logical_not(lane_lo)
                qm = jnp.where(keep, q2, jnp.zeros_like(q2))
                if kv_head != half:
                    qm = jnp.concatenate([qm[:, HEAD_DIM:], qm[:, :HEAD_DIM]], axis=1)
                s_c = _nt_dot(qm, kc) + bbc_ref[h]
                s_n = _nt_dot(qm, kn) + bbn_ref[h]
                o = _softmax_pv([s_c, s_n], [vc, vn], sink=sink_ref[h])
                if kv_head != half:
                    o = jnp.concatenate([o[:, HEAD_DIM:], o[:, :HEAD_DIM]], axis=1)
                halves.append(o)
            pairs_b.append(jnp.where(lane_lo, halves[0], halves[1]))
        _rms_store(o_ref, rows, WIDTH_A, pairs_b, gnb_ref)


def _attn_sample(h, sinks, cka, cva, ckb, cvb, bias, gn_a, gn_b, *, first_row, n_rows_out, n_dec, dec_seq):
    bac, ban, bbc, bbn = bias
    n_steps = n_rows_out // dec_seq
    rb = first_row // dec_seq
    last = n_dec - 1

    def hrow(b):
        return rb + jnp.minimum(b, last)

    def cache_spec(c):
        return pl.BlockSpec((1,) + c.shape[1:], lambda b: (jnp.minimum(b, last), 0, 0))

    def const_spec(a):
        return pl.BlockSpec(a.shape, lambda b: (0,) * a.ndim)

    return pl.pallas_call(
        functools.partial(_attn_sample_kernel, n_dec=n_dec, dec_seq=dec_seq),
        grid=(n_steps,),
        in_specs=[
            pl.BlockSpec(memory_space=pltpu.SMEM),
            pl.BlockSpec((dec_seq, WIDTH_A), lambda b: (hrow(b), COL_QA // WIDTH_A)),
            pl.BlockSpec((dec_seq, WIDTH_A), lambda b: (hrow(b), COL_KA // WIDTH_A)),
            pl.BlockSpec((dec_seq, WIDTH_A), lambda b: (hrow(b), COL_VA // WIDTH_A)),
            pl.BlockSpec((dec_seq, WIDTH_B), lambda b: (hrow(b), COL_QB // WIDTH_B)),
            pl.BlockSpec((dec_seq, KV_WIDTH_B), lambda b: (hrow(b), COL_KB // KV_WIDTH_B)),
            pl.BlockSpec((dec_seq, KV_WIDTH_B), lambda b: (hrow(b), COL_VB // KV_WIDTH_B)),
            cache_spec(cka), cache_spec(cva), cache_spec(ckb), cache_spec(cvb),
            const_spec(bac), const_spec(ban), const_spec(bbc), const_spec(bbn),
            const_spec(gn_a), const_spec(gn_b),
        ],
        out_specs=pl.BlockSpec((dec_seq, WIDTH_A + WIDTH_B), lambda b: (b, 0)),
        out_shape=jax.ShapeDtypeStruct((n_rows_out, WIDTH_A + WIDTH_B), BF16),
        compiler_params=_cparams(("arbitrary",)),
        name="attn_sample",
    )(sinks, h, h, h, h, h, h, cka, cva, ckb, cvb, bac, ban, bbc, bbn, gn_a, gn_b)


def _layer_norm(r, g, b):
    mu = r.mean(axis=-1, keepdims=True)
    c = r - mu
    var = (c * c).mean(axis=-1, keepdims=True)
    return c * lax.rsqrt(var + LN_EPS) * g + b


def _route_rows(logits):
    rows = [logits[e:e + 1, :] for e in range(N_EXPERTS)]

    def top2_sum(a, b, c, d):
        hi1, lo1 = jnp.maximum(a, b), jnp.minimum(a, b)
        hi2, lo2 = jnp.maximum(c, d), jnp.minimum(c, d)
        return jnp.maximum(hi1, hi2) + jnp.maximum(jnp.minimum(hi1, hi2), jnp.maximum(lo1, lo2))

    scores = [top2_sum(*rows[4 * g:4 * g + 4]) for g in range(N_GROUPS)]
    best = scores[0]
    g_sel = jnp.zeros(best.shape, jnp.int32)
    for g in range(1, N_GROUPS):
        upd = scores[g] > best
        best = jnp.where(upd, scores[g], best)
        g_sel = jnp.where(upd, g, g_sel)

    vals = []
    for k in range(EXPERTS_PER_GROUP):
        v = rows[k]
        for g in range(1, N_GROUPS):
            v = jnp.where(g_sel == g, rows[4 * g + k], v)
        vals.append(v)

    v1 = vals[0]
    i1 = jnp.zeros(v1.shape, jnp.int32)
    for k in range(1, EXPERTS_PER_GROUP):
        upd = vals[k] > v1
        v1 = jnp.where(upd, vals[k], v1)
        i1 = jnp.where(upd, k, i1)
    v2 = jnp.full(v1.shape, NEG_INF, F32)
    i2 = jnp.zeros(v1.shape, jnp.int32)
    for k in range(EXPERTS_PER_GROUP):
        upd = jnp.logical_and(i1 != k, vals[k] > v2)
        v2 = jnp.where(upd, vals[k], v2)
        i2 = jnp.where(upd, k, i2)

    e = jnp.exp(v2 - v1)
    den = 1.0 + e
    w1 = 1.0 / den
    w2 = e / den
    first_is_lo = i1 < i2
    lo = jnp.where(first_is_lo, i1, i2)
    hi = jnp.where(first_is_lo, i2, i1)
    gate_lo = jnp.where(first_is_lo, w1, w2)
    gate_hi = jnp.where(first_is_lo, w2, w1)
    pair_base = jnp.where(lo == 0, 0, jnp.where(lo == 1, 3, 5))
    cls = g_sel * PAIRS_PER_GROUP + pair_base + hi - lo - 1
    return cls, gate_lo, gate_hi


OUTPROJ_SUBTILES = 4


def _outproj_kernel(ap_ref, as_ref, xm_ref, xt_ref, w_ref, g_ref, b_ref, wrh_ref, wrl_ref, br_ref,
                    x1_ref, xpk_ref, cls_ref, *, n_prompt_tiles, n_main_tiles, alpha, d):
    i = pl.program_id(0)
    sub = TM // OUTPROJ_SUBTILES
    w = w_ref[...]
    ys = []
    for s in range(OUTPROJ_SUBTILES):
        rows = slice(s * sub, (s + 1) * sub)
        a = jnp.where(i < n_prompt_tiles, ap_ref[rows, :], as_ref[rows, :])
        ys.append(jnp.dot(a, w, preferred_element_type=F32))
    for s in range(OUTPROJ_SUBTILES):
        rows = slice(s * sub, (s + 1) * sub)
        x = jnp.where(i < n_main_tiles, xm_ref[rows, :], xt_ref[rows, :])
        x1 = _layer_norm(alpha * x + ys[s], g_ref[...], b_ref[...])
        x1_ref[rows, :] = x1
        xpk_ref[rows, :d // 2] = _pack_bf16_pairs(x1[:, :d // 2], x1[:, d // 2:])
        x_hi = x1.astype(BF16)
        x_lo = (x1 - x_hi.astype(F32)).astype(BF16)
        logits = (_nt_dot(wrh_ref[...], x_hi) + _nt_dot(wrh_ref[...], x_lo) + _nt_dot(wrl_ref[...], x_hi)
                  + br_ref[...])
        cls, gate_lo, gate_hi = _route_rows(logits)
        cls_ref[:, rows] = cls
        gates = jnp.concatenate([gate_lo, gate_hi, jnp.zeros((GATE_LANES - 2, sub), F32)], axis=0)
        xpk_ref[rows, d // 2:] = lax.bitcast_convert_type(gates.T, U32)


def _outproj(a_prompt, a_sample, x_main, x_tail, w_out, g, b, wr_hi, wr_lo, br, *, n_tiles, n_prompt_tiles, alpha):
    d = x_main.shape[1]
    t_pad = n_tiles * TM
    n_main_tiles = x_main.shape[0] // TM
    last_p = n_prompt_tiles - 1
    const = lambda i: (0, 0)
    return pl.pallas_call(
        functools.partial(_outproj_kernel, n_prompt_tiles=n_prompt_tiles, n_main_tiles=n_main_tiles,
                          alpha=alpha, d=d),
        grid=(n_tiles,),
        in_specs=[pl.BlockSpec((TM, d), lambda i: (jnp.minimum(i, last_p), 0)),
                  pl.BlockSpec((TM, d), lambda i: (jnp.maximum(i - n_prompt_tiles, 0), 0)),
                  pl.BlockSpec((TM, d), lambda i: (jnp.minimum(i, n_main_tiles - 1), 0)),
                  pl.BlockSpec((TM, d), lambda i: (jnp.maximum(i - n_main_tiles, 0), 0)),
                  pl.BlockSpec(w_out.shape, const),
                  pl.BlockSpec(g.shape, const), pl.BlockSpec(b.shape, const),
                  pl.BlockSpec(wr_hi.shape, const), pl.BlockSpec(wr_lo.shape, const),
                  pl.BlockSpec(br.shape, const)],
        out_specs=[pl.BlockSpec((TM, d), lambda i: (i, 0)),
                   pl.BlockSpec((TM, d // 2 + GATE_LANES), lambda i: (i, 0)),
                   pl.BlockSpec((1, TM), lambda i: (0, i))],
        out_shape=[jax.ShapeDtypeStruct((t_pad, d), F32),
                   jax.ShapeDtypeStruct((t_pad, d // 2 + GATE_LANES), U32),
                   jax.ShapeDtypeStruct((1, t_pad), jnp.int32)],
        compiler_params=_cparams(("arbitrary",)),
        name="outproj_ln_route",
    )(a_prompt, a_sample, x_main, x_tail, w_out, g, b, wr_hi, wr_lo, br)


def _slots_kernel(cls_ref, pos_ref, ea_ref, eb_ref, nu_ref, padlo_ref, padhi_ref, *, rows_pad):
    lane = lax.broadcasted_iota(jnp.int32, (1, V7X_LANES), 1)
    pad_lo = jnp.zeros((1, V7X_LANES), F32)
    pad_hi = jnp.zeros((1, V7X_LANES), F32)
    cls = cls_ref[...]
    rows = cls.shape[0]
    r_i = lax.broadcasted_iota(jnp.int32, (rows_pad, rows_pad), 0)
    c_i = lax.broadcasted_iota(jnp.int32, (rows_pad, rows_pad), 1)
    lower = jnp.where(c_i < r_i, 1.0, 0.0).astype(BF16)
    l_r = lax.broadcasted_iota(jnp.int32, (V7X_LANES, V7X_LANES), 0)
    l_c = lax.broadcasted_iota(jnp.int32, (V7X_LANES, V7X_LANES), 1)
    upper = jnp.where(l_r < l_c, 1.0, 0.0).astype(BF16)

    n_t = ea_ref.shape[1]
    tile_start = lax.broadcasted_iota(jnp.int32, (1, n_t), 1).astype(F32) * float(TM)
    tile_cls = jnp.zeros((1, n_t), jnp.int32)
    off = jnp.zeros((1, 1), F32)
    pos = jnp.zeros((rows, V7X_LANES), F32)
    for c in range(N_CLASSES):
        m = jnp.where(cls == c, 1.0, 0.0)
        mb = m.astype(BF16)
        if rows_pad > rows:
            mb_pad = jnp.concatenate([mb, jnp.zeros((rows_pad - rows, V7X_LANES), BF16)], axis=0)
        else:
            mb_pad = mb
        before_rows = jnp.dot(lower, mb_pad, preferred_element_type=F32)[:rows]
        before_rows = before_rows.sum(axis=-1, keepdims=True)
        before_lanes = jnp.dot(mb, upper, preferred_element_type=F32)
        pos = pos + m * (off + before_rows + before_lanes)
        if c > 0:
            tile_cls = tile_cls + jnp.where(tile_start >= off, 1, 0)
        count = m.sum(axis=-1, keepdims=True).sum(axis=0, keepdims=True)
        pad_lo = jnp.where(lane == c, off + count, pad_lo)
        off = off + jnp.floor((count + float(TM - 1)) * (1.0 / TM)) * float(TM)
        pad_hi = jnp.where(lane == c, off, pad_hi)
    pos_ref[...] = pos.astype(jnp.int32)
    padlo_ref[...] = pad_lo.astype(jnp.int32)
    padhi_ref[...] = pad_hi.astype(jnp.int32)
    grp = (jnp.where(tile_cls >= 6, 1, 0) + jnp.where(tile_cls >= 12, 1, 0)
           + jnp.where(tile_cls >= 18, 1, 0))
    pair = tile_cls - PAIRS_PER_GROUP * grp
    lo = jnp.where(pair >= 3, 1, 0) + jnp.where(pair >= 5, 1, 0)
    hi = pair + 1 - 2 * jnp.where(pair >= 3, 1, 0) - jnp.where(pair >= 5, 1, 0)
    ea_ref[...] = EXPERTS_PER_GROUP * grp + lo
    eb_ref[...] = EXPERTS_PER_GROUP * grp + hi
    nu_ref[...] = jnp.broadcast_to(off * (1.0 / TM), nu_ref.shape).astype(jnp.int32)


def _slots(cls2d, *, n_slot_tiles):
    rows = cls2d.shape[0]
    rows_pad = -(-rows // V7X_LANES) * V7X_LANES
    n_t = -(-n_slot_tiles // V7X_LANES) * V7X_LANES
    i32 = jnp.int32
    return pl.pallas_call(
        functools.partial(_slots_kernel, rows_pad=rows_pad),
        out_shape=[jax.ShapeDtypeStruct(cls2d.shape, i32),
                   jax.ShapeDtypeStruct((1, n_t), i32), jax.ShapeDtypeStruct((1, n_t), i32),
                   jax.ShapeDtypeStruct((1, V7X_LANES), i32), jax.ShapeDtypeStruct((1, V7X_LANES), i32),
                   jax.ShapeDtypeStruct((1, V7X_LANES), i32)],
        compiler_params=pltpu.CompilerParams(vmem_limit_bytes=V7X_VMEM_LIMIT),
        name="moe_slots",
    )(cls2d)


def _row_copy(src, dst, src_row, dst_row, sem):
    return pltpu.make_async_copy(src.at[pl.ds(src_row, 1)], dst.at[pl.ds(dst_row, 1)], sem)


def _start_tile_rows(make_copy):
    for t in range(TM):
        make_copy(t).start()


def _dispatch_kernel(pos_ref, padlo_ref, padhi_ref, nu_ref, x_ref, xs_hbm, zbuf, stage, sem, sems,
                     *, n_slot_tiles, n_tiles):
    @pl.when(pl.program_id(0) == 0)
    def _():
        zbuf[...] = jnp.zeros(zbuf.shape, zbuf.dtype)

        def zero_tile(t, carry):
            cp = pltpu.make_async_copy(zbuf, xs_hbm.at[pl.ds(pl.multiple_of(t * TM, TM), TM)], sem)
            cp.start()
            cp.wait()
            return carry

        lax.fori_loop(nu_ref[0], n_slot_tiles, zero_tile, 0)

        for c in range(N_CLASSES):
            lo, hi = padlo_ref[c], padhi_ref[c]

            def zero_row(s, carry):
                _row_copy(zbuf, xs_hbm, 0, s, sem).start()
                return carry

            def zero_wait(s, carry):
                _row_copy(zbuf, xs_hbm, 0, 0, sem).wait()
                return carry

            lax.fori_loop(lo, hi, zero_row, 0)
            lax.fori_loop(lo, hi, zero_wait, 0)

    i = pl.program_id(0)
    slot = i % 2

    def wait(s):
        pltpu.make_async_copy(stage.at[s], xs_hbm.at[pl.ds(0, TM)], sems.at[s]).wait()

    for s in range(2):
        @pl.when(slot == s)
        def _():
            stage[s] = x_ref[...]
            _start_tile_rows(lambda t: _row_copy(stage.at[s], xs_hbm, t, pos_ref[0, 0, t], sems.at[s]))

    @pl.when(i > 0)
    def _():
        wait(1 - slot)

    @pl.when(i == n_tiles - 1)
    def _():
        wait(slot)


def _dispatch(pos3, pad_lo, pad_hi, n_used, x1g, *, n_slot_tiles):
    n_tiles = pos3.shape[0]
    width = x1g.shape[1]
    smem = pl.BlockSpec(memory_space=pltpu.SMEM)
    return pl.pallas_call(
        functools.partial(_dispatch_kernel, n_slot_tiles=n_slot_tiles, n_tiles=n_tiles),
        grid=(n_tiles,),
        in_specs=[pl.BlockSpec((1, 1, TM), lambda i: (i, 0, 0), memory_space=pltpu.SMEM),
                  smem, smem, smem,
                  pl.BlockSpec((TM, width), lambda i: (i, 0))],
        out_specs=pl.BlockSpec(memory_space=pl.ANY),
        out_shape=jax.ShapeDtypeStruct((n_slot_tiles * TM, width), x1g.dtype),
        scratch_shapes=[pltpu.VMEM((TM, width), x1g.dtype), pltpu.VMEM((2, TM, width), x1g.dtype),
                        pltpu.SemaphoreType.DMA(()), pltpu.SemaphoreType.DMA((2,))],
        compiler_params=pltpu.CompilerParams(dimension_semantics=("arbitrary",), has_side_effects=True,
                                             vmem_limit_bytes=V7X_VMEM_LIMIT),
        name="moe_dispatch",
    )(pos3, pad_lo, pad_hi, n_used, x1g)


def _moe_kernel(ea_ref, eb_ref, nu_ref, xs_ref, wga, wua, wda, wgb, wub, wdb, ys_ref, *, d):
    del ea_ref, eb_ref
    used = pl.program_id(0) < nu_ref[0]

    @pl.when(jnp.logical_not(used))
    def _():
        ys_ref[...] = jnp.zeros(ys_ref.shape, ys_ref.dtype)

    @pl.when(used)
    def _():
        x = jnp.concatenate(_unpack_bf16_pairs(xs_ref[:, :d // 2]), axis=1)
        gates = lax.bitcast_convert_type(xs_ref[:, d // 2:], F32)
        gate_a = gates[:, 0:1]
        gate_b = gates[:, 1:2]

        def expert(wg, wu, wd):
            g = jnp.dot(x, wg[0], preferred_element_type=F32)
            u = jnp.dot(x, wu[0], preferred_element_type=F32)
            h = (g * jax.nn.sigmoid(g)) * u
            return jnp.dot(h.astype(BF16), wd[0], preferred_element_type=F32)

        ys_ref[...] = gate_a * expert(wga, wua, wda) + gate_b * expert(wgb, wub, wdb)


def _moe(tile_ea, tile_eb, n_used, xs, w_gate, w_up, w_down):
    n_slots, width = xs.shape
    d = 2 * (width - GATE_LANES)
    n_tiles = n_slots // TM
    d_e = w_gate.shape[-1]

    def row_map(i, ea, eb, nu):
        return (jnp.minimum(i, nu[0] - 1), 0)

    def w_a(i, ea, eb, nu):
        return (ea[i], 0, 0)

    def w_b(i, ea, eb, nu):
        return (eb[i], 0, 0)

    grid_spec = pltpu.PrefetchScalarGridSpec(
        num_scalar_prefetch=3,
        grid=(n_tiles,),
        in_specs=[pl.BlockSpec((TM, width), row_map),
                  pl.BlockSpec((1, d, d_e), w_a), pl.BlockSpec((1, d, d_e), w_a),
                  pl.BlockSpec((1, d_e, d), w_a),
                  pl.BlockSpec((1, d, d_e), w_b), pl.BlockSpec((1, d, d_e), w_b),
                  pl.BlockSpec((1, d_e, d), w_b)],
        out_specs=pl.BlockSpec((TM, d), lambda i, ea, eb, nu: (i, 0)),
    )
    return pl.pallas_call(
        functools.partial(_moe_kernel, d=d),
        grid_spec=grid_spec,
        out_shape=jax.ShapeDtypeStruct((n_slots, d), F32),
        compiler_params=_cparams(("arbitrary",)),
        name="moe_experts",
    )(tile_ea, tile_eb, n_used, xs, w_gate, w_up, w_down, w_gate, w_up, w_down)


def _gather_ahead(pos_ref, posn_ref, ys_hbm, ybuf, sems, n_tiles):
    i = pl.program_id(0)
    slot = i % 2

    def wait(s):
        pltpu.make_async_copy(ys_hbm.at[pl.ds(0, TM)], ybuf.at[s], sems.at[s]).wait()

    @pl.when(i == 0)
    def _():
        def start(t, carry):
            _row_copy(ys_hbm, ybuf.at[0], pos_ref[0, 0, t], t, sems.at[0]).start()
            return carry
        lax.fori_loop(0, TM, start, 0)

    for s in range(2):
        @pl.when(slot == 1 - s)
        def _():
            _start_tile_rows(lambda t: _row_copy(ys_hbm, ybuf.at[s], posn_ref[0, 0, t], t, sems.at[s]))

    wait(slot)

    def drain():
        @pl.when(i == n_tiles - 1)
        def _():
            wait(1 - slot)

    return slot, drain


def _combine_final_kernel(pos_ref, posn_ref, ys_hbm, x1_ref, g_ref, b_ref, yp_ref, yt_ref, ybuf, sems,
                          *, alpha, d, n_tiles, n_prompt_tiles):
    slot, drain = _gather_ahead(pos_ref, posn_ref, ys_hbm, ybuf, sems, n_tiles)
    x2 = _layer_norm(alpha * x1_ref[:, :d] + ybuf[slot], g_ref[...], b_ref[...])
    is_prompt = pl.program_id(0) < n_prompt_tiles

    @pl.when(is_prompt)
    def _():
        yp_ref[...] = x2

    @pl.when(jnp.logical_not(is_prompt))
    def _():
        yt_ref[...] = x2

    drain()


def _combine_final(pos3, ys, x1g, g, b, *, alpha, n_prompt_tiles):
    t_pad, width = x1g.shape
    d = width
    n_tiles = t_pad // TM
    last_p = n_prompt_tiles - 1
    const = lambda i: (0, 0)
    return pl.pallas_call(
        functools.partial(_combine_final_kernel, alpha=alpha, d=d, n_tiles=n_tiles,
                          n_prompt_tiles=n_prompt_tiles),
        grid=(n_tiles,),
        in_specs=[pl.BlockSpec((1, 1, TM), lambda i: (i, 0, 0), memory_space=pltpu.SMEM),
                  pl.BlockSpec((1, 1, TM), lambda i: (jnp.minimum(i + 1, n_tiles - 1), 0, 0),
                               memory_space=pltpu.SMEM),
                  pl.BlockSpec(memory_space=pl.ANY),
                  pl.BlockSpec((TM, width), lambda i: (i, 0)),
                  pl.BlockSpec(g.shape, const), pl.BlockSpec(b.shape, const)],
        out_specs=[pl.BlockSpec((TM, d), lambda i: (jnp.minimum(i, last_p), 0)),
                   pl.BlockSpec((TM, d), lambda i: (jnp.maximum(i - n_prompt_tiles, 0), 0))],
        out_shape=[jax.ShapeDtypeStruct((n_prompt_tiles * TM, d), F32),
                   jax.ShapeDtypeStruct(((n_tiles - n_prompt_tiles) * TM, d), F32)],
        scratch_shapes=[pltpu.VMEM((2, TM, d), F32), pltpu.SemaphoreType.DMA((2,))],
        compiler_params=_cparams(("arbitrary",)),
        name="moe_combine_ln_final",
    )(pos3, pos3, ys, x1g, g, b)


def _combine_inproj_kernel(pos_ref, posn_ref, ys_hbm, x1_ref, g_ref, b_ref, w_ref,
                           x2_ref, h_ref, ka_ref, va_ref, kvb_ref, ybuf, sems,
                           *, alpha, d, n_tiles):
    slot, drain = _gather_ahead(pos_ref, posn_ref, ys_hbm, ybuf, sems, n_tiles)
    sub = TM // OUTPROJ_SUBTILES
    w = w_ref[...]
    for s in range(OUTPROJ_SUBTILES):
        rows = slice(s * sub, (s + 1) * sub)
        x2 = _layer_norm(alpha * x1_ref[rows, :d] + ybuf[slot, rows, :], g_ref[...], b_ref[...])
        x2_ref[rows, :] = x2
        acc = jnp.dot(x2.astype(BF16), w, preferred_element_type=F32)
        h_ref[rows, :] = acc.astype(BF16)
        _store_kv((ka_ref, va_ref, kvb_ref), rows, acc)

    drain()


def _combine_inproj(pos3, ys, x1g, g, b, w_in, *, alpha, seq, n_prompt_tiles):
    t_pad, width = x1g.shape
    d = width
    n_tiles = t_pad // TM
    tiles_per_seq = seq // TM
    n_seq = n_prompt_tiles // tiles_per_seq
    n_kv_blocks = n_seq + (n_tiles - n_prompt_tiles)
    const = lambda i: (0, 0)

    def kv_map(i):
        return (jnp.where(i < n_prompt_tiles, i // tiles_per_seq, n_seq + i - n_prompt_tiles), 0)

    kv_specs, kv_shapes = _kv_out(n_kv_blocks, kv_map)
    x2, h, *kv = pl.pallas_call(
        functools.partial(_combine_inproj_kernel, alpha=alpha, d=d, n_tiles=n_tiles),
        grid=(n_tiles,),
        in_specs=[pl.BlockSpec((1, 1, TM), lambda i: (i, 0, 0), memory_space=pltpu.SMEM),
                  pl.BlockSpec((1, 1, TM), lambda i: (jnp.minimum(i + 1, n_tiles - 1), 0, 0),
                               memory_space=pltpu.SMEM),
                  pl.BlockSpec(memory_space=pl.ANY),
                  pl.BlockSpec((TM, width), lambda i: (i, 0)),
                  pl.BlockSpec(g.shape, const), pl.BlockSpec(b.shape, const),
                  pl.BlockSpec(w_in.shape, const)],
        out_specs=[pl.BlockSpec((TM, d), lambda i: (i, 0)),
                   pl.BlockSpec((TM, D_IN), lambda i: (i, 0))] + kv_specs,
        out_shape=[jax.ShapeDtypeStruct((t_pad, d), F32),
                   jax.ShapeDtypeStruct((t_pad, D_IN), BF16)] + kv_shapes,
        scratch_shapes=[pltpu.VMEM((2, TM, d), F32), pltpu.SemaphoreType.DMA((2,))],
        compiler_params=_cparams(("arbitrary",)),
        name="moe_combine_ln_inproj",
    )(pos3, pos3, ys, x1g, g, b, w_in)
    return x2, h, kv


def _alibi_slopes():
    return 2.0 ** (-8.0 * jnp.arange(1, N_HEADS_B + 1, dtype=F32) / N_HEADS_B)


def _rel_bias_table(rel_bias, q0, n_q, n_k):
    n_diag = n_q + n_k - 1
    dist = q0 + (n_q - 1) - jnp.arange(n_diag, dtype=jnp.int32)
    diag = jnp.take(rel_bias.astype(F32), jnp.clip(dist, -REL_CLIP, REL_CLIP) + REL_CLIP, axis=1)
    n_h = diag.shape[0]
    ext = jnp.concatenate([diag, jnp.zeros((n_h, 1), F32)], axis=1)
    shifted = jnp.tile(ext, (1, n_q))[:, :n_q * n_diag].reshape(n_h, n_q, n_diag)
    return shifted[:, :, n_q - 1:n_q - 1 + n_k]


def _alibi_table(q0, n_q, n_k):
    dist = q0 + jnp.arange(n_q, dtype=jnp.int32)[:, None] - jnp.arange(n_k, dtype=jnp.int32)[None, :]
    return -_alibi_slopes()[:, None, None] * jnp.abs(dist).astype(F32)[None]


def kernel(x_prompt, x_sample, cache_a_k, cache_a_v, cache_b_k, cache_b_v, w_in, rel_bias, attn_sinks,
           gn_a, gn_b, w_out, ln1_g, ln1_b, w_router, b_router, w_gate, w_up, w_down, ln2_g, ln2_b):
    n_seq, seq, d = x_prompt.shape
    n_dec, dec_seq, _ = x_sample.shape
    depth = w_in.shape[0]
    rows_ca, rows_cb = cache_a_k.shape[2], cache_b_k.shape[2]
    alpha = (2.0 * depth) ** 0.25
    assert seq % TM == 0 and PAD_A == TM and n_dec * dec_seq <= TM and TM % dec_seq == 0
    assert dec_seq % 16 == 0 and seq >= PAD_A

    t_prompt = n_seq * seq
    t_real = t_prompt + n_dec * dec_seq
    t_pad = -(-t_real // (2 * TM)) * (2 * TM)
    n_prompt_tiles = t_prompt // TM
    n_slot_tiles = t_pad // TM + N_CLASSES

    x = x_prompt.reshape(t_prompt, d)
    x_tail = jnp.concatenate([x_sample.reshape(n_dec * dec_seq, d), jnp.zeros((t_pad - t_real, d), F32)], axis=0)

    w_in_b = w_in.astype(BF16)
    w_out_b = w_out.astype(BF16)
    w_gate_b, w_up_b, w_down_b = w_gate.astype(BF16), w_up.astype(BF16), w_down.astype(BF16)
    wr_t = w_router.astype(F32).T
    wr_hi = wr_t.astype(BF16)
    wr_lo = (wr_t - wr_hi.astype(F32)).astype(BF16)
    br = b_router.astype(F32).reshape(N_EXPERTS, 1)

    alibi_p = _alibi_table(PAD_B, CHUNK, BAND_B * CHUNK)
    alibi_s = _alibi_table(rows_cb, dec_seq, rows_cb + dec_seq)

    pa_k, pa_v, pb_k, pb_v, sa_k, sa_v, sb_k, sb_v = ([] for _ in range(8))
    h, kv = _inproj(x, x_tail, w_in_b[0], seq=seq, n_prompt_tiles=n_prompt_tiles)
    for l in range(depth):
        sinks = attn_sinks[l].astype(F32)
        gna = gn_a[l].astype(F32).reshape(1, WIDTH_A)
        gnb = gn_b[l].astype(F32).reshape(1, WIDTH_B)
        bias_pa = _rel_bias_table(rel_bias[l], PAD_A, CHUNK, BAND_A * CHUNK)
        a_prompt = _attn_prompt(h, sinks, bias_pa, alibi_p, gna, gnb, n_seq=n_seq, seq=seq)
        bias_sa = _rel_bias_table(rel_bias[l], rows_ca, dec_seq, rows_ca + dec_seq)
        bias_s = (bias_sa[:, :, :rows_ca], bias_sa[:, :, rows_ca:],
                  alibi_s[:, :, :rows_cb], alibi_s[:, :, rows_cb:])
        a_sample = _attn_sample(
            h, sinks,
            cache_a_k[l].reshape(n_dec, rows_ca, WIDTH_A), cache_a_v[l].reshape(n_dec, rows_ca, WIDTH_A),
            cache_b_k[l].reshape(n_dec, rows_cb, KV_WIDTH_B), cache_b_v[l].reshape(n_dec, rows_cb, KV_WIDTH_B),
            bias_s, gna, gnb, first_row=t_prompt, n_rows_out=t_pad - t_prompt, n_dec=n_dec, dec_seq=dec_seq)

        x1g, xpk, cls = _outproj(a_prompt, a_sample, x, x_tail, w_out_b[l],
                            ln1_g[l].astype(F32).reshape(1, d), ln1_b[l].astype(F32).reshape(1, d),
                            wr_hi, wr_lo, br, n_tiles=t_pad // TM, n_prompt_tiles=n_prompt_tiles, alpha=alpha)
        pos, tile_ea, tile_eb, n_used, pad_lo, pad_hi = _slots(
            cls.reshape(t_pad // V7X_LANES, V7X_LANES), n_slot_tiles=n_slot_tiles)
        pos3 = pos.reshape(t_pad // TM, 1, TM)
        xs = _dispatch(pos3, pad_lo[0], pad_hi[0], n_used[0], xpk, n_slot_tiles=n_slot_tiles)
        ys = _moe(tile_ea[0, :n_slot_tiles], tile_eb[0, :n_slot_tiles], n_used[0, :1],
                  xs, w_gate_b[l], w_up_b[l], w_down_b[l])
        kv_l = kv
        g2, b2 = ln2_g[l].astype(F32).reshape(1, d), ln2_b[l].astype(F32).reshape(1, d)
        if l + 1 < depth:
            x, h, kv = _combine_inproj(pos3, ys, x1g, g2, b2, w_in_b[l + 1], alpha=alpha, seq=seq,
                                       n_prompt_tiles=n_prompt_tiles)
        else:
            y_p, y_t = _combine_final(pos3, ys, x1g, g2, b2, alpha=alpha, n_prompt_tiles=n_prompt_tiles)

        ka_l, va_l, kvb_l = kv_l
        n_p, n_s = n_seq * TM, n_dec * dec_seq
        ra, rb = min(PAD_A, seq), min(WINDOW_B, seq)
        kvb_p = kvb_l[:n_p].reshape(n_seq, TM, 2 * KV_WIDTH_B)[:, TM - rb:]
        kvb_s = kvb_l[n_p:n_p + n_s].reshape(n_dec, dec_seq, 2 * KV_WIDTH_B)
        pa_k.append(ka_l[:n_p].reshape(n_seq, TM, N_HEADS_A, HEAD_DIM)[:, TM - ra:])
        pa_v.append(va_l[:n_p].reshape(n_seq, TM, N_HEADS_A, HEAD_DIM)[:, TM - ra:])
        pb_k.append(kvb_p[:, :, :KV_WIDTH_B].reshape(n_seq, rb, N_KV_B, HEAD_DIM))
        pb_v.append(kvb_p[:, :, KV_WIDTH_B:].reshape(n_seq, rb, N_KV_B, HEAD_DIM))
        sa_k.append(ka_l[n_p:n_p + n_s].reshape(n_dec, dec_seq, N_HEADS_A, HEAD_DIM))
        sa_v.append(va_l[n_p:n_p + n_s].reshape(n_dec, dec_seq, N_HEADS_A, HEAD_DIM))
        sb_k.append(kvb_s[:, :, :KV_WIDTH_B].reshape(n_dec, dec_seq, N_KV_B, HEAD_DIM))
        sb_v.append(kvb_s[:, :, KV_WIDTH_B:].reshape(n_dec, dec_seq, N_KV_B, HEAD_DIM))

    y_prompt = y_p.reshape(n_seq, seq, d)
    y_sample = y_t[:n_dec * dec_seq].reshape(n_dec, dec_seq, d)
    return (y_prompt, y_sample,
            jnp.stack(pa_k), jnp.stack(pa_v), jnp.stack(pb_k), jnp.stack(pb_v),
            jnp.stack(sa_k), jnp.stack(sa_v), jnp.stack(sb_k), jnp.stack(sb_v))
```

```python
import functools

import jax
import jax.numpy as jnp
import numpy as np
from jax import lax
from jax.experimental import pallas as pl
from jax.experimental.pallas import tpu as pltpu

CHUNK = 64
HEAD_DIM = 64
N_HEADS_A = 8
BAND_A = 9
REL_CLIP = 128
N_HEADS_B = 8
N_KV_B = 2
WINDOW_B = 128
BAND_B = 1 + WINDOW_B // CHUNK
N_EXPERTS = 16
N_GROUPS = 4
EXPERTS_PER_GROUP = N_EXPERTS // N_GROUPS
PAIRS_PER_GROUP = 6
N_CLASSES = N_GROUPS * PAIRS_PER_GROUP
LN_EPS = 1e-5
RMS_EPS = 1e-6

WIDTH_A = N_HEADS_A * HEAD_DIM
WIDTH_B = N_HEADS_B * HEAD_DIM
KV_WIDTH_B = N_KV_B * HEAD_DIM
D_IN = 3 * WIDTH_A + WIDTH_B + 2 * KV_WIDTH_B
COL_QA, COL_KA, COL_VA = 0, WIDTH_A, 2 * WIDTH_A
COL_QB = 3 * WIDTH_A
COL_KB = COL_QB + WIDTH_B
COL_VB = COL_KB + KV_WIDTH_B
KV_OUT = 2 * WIDTH_A + 2 * KV_WIDTH_B
PAD_A = (BAND_A - 1) * CHUNK
PAD_B = WINDOW_B

V7X_LANES = 128
TM = 512
GATE_LANES = V7X_LANES
V7X_VMEM_LIMIT = 56 * 1024 * 1024

F32 = jnp.float32
BF16 = jnp.bfloat16
NEG_INF = float("-inf")


def _cparams(sem):
    return pltpu.CompilerParams(dimension_semantics=sem, vmem_limit_bytes=V7X_VMEM_LIMIT)


def _inproj_kernel(xp_ref, xt_ref, w_ref, h_ref, kv_ref, *, tiles_per_seq, n_prompt_tiles):
    i = pl.program_id(0)
    x = jnp.where(i < n_prompt_tiles, xp_ref[...], xt_ref[...])
    acc = jnp.dot(x.astype(BF16), w_ref[...], preferred_element_type=F32)
    h_ref[...] = acc.astype(BF16)
    keeps_kv = jnp.logical_or(i % tiles_per_seq == tiles_per_seq - 1, i >= n_prompt_tiles)

    @pl.when(keeps_kv)
    def _():
        kv_ref[:, : 2 * WIDTH_A] = acc[:, COL_KA:COL_QB]
        kv_ref[:, 2 * WIDTH_A:] = acc[:, COL_KB:]


def _inproj(x_prompt, x_tail, w_in, *, seq, n_prompt_tiles):
    d = x_prompt.shape[1]
    t_pad = x_prompt.shape[0] + x_tail.shape[0]
    n_tiles = t_pad // TM
    last_p = n_prompt_tiles - 1
    tiles_per_seq = seq // TM
    n_seq = n_prompt_tiles // tiles_per_seq
    n_kv_blocks = n_seq + (n_tiles - n_prompt_tiles)

    def kv_map(i):
        return (jnp.where(i < n_prompt_tiles, i // tiles_per_seq, n_seq + i - n_prompt_tiles), 0)

    return pl.pallas_call(
        functools.partial(_inproj_kernel, tiles_per_seq=tiles_per_seq, n_prompt_tiles=n_prompt_tiles),
        grid=(n_tiles,),
        in_specs=[pl.BlockSpec((TM, d), lambda i: (jnp.minimum(i, last_p), 0)),
                  pl.BlockSpec((TM, d), lambda i: (jnp.maximum(i - n_prompt_tiles, 0), 0)),
                  pl.BlockSpec((d, D_IN), lambda i: (0, 0))],
        out_specs=[pl.BlockSpec((TM, D_IN), lambda i: (i, 0)),
                   pl.BlockSpec((TM, KV_OUT), kv_map)],
        out_shape=[jax.ShapeDtypeStruct((t_pad, D_IN), BF16),
                   jax.ShapeDtypeStruct((n_kv_blocks * TM, KV_OUT), F32)],
        compiler_params=_cparams(("arbitrary",)),
        name="inproj",
    )(x_prompt, x_tail, w_in)


def _softmax_pv(s_parts, v_parts, sink=None):
    m = s_parts[0].max(axis=-1, keepdims=True)
    for s in s_parts[1:]:
        m = jnp.maximum(m, s.max(axis=-1, keepdims=True))
    if sink is not None:
        m = jnp.maximum(m, sink)
    denom = None
    acc = None
    for s, v in zip(s_parts, v_parts):
        e = jnp.exp(s - m)
        d = e.sum(axis=-1, keepdims=True)
        pv = jnp.dot(e.astype(BF16), v, preferred_element_type=F32)
        denom = d if denom is None else denom + d
        acc = pv if acc is None else acc + pv
    if sink is not None:
        denom = denom + jnp.exp(sink - m)
    return acc / denom


def _nt_dot(a, b):
    return lax.dot_general(a, b, (((1,), (1,)), ((), ())), preferred_element_type=F32)


def _head_pair_lane_mask(rows):
    return lax.broadcasted_iota(jnp.int32, (rows, 2 * HEAD_DIM), 1) < HEAD_DIM


def _rms_store(o_ref, row_slice, col0, pairs, gn_ref):
    ssq = None
    for o in pairs:
        s = (o * o).sum(axis=-1, keepdims=True)
        ssq = s if ssq is None else ssq + s
    width = len(pairs) * 2 * HEAD_DIM
    inv = lax.rsqrt(ssq / width + RMS_EPS)
    for p, o in enumerate(pairs):
        c0 = p * 2 * HEAD_DIM
        g = gn_ref[:, c0:c0 + 2 * HEAD_DIM]
        o_ref[row_slice, col0 + c0:col0 + c0 + 2 * HEAD_DIM] = (o * inv * g).astype(o_ref.dtype)


KEY_BLOCK = V7X_LANES


def _swap_halves(x):
    return jnp.concatenate([x[:, HEAD_DIM:], x[:, :HEAD_DIM]], axis=1)


def _pair_rows(q2, lane_lo, kv_half=None):
    hi_lanes = jnp.logical_not(lane_lo)
    zero = jnp.zeros_like(q2)
    if kv_half is None:
        lo, hi = jnp.where(lane_lo, q2, zero), jnp.where(hi_lanes, q2, zero)
    elif kv_half == 0:
        lo, hi = jnp.where(lane_lo, q2, zero), jnp.where(lane_lo, _swap_halves(q2), zero)
    else:
        lo, hi = jnp.where(hi_lanes, _swap_halves(q2), zero), jnp.where(hi_lanes, q2, zero)
    return jnp.concatenate([lo, hi], axis=0)


def _tn_dot(a, b):
    return lax.dot_general(a, b, (((0,), (0,)), ((), ())), preferred_element_type=F32)


PV_KEYS = 256


def _score_blocks(k_ref, lanes, row0, n_keys, w):
    out = []
    for off in range(0, n_keys, KEY_BLOCK):
        n = min(KEY_BLOCK, n_keys - off)
        start = pl.multiple_of(row0 + off, CHUNK)
        out.append((_nt_dot(k_ref[pl.ds(start, n), lanes], w), start, n, off))
    return out


def _pair_probs(score_blocks, lane_tile, bias_ref, pair, pad, masked, sink_block=None):
    lanes = slice(lane_tile * V7X_LANES, (lane_tile + 1) * V7X_LANES)
    blocks = []
    for s_all, start, n, off in score_blocks:
        s = s_all[:, lanes] + bias_ref[pair, off:off + n, :]
        if masked:
            key_row = lax.broadcasted_iota(jnp.int32, s.shape, 0) + start
            s = jnp.where(key_row >= pad, s, NEG_INF)
        blocks.append((s, start, n))
    m = None
    for s, _, _ in blocks:
        bm = s.max(axis=0, keepdims=True)
        m = bm if m is None else jnp.maximum(m, bm)
    if sink_block is not None:
        m = jnp.maximum(m, sink_block.max(axis=0, keepdims=True))
    probs = [(jnp.exp(s - m).astype(BF16), start, n) for s, start, n in blocks]
    sink_p = None if sink_block is None else jnp.exp(sink_block - m).astype(BF16)
    return probs, sink_p


def _pair_values(probs, sink_p, v_ref, lanes):
    groups, cur, cur_rows = [], [], 0
    for blk in probs:
        if cur and cur_rows + blk[2] > PV_KEYS:
            groups.append(cur)
            cur, cur_rows = [], 0
        cur.append(blk)
        cur_rows += blk[2]
    groups.append(cur)
    acc = None
    for gi, grp in enumerate(groups):
        rows = sum(n for _, _, n in grp)
        p = grp[0][0] if len(grp) == 1 else jnp.concatenate([b[0] for b in grp], axis=0)
        v1 = jnp.concatenate([v_ref[pl.ds(grp[0][1], rows), lanes], jnp.ones((rows, V7X_LANES), BF16)], axis=1)
        if sink_p is not None and gi == len(groups) - 1:
            n = sink_p.shape[0]
            p = jnp.concatenate([p, sink_p], axis=0)
            sink_v = jnp.concatenate([jnp.zeros((n, V7X_LANES), BF16), jnp.ones((n, V7X_LANES), BF16)], axis=1)
            v1 = jnp.concatenate([v1, sink_v], axis=0)
        c = _tn_dot(p, v1)
        acc = c if acc is None else acc + c
    return acc[:, :V7X_LANES] / acc[:, V7X_LANES:]


def _quad_rows(q4):
    head_of_lane = lax.broadcasted_iota(jnp.int32, q4.shape, 1) // HEAD_DIM
    zero = jnp.zeros_like(q4)
    return jnp.concatenate([jnp.where(head_of_lane == h, q4, zero) for h in range(4)], axis=0)


def _attn_prompt_kernel(qa_ref, ka_ref, va_ref, qb_ref, kb_ref, vb_ref,
                        ba_ref, bb_ref, sink_ref, gna_ref, gnb_ref, o_ref,
                        kpa, vpa, kpb, vpb, pa_even, pb_even, pa_odd, pb_odd, *, seq, tq):
    j = pl.program_id(1)

    @pl.when(j == 0)
    def _():
        kpa[0:PAD_A, :] = jnp.zeros((PAD_A, WIDTH_A), BF16)
        vpa[0:PAD_A, :] = jnp.zeros((PAD_A, WIDTH_A), BF16)
        kpb[0:PAD_B, :] = jnp.zeros((PAD_B, KV_WIDTH_B), BF16)
        vpb[0:PAD_B, :] = jnp.zeros((PAD_B, KV_WIDTH_B), BF16)
        kpa[PAD_A:PAD_A + seq, :] = ka_ref[...]
        vpa[PAD_A:PAD_A + seq, :] = va_ref[...]
        kpb[PAD_B:PAD_B + seq, :] = kb_ref[...]
        vpb[PAD_B:PAD_B + seq, :] = vb_ref[...]

    lane_lo = _head_pair_lane_mask(CHUNK)
    scale = HEAD_DIM ** -0.5
    all_lanes = slice(0, KV_WIDTH_B)

    n_pairs_a, n_pairs_b = N_HEADS_A // 2, N_HEADS_B // 2
    band_a, band_b = BAND_A * CHUNK, BAND_B * CHUNK
    n_sink = sink_ref.shape[1]
    even_bufs, odd_bufs = (pa_even, pb_even), (pa_odd, pb_odd)

    def band_start(c):
        return pl.multiple_of(j * tq + c * CHUNK, CHUNK)

    def probs_phase(c, masked, bufs):
        pa_buf, pb_buf = bufs
        rows = pl.ds(pl.multiple_of(c * CHUNK, CHUNK), CHUNK)
        cs = band_start(c)
        scores_a = []
        for quad in range(N_HEADS_A // 4):
            lanes = slice(quad * 4 * HEAD_DIM, (quad + 1) * 4 * HEAD_DIM)
            w = _quad_rows(qa_ref[rows, lanes] * scale)
            scores_a.append(_score_blocks(kpa, lanes, cs, band_a, w))
        w = jnp.concatenate(
            [_pair_rows(qb_ref[rows, p * 2 * HEAD_DIM:(p + 1) * 2 * HEAD_DIM] * scale, lane_lo,
                        kv_half=(2 * p) // (N_HEADS_B // N_KV_B))
             for p in range(n_pairs_b)], axis=0)
        scores_b = _score_blocks(kpb, all_lanes, cs, band_b, w)
        for p in range(n_pairs_a):
            probs, _ = _pair_probs(scores_a[p // 2], p % 2, ba_ref, p, PAD_A, masked)
            for blk, (_, _, n, off) in zip(probs, scores_a[p // 2]):
                pa_buf[p, off:off + n, :] = blk[0]
        for p in range(n_pairs_b):
            probs, sink_p = _pair_probs(scores_b, p, bb_ref, p, PAD_B, masked, sink_block=sink_ref[p])
            for blk, (_, _, n, off) in zip(probs, scores_b):
                pb_buf[p, off:off + n, :] = blk[0]
            pb_buf[p, band_b:band_b + n_sink, :] = sink_p

    def values_phase(c, bufs):
        pa_buf, pb_buf = bufs
        rows = pl.ds(pl.multiple_of(c * CHUNK, CHUNK), CHUNK)
        cs = band_start(c)

        def groups(buf, p, n_keys):
            return [(buf[p, off:off + min(PV_KEYS, n_keys - off), :], pl.multiple_of(cs + off, CHUNK),
                     min(PV_KEYS, n_keys - off)) for off in range(0, n_keys, PV_KEYS)]

        pairs_a = []
        for p in range(n_pairs_a):
            lanes = slice(p * 2 * HEAD_DIM, (p + 1) * 2 * HEAD_DIM)
            o = _pair_values(groups(pa_buf, p, band_a), None, vpa, lanes)
            pairs_a.append(jnp.where(lane_lo, o[:CHUNK], o[CHUNK:]))
        _rms_store(o_ref, rows, 0, pairs_a, gna_ref)

        pairs_b = []
        for p in range(n_pairs_b):
            kv_half = (2 * p) // (N_HEADS_B // N_KV_B)
            o = _pair_values(groups(pb_buf, p, band_b), pb_buf[p, band_b:band_b + n_sink, :], vpb, all_lanes)
            lo, hi = o[:CHUNK], o[CHUNK:]
            if kv_half == 0:
                hi = _swap_halves(hi)
            else:
                lo = _swap_halves(lo)
            pairs_b.append(jnp.where(lane_lo, lo, hi))
        _rms_store(o_ref, rows, WIDTH_A, pairs_b, gnb_ref)

    def run(masked):
        n_chunks = tq // CHUNK
        bufs = (even_bufs, odd_bufs)
        probs_phase(0, masked, even_bufs)
        for c in range(1, n_chunks):
            probs_phase(c, masked, bufs[c % 2])
            values_phase(c - 1, bufs[(c - 1) % 2])
        values_phase(n_chunks - 1, bufs[(n_chunks - 1) % 2])

    @pl.when(j == 0)
    def _():
        run(True)

    @pl.when(j > 0)
    def _():
        run(False)


def _pair_transposed(table):
    n_h, n_q, n_k = table.shape
    return table.reshape(n_h // 2, 2, n_q, n_k).transpose(0, 3, 1, 2).reshape(n_h // 2, n_k, 2 * n_q)


def _sink_blocks(sinks):
    row = jnp.repeat(sinks.astype(F32).reshape(N_HEADS_B // 2, 2), HEAD_DIM, axis=1)
    rest = jnp.full((N_HEADS_B // 2, 15, 2 * HEAD_DIM), NEG_INF, F32)
    return jnp.concatenate([row[:, None, :], rest], axis=1)


def _attn_prompt(h, sinks, bias_a, bias_b, gn_a, gn_b, *, n_seq, seq):
    tq = TM
    assert tq >= PAD_A and tq >= PAD_B and tq % CHUNK == 0
    n_q = seq // tq
    vmem = pltpu.VMEM
    bias_a, bias_b = _pair_transposed(bias_a), _pair_transposed(bias_b)
    sinks = _sink_blocks(sinks)
    return pl.pallas_call(
        functools.partial(_attn_prompt_kernel, seq=seq, tq=tq),
        grid=(n_seq, n_q),
        in_specs=[
            pl.BlockSpec((tq, WIDTH_A), lambda b, j: (b * n_q + j, COL_QA // WIDTH_A)),
            pl.BlockSpec((seq, WIDTH_A), lambda b, j: (b, COL_KA // WIDTH_A)),
            pl.BlockSpec((seq, WIDTH_A), lambda b, j: (b, COL_VA // WIDTH_A)),
            pl.BlockSpec((tq, WIDTH_B), lambda b, j: (b * n_q + j, COL_QB // WIDTH_B)),
            pl.BlockSpec((seq, KV_WIDTH_B), lambda b, j: (b, COL_KB // KV_WIDTH_B)),
            pl.BlockSpec((seq, KV_WIDTH_B), lambda b, j: (b, COL_VB // KV_WIDTH_B)),
            pl.BlockSpec(bias_a.shape, lambda b, j: (0, 0, 0)),
            pl.BlockSpec(bias_b.shape, lambda b, j: (0, 0, 0)),
            pl.BlockSpec(sinks.shape, lambda b, j: (0, 0, 0)),
            pl.BlockSpec(gn_a.shape, lambda b, j: (0, 0)),
            pl.BlockSpec(gn_b.shape, lambda b, j: (0, 0)),
        ],
        out_specs=pl.BlockSpec((tq, WIDTH_A + WIDTH_B), lambda b, j: (b * n_q + j, 0)),
        out_shape=jax.ShapeDtypeStruct((n_seq * seq, WIDTH_A + WIDTH_B), BF16),
        scratch_shapes=[vmem((PAD_A + seq, WIDTH_A), BF16), vmem((PAD_A + seq, WIDTH_A), BF16),
                        vmem((PAD_B + seq, KV_WIDTH_B), BF16), vmem((PAD_B + seq, KV_WIDTH_B), BF16),
                        vmem((N_HEADS_A // 2, BAND_A * CHUNK, V7X_LANES), BF16),
                        vmem((N_HEADS_B // 2, BAND_B * CHUNK + sinks.shape[1], V7X_LANES), BF16),
                        vmem((N_HEADS_A // 2, BAND_A * CHUNK, V7X_LANES), BF16),
                        vmem((N_HEADS_B // 2, BAND_B * CHUNK + sinks.shape[1], V7X_LANES), BF16)],
        compiler_params=_cparams(("arbitrary", "arbitrary")),
        name="attn_prompt",
    )(h, h, h, h, h, h, bias_a, bias_b, sinks, gn_a, gn_b)


def _attn_sample_kernel(sink_ref, qa_ref, ka_ref, va_ref, qb_ref, kb_ref, vb_ref,
                        cka_ref, cva_ref, ckb_ref, cvb_ref,
                        bac_ref, ban_ref, bbc_ref, bbn_ref, gna_ref, gnb_ref, o_ref, *, n_dec, dec_seq):
    b = pl.program_id(0)

    @pl.when(b >= n_dec)
    def _():
        o_ref[...] = jnp.zeros(o_ref.shape, o_ref.dtype)

    @pl.when(b < n_dec)
    def _():
        lane_lo = _head_pair_lane_mask(dec_seq)
        scale = HEAD_DIM ** -0.5
        rows = slice(0, dec_seq)

        pairs_a = []
        for p in range(N_HEADS_A // 2):
            lanes = slice(p * 2 * HEAD_DIM, (p + 1) * 2 * HEAD_DIM)
            q2 = qa_ref[:, lanes] * scale
            kc = cka_ref[0, :, lanes].astype(BF16)
            vc = cva_ref[0, :, lanes].astype(BF16)
            kn = ka_ref[:, lanes]
            vn = va_ref[:, lanes]
            halves = []
            for half in range(2):
                h = 2 * p + half
                keep = lane_lo if half == 0 else jnp.logical_not(lane_lo)
                qm = jnp.where(keep, q2, jnp.zeros_like(q2))
                s_c = _nt_dot(qm, kc) + bac_ref[h]
                s_n = _nt_dot(qm, kn) + ban_ref[h]
                halves.append(_softmax_pv([s_c, s_n], [vc, vn]))
            pairs_a.append(jnp.where(lane_lo, halves[0], halves[1]))
        _rms_store(o_ref, rows, 0, pairs_a, gna_ref)

        kc = ckb_ref[0].astype(BF16)
        vc = cvb_ref[0].astype(BF16)
        kn = kb_ref[...]
        vn = vb_ref[...]
        pairs_b = []
        for p in range(N_HEADS_B // 2):
            lanes = slice(p * 2 * HEAD_DIM, (p + 1) * 2 * HEAD_DIM)
            q2 = qb_ref[:, lanes] * scale
            halves = []
            for half in range(2):
                h = 2 * p + half
                kv_head = h // (N_HEADS_B // N_KV_B)
                keep = lane_lo if half == 0 else jnp.logical_not(lane_lo)
                qm = jnp.where(keep, q2, jnp.zeros_like(q2))
                if kv_head != half:
                    qm = jnp.concatenate([qm[:, HEAD_DIM:], qm[:, :HEAD_DIM]], axis=1)
                s_c = _nt_dot(qm, kc) + bbc_ref[h]
                s_n = _nt_dot(qm, kn) + bbn_ref[h]
                o = _softmax_pv([s_c, s_n], [vc, vn], sink=sink_ref[h])
                if kv_head != half:
                    o = jnp.concatenate([o[:, HEAD_DIM:], o[:, :HEAD_DIM]], axis=1)
                halves.append(o)
            pairs_b.append(jnp.where(lane_lo, halves[0], halves[1]))
        _rms_store(o_ref, rows, WIDTH_A, pairs_b, gnb_ref)


def _attn_sample(h, sinks, cka, cva, ckb, cvb, bias, gn_a, gn_b, *, first_row, n_rows_out, n_dec, dec_seq):
    bac, ban, bbc, bbn = bias
    n_steps = n_rows_out // dec_seq
    rb = first_row // dec_seq
    last = n_dec - 1

    def hrow(b):
        return rb + jnp.minimum(b, last)

    def cache_spec(c):
        return pl.BlockSpec((1,) + c.shape[1:], lambda b: (jnp.minimum(b, last), 0, 0))

    def const_spec(a):
        return pl.BlockSpec(a.shape, lambda b: (0,) * a.ndim)

    return pl.pallas_call(
        functools.partial(_attn_sample_kernel, n_dec=n_dec, dec_seq=dec_seq),
        grid=(n_steps,),
        in_specs=[
            pl.BlockSpec(memory_space=pltpu.SMEM),
            pl.BlockSpec((dec_seq, WIDTH_A), lambda b: (hrow(b), COL_QA // WIDTH_A)),
            pl.BlockSpec((dec_seq, WIDTH_A), lambda b: (hrow(b), COL_KA // WIDTH_A)),
            pl.BlockSpec((dec_seq, WIDTH_A), lambda b: (hrow(b), COL_VA // WIDTH_A)),
            pl.BlockSpec((dec_seq, WIDTH_B), lambda b: (hrow(b), COL_QB // WIDTH_B)),
            pl.BlockSpec((dec_seq, KV_WIDTH_B), lambda b: (hrow(b), COL_KB // KV_WIDTH_B)),
            pl.BlockSpec((dec_seq, KV_WIDTH_B), lambda b: (hrow(b), COL_VB // KV_WIDTH_B)),
            cache_spec(cka), cache_spec(cva), cache_spec(ckb), cache_spec(cvb),
            const_spec(bac), const_spec(ban), const_spec(bbc), const_spec(bbn),
            const_spec(gn_a), const_spec(gn_b),
        ],
        out_specs=pl.BlockSpec((dec_seq, WIDTH_A + WIDTH_B), lambda b: (b, 0)),
        out_shape=jax.ShapeDtypeStruct((n_rows_out, WIDTH_A + WIDTH_B), BF16),
        compiler_params=_cparams(("arbitrary",)),
        name="attn_sample",
    )(sinks, h, h, h, h, h, h, cka, cva, ckb, cvb, bac, ban, bbc, bbn, gn_a, gn_b)


def _layer_norm(r, g, b):
    mu = r.mean(axis=-1, keepdims=True)
    c = r - mu
    var = (c * c).mean(axis=-1, keepdims=True)
    return c * lax.rsqrt(var + LN_EPS) * g + b


def _route_rows(logits):
    rows = [logits[e:e + 1, :] for e in range(N_EXPERTS)]

    def top2_sum(a, b, c, d):
        hi1, lo1 = jnp.maximum(a, b), jnp.minimum(a, b)
        hi2, lo2 = jnp.maximum(c, d), jnp.minimum(c, d)
        return jnp.maximum(hi1, hi2) + jnp.maximum(jnp.minimum(hi1, hi2), jnp.maximum(lo1, lo2))

    scores = [top2_sum(*rows[4 * g:4 * g + 4]) for g in range(N_GROUPS)]
    best = scores[0]
    g_sel = jnp.zeros(best.shape, jnp.int32)
    for g in range(1, N_GROUPS):
        upd = scores[g] > best
        best = jnp.where(upd, scores[g], best)
        g_sel = jnp.where(upd, g, g_sel)

    vals = []
    for k in range(EXPERTS_PER_GROUP):
        v = rows[k]
        for g in range(1, N_GROUPS):
            v = jnp.where(g_sel == g, rows[4 * g + k], v)
        vals.append(v)

    v1 = vals[0]
    i1 = jnp.zeros(v1.shape, jnp.int32)
    for k in range(1, EXPERTS_PER_GROUP):
        upd = vals[k] > v1
        v1 = jnp.where(upd, vals[k], v1)
        i1 = jnp.where(upd, k, i1)
    v2 = jnp.full(v1.shape, NEG_INF, F32)
    i2 = jnp.zeros(v1.shape, jnp.int32)
    for k in range(EXPERTS_PER_GROUP):
        upd = jnp.logical_and(i1 != k, vals[k] > v2)
        v2 = jnp.where(upd, vals[k], v2)
        i2 = jnp.where(upd, k, i2)

    e = jnp.exp(v2 - v1)
    den = 1.0 + e
    w1 = 1.0 / den
    w2 = e / den
    first_is_lo = i1 < i2
    lo = jnp.where(first_is_lo, i1, i2)
    hi = jnp.where(first_is_lo, i2, i1)
    gate_lo = jnp.where(first_is_lo, w1, w2)
    gate_hi = jnp.where(first_is_lo, w2, w1)
    pair_base = jnp.where(lo == 0, 0, jnp.where(lo == 1, 3, 5))
    cls = g_sel * PAIRS_PER_GROUP + pair_base + hi - lo - 1
    return cls, gate_lo, gate_hi


OUTPROJ_SUBTILES = 4


def _outproj_kernel(ap_ref, as_ref, xm_ref, xt_ref, w_ref, g_ref, b_ref, wrh_ref, wrl_ref, br_ref,
                    x1_ref, cls_ref, *, n_prompt_tiles, n_main_tiles, alpha, d):
    i = pl.program_id(0)
    sub = TM // OUTPROJ_SUBTILES
    w = w_ref[...]
    ys = []
    for s in range(OUTPROJ_SUBTILES):
        rows = slice(s * sub, (s + 1) * sub)
        a = jnp.where(i < n_prompt_tiles, ap_ref[rows, :], as_ref[rows, :])
        ys.append(jnp.dot(a, w, preferred_element_type=F32))
    for s in range(OUTPROJ_SUBTILES):
        rows = slice(s * sub, (s + 1) * sub)
        x = jnp.where(i < n_main_tiles, xm_ref[rows, :], xt_ref[rows, :])
        x1 = _layer_norm(alpha * x + ys[s], g_ref[...], b_ref[...])
        x1_ref[rows, :d] = x1
        x_hi = x1.astype(BF16)
        x_lo = (x1 - x_hi.astype(F32)).astype(BF16)
        logits = (_nt_dot(wrh_ref[...], x_hi) + _nt_dot(wrh_ref[...], x_lo) + _nt_dot(wrl_ref[...], x_hi)
                  + br_ref[...])
        cls, gate_lo, gate_hi = _route_rows(logits)
        cls_ref[:, rows] = cls
        gates = jnp.concatenate([gate_lo, gate_hi, jnp.zeros((GATE_LANES - 2, sub), F32)], axis=0)
        x1_ref[rows, d:] = gates.T


def _outproj(a_prompt, a_sample, x_main, x_tail, w_out, g, b, wr_hi, wr_lo, br, *, n_tiles, n_prompt_tiles, alpha):
    d = x_main.shape[1]
    t_pad = n_tiles * TM
    n_main_tiles = x_main.shape[0] // TM
    last_p = n_prompt_tiles - 1
    const = lambda i: (0, 0)
    return pl.pallas_call(
        functools.partial(_outproj_kernel, n_prompt_tiles=n_prompt_tiles, n_main_tiles=n_main_tiles,
                          alpha=alpha, d=d),
        grid=(n_tiles,),
        in_specs=[pl.BlockSpec((TM, d), lambda i: (jnp.minimum(i, last_p), 0)),
                  pl.BlockSpec((TM, d), lambda i: (jnp.maximum(i - n_prompt_tiles, 0), 0)),
                  pl.BlockSpec((TM, d), lambda i: (jnp.minimum(i, n_main_tiles - 1), 0)),
                  pl.BlockSpec((TM, d), lambda i: (jnp.maximum(i - n_main_tiles, 0), 0)),
                  pl.BlockSpec(w_out.shape, const),
                  pl.BlockSpec(g.shape, const), pl.BlockSpec(b.shape, const),
                  pl.BlockSpec(wr_hi.shape, const), pl.BlockSpec(wr_lo.shape, const),
                  pl.BlockSpec(br.shape, const)],
        out_specs=[pl.BlockSpec((TM, d + GATE_LANES), lambda i: (i, 0)),
                   pl.BlockSpec((1, TM), lambda i: (0, i))],
        out_shape=[jax.ShapeDtypeStruct((t_pad, d + GATE_LANES), F32),
                   jax.ShapeDtypeStruct((1, t_pad), jnp.int32)],
        compiler_params=_cparams(("arbitrary",)),
        name="outproj_ln_route",
    )(a_prompt, a_sample, x_main, x_tail, w_out, g, b, wr_hi, wr_lo, br)


def _slots_kernel(cls_ref, pos_ref, ea_ref, eb_ref, nu_ref, padlo_ref, padhi_ref, *, rows_pad):
    lane = lax.broadcasted_iota(jnp.int32, (1, V7X_LANES), 1)
    pad_lo = jnp.zeros((1, V7X_LANES), F32)
    pad_hi = jnp.zeros((1, V7X_LANES), F32)
    cls = cls_ref[...]
    rows = cls.shape[0]
    r_i = lax.broadcasted_iota(jnp.int32, (rows_pad, rows_pad), 0)
    c_i = lax.broadcasted_iota(jnp.int32, (rows_pad, rows_pad), 1)
    lower = jnp.where(c_i < r_i, 1.0, 0.0).astype(BF16)
    l_r = lax.broadcasted_iota(jnp.int32, (V7X_LANES, V7X_LANES), 0)
    l_c = lax.broadcasted_iota(jnp.int32, (V7X_LANES, V7X_LANES), 1)
    upper = jnp.where(l_r < l_c, 1.0, 0.0).astype(BF16)

    n_t = ea_ref.shape[1]
    tile_start = lax.broadcasted_iota(jnp.int32, (1, n_t), 1).astype(F32) * float(TM)
    tile_cls = jnp.zeros((1, n_t), jnp.int32)
    off = jnp.zeros((1, 1), F32)
    pos = jnp.zeros((rows, V7X_LANES), F32)
    for c in range(N_CLASSES):
        m = jnp.where(cls == c, 1.0, 0.0)
        mb = m.astype(BF16)
        if rows_pad > rows:
            mb_pad = jnp.concatenate([mb, jnp.zeros((rows_pad - rows, V7X_LANES), BF16)], axis=0)
        else:
            mb_pad = mb
        before_rows = jnp.dot(lower, mb_pad, preferred_element_type=F32)[:rows]
        before_rows = before_rows.sum(axis=-1, keepdims=True)
        before_lanes = jnp.dot(mb, upper, preferred_element_type=F32)
        pos = pos + m * (off + before_rows + before_lanes)
        if c > 0:
            tile_cls = tile_cls + jnp.where(tile_start >= off, 1, 0)
        count = m.sum(axis=-1, keepdims=True).sum(axis=0, keepdims=True)
        pad_lo = jnp.where(lane == c, off + count, pad_lo)
        off = off + jnp.floor((count + float(TM - 1)) * (1.0 / TM)) * float(TM)
        pad_hi = jnp.where(lane == c, off, pad_hi)
    pos_ref[...] = pos.astype(jnp.int32)
    padlo_ref[...] = pad_lo.astype(jnp.int32)
    padhi_ref[...] = pad_hi.astype(jnp.int32)
    grp = (jnp.where(tile_cls >= 6, 1, 0) + jnp.where(tile_cls >= 12, 1, 0)
           + jnp.where(tile_cls >= 18, 1, 0))
    pair = tile_cls - PAIRS_PER_GROUP * grp
    lo = jnp.where(pair >= 3, 1, 0) + jnp.where(pair >= 5, 1, 0)
    hi = pair + 1 - 2 * jnp.where(pair >= 3, 1, 0) - jnp.where(pair >= 5, 1, 0)
    ea_ref[...] = EXPERTS_PER_GROUP * grp + lo
    eb_ref[...] = EXPERTS_PER_GROUP * grp + hi
    nu_ref[...] = jnp.broadcast_to(off * (1.0 / TM), nu_ref.shape).astype(jnp.int32)


def _slots(cls2d, *, n_slot_tiles):
    rows = cls2d.shape[0]
    rows_pad = -(-rows // V7X_LANES) * V7X_LANES
    n_t = -(-n_slot_tiles // V7X_LANES) * V7X_LANES
    i32 = jnp.int32
    return pl.pallas_call(
        functools.partial(_slots_kernel, rows_pad=rows_pad),
        out_shape=[jax.ShapeDtypeStruct(cls2d.shape, i32),
                   jax.ShapeDtypeStruct((1, n_t), i32), jax.ShapeDtypeStruct((1, n_t), i32),
                   jax.ShapeDtypeStruct((1, V7X_LANES), i32), jax.ShapeDtypeStruct((1, V7X_LANES), i32),
                   jax.ShapeDtypeStruct((1, V7X_LANES), i32)],
        compiler_params=pltpu.CompilerParams(vmem_limit_bytes=V7X_VMEM_LIMIT),
        name="moe_slots",
    )(cls2d)


def _row_copy(src, dst, src_row, dst_row, sem):
    return pltpu.make_async_copy(src.at[pl.ds(src_row, 1)], dst.at[pl.ds(dst_row, 1)], sem)


def _start_tile_rows(make_copy):
    for t in range(TM):
        make_copy(t).start()


def _dispatch_kernel(pos_ref, padlo_ref, padhi_ref, nu_ref, x_ref, xs_hbm, zbuf, stage, sem, sems,
                     *, n_slot_tiles, n_tiles):
    @pl.when(pl.program_id(0) == 0)
    def _():
        zbuf[...] = jnp.zeros(zbuf.shape, zbuf.dtype)

        def zero_tile(t, carry):
            cp = pltpu.make_async_copy(zbuf, xs_hbm.at[pl.ds(pl.multiple_of(t * TM, TM), TM)], sem)
            cp.start()
            cp.wait()
            return carry

        lax.fori_loop(nu_ref[0], n_slot_tiles, zero_tile, 0)

        for c in range(N_CLASSES):
            lo, hi = padlo_ref[c], padhi_ref[c]

            def zero_row(s, carry):
                _row_copy(zbuf, xs_hbm, 0, s, sem).start()
                return carry

            def zero_wait(s, carry):
                _row_copy(zbuf, xs_hbm, 0, 0, sem).wait()
                return carry

            lax.fori_loop(lo, hi, zero_row, 0)
            lax.fori_loop(lo, hi, zero_wait, 0)

    i = pl.program_id(0)
    slot = i % 2

    def wait(s):
        pltpu.make_async_copy(stage.at[s], xs_hbm.at[pl.ds(0, TM)], sems.at[s]).wait()

    for s in range(2):
        @pl.when(slot == s)
        def _():
            stage[s] = x_ref[...]
            _start_tile_rows(lambda t: _row_copy(stage.at[s], xs_hbm, t, pos_ref[0, 0, t], sems.at[s]))

    @pl.when(i > 0)
    def _():
        wait(1 - slot)

    @pl.when(i == n_tiles - 1)
    def _():
        wait(slot)


def _dispatch(pos3, pad_lo, pad_hi, n_used, x1g, *, n_slot_tiles):
    n_tiles = pos3.shape[0]
    width = x1g.shape[1]
    smem = pl.BlockSpec(memory_space=pltpu.SMEM)
    return pl.pallas_call(
        functools.partial(_dispatch_kernel, n_slot_tiles=n_slot_tiles, n_tiles=n_tiles),
        grid=(n_tiles,),
        in_specs=[pl.BlockSpec((1, 1, TM), lambda i: (i, 0, 0), memory_space=pltpu.SMEM),
                  smem, smem, smem,
                  pl.BlockSpec((TM, width), lambda i: (i, 0))],
        out_specs=pl.BlockSpec(memory_space=pl.ANY),
        out_shape=jax.ShapeDtypeStruct((n_slot_tiles * TM, width), x1g.dtype),
        scratch_shapes=[pltpu.VMEM((TM, width), x1g.dtype), pltpu.VMEM((2, TM, width), x1g.dtype),
                        pltpu.SemaphoreType.DMA(()), pltpu.SemaphoreType.DMA((2,))],
        compiler_params=pltpu.CompilerParams(dimension_semantics=("arbitrary",), has_side_effects=True,
                                             vmem_limit_bytes=V7X_VMEM_LIMIT),
        name="moe_dispatch",
    )(pos3, pad_lo, pad_hi, n_used, x1g)


def _moe_kernel(ea_ref, eb_ref, nu_ref, xs_ref, wga, wua, wda, wgb, wub, wdb, ys_ref, *, d):
    del ea_ref, eb_ref
    used = pl.program_id(0) < nu_ref[0]

    @pl.when(jnp.logical_not(used))
    def _():
        ys_ref[...] = jnp.zeros(ys_ref.shape, ys_ref.dtype)

    @pl.when(used)
    def _():
        x = xs_ref[:, :d].astype(BF16)
        gate_a = xs_ref[:, d:d + 1]
        gate_b = xs_ref[:, d + 1:d + 2]

        def expert(wg, wu, wd):
            g = jnp.dot(x, wg[0], preferred_element_type=F32)
            u = jnp.dot(x, wu[0], preferred_element_type=F32)
            h = (g * jax.nn.sigmoid(g)) * u
            return jnp.dot(h.astype(BF16), wd[0], preferred_element_type=F32)

        ys_ref[...] = gate_a * expert(wga, wua, wda) + gate_b * expert(wgb, wub, wdb)


def _moe(tile_ea, tile_eb, n_used, xs, w_gate, w_up, w_down):
    n_slots, width = xs.shape
    d = width - GATE_LANES
    n_tiles = n_slots // TM
    d_e = w_gate.shape[-1]

    def row_map(i, ea, eb, nu):
        return (jnp.minimum(i, nu[0] - 1), 0)

    def w_a(i, ea, eb, nu):
        return (ea[i], 0, 0)

    def w_b(i, ea, eb, nu):
        return (eb[i], 0, 0)

    grid_spec = pltpu.PrefetchScalarGridSpec(
        num_scalar_prefetch=3,
        grid=(n_tiles,),
        in_specs=[pl.BlockSpec((TM, width), row_map),
                  pl.BlockSpec((1, d, d_e), w_a), pl.BlockSpec((1, d, d_e), w_a),
                  pl.BlockSpec((1, d_e, d), w_a),
                  pl.BlockSpec((1, d, d_e), w_b), pl.BlockSpec((1, d, d_e), w_b),
                  pl.BlockSpec((1, d_e, d), w_b)],
        out_specs=pl.BlockSpec((TM, d), lambda i, ea, eb, nu: (i, 0)),
    )
    return pl.pallas_call(
        functools.partial(_moe_kernel, d=d),
        grid_spec=grid_spec,
        out_shape=jax.ShapeDtypeStruct((n_slots, d), F32),
        compiler_params=_cparams(("arbitrary",)),
        name="moe_experts",
    )(tile_ea, tile_eb, n_used, xs, w_gate, w_up, w_down, w_gate, w_up, w_down)


def _gather_ahead(pos_ref, posn_ref, ys_hbm, ybuf, sems, n_tiles):
    i = pl.program_id(0)
    slot = i % 2

    def wait(s):
        pltpu.make_async_copy(ys_hbm.at[pl.ds(0, TM)], ybuf.at[s], sems.at[s]).wait()

    @pl.when(i == 0)
    def _():
        def start(t, carry):
            _row_copy(ys_hbm, ybuf.at[0], pos_ref[0, 0, t], t, sems.at[0]).start()
            return carry
        lax.fori_loop(0, TM, start, 0)

    for s in range(2):
        @pl.when(slot == 1 - s)
        def _():
            _start_tile_rows(lambda t: _row_copy(ys_hbm, ybuf.at[s], posn_ref[0, 0, t], t, sems.at[s]))

    wait(slot)

    def drain():
        @pl.when(i == n_tiles - 1)
        def _():
            wait(1 - slot)

    return slot, drain


def _combine_final_kernel(pos_ref, posn_ref, ys_hbm, x1_ref, g_ref, b_ref, yp_ref, yt_ref, ybuf, sems,
                          *, alpha, d, n_tiles, n_prompt_tiles):
    slot, drain = _gather_ahead(pos_ref, posn_ref, ys_hbm, ybuf, sems, n_tiles)
    x2 = _layer_norm(alpha * x1_ref[:, :d] + ybuf[slot], g_ref[...], b_ref[...])
    is_prompt = pl.program_id(0) < n_prompt_tiles

    @pl.when(is_prompt)
    def _():
        yp_ref[...] = x2

    @pl.when(jnp.logical_not(is_prompt))
    def _():
        yt_ref[...] = x2

    drain()


def _combine_final(pos3, ys, x1g, g, b, *, alpha, n_prompt_tiles):
    t_pad, width = x1g.shape
    d = width - GATE_LANES
    n_tiles = t_pad // TM
    last_p = n_prompt_tiles - 1
    const = lambda i: (0, 0)
    return pl.pallas_call(
        functools.partial(_combine_final_kernel, alpha=alpha, d=d, n_tiles=n_tiles,
                          n_prompt_tiles=n_prompt_tiles),
        grid=(n_tiles,),
        in_specs=[pl.BlockSpec((1, 1, TM), lambda i: (i, 0, 0), memory_space=pltpu.SMEM),
                  pl.BlockSpec((1, 1, TM), lambda i: (jnp.minimum(i + 1, n_tiles - 1), 0, 0),
                               memory_space=pltpu.SMEM),
                  pl.BlockSpec(memory_space=pl.ANY),
                  pl.BlockSpec((TM, width), lambda i: (i, 0)),
                  pl.BlockSpec(g.shape, const), pl.BlockSpec(b.shape, const)],
        out_specs=[pl.BlockSpec((TM, d), lambda i: (jnp.minimum(i, last_p), 0)),
                   pl.BlockSpec((TM, d), lambda i: (jnp.maximum(i - n_prompt_tiles, 0), 0))],
        out_shape=[jax.ShapeDtypeStruct((n_prompt_tiles * TM, d), F32),
                   jax.ShapeDtypeStruct(((n_tiles - n_prompt_tiles) * TM, d), F32)],
        scratch_shapes=[pltpu.VMEM((2, TM, d), F32), pltpu.SemaphoreType.DMA((2,))],
        compiler_params=_cparams(("arbitrary",)),
        name="moe_combine_ln_final",
    )(pos3, pos3, ys, x1g, g, b)


def _combine_inproj_kernel(pos_ref, posn_ref, ys_hbm, x1_ref, g_ref, b_ref, w_ref,
                           x2_ref, h_ref, kv_ref, ybuf, sems,
                           *, alpha, d, n_tiles):
    slot, drain = _gather_ahead(pos_ref, posn_ref, ys_hbm, ybuf, sems, n_tiles)
    sub = TM // OUTPROJ_SUBTILES
    w = w_ref[...]
    for s in range(OUTPROJ_SUBTILES):
        rows = slice(s * sub, (s + 1) * sub)
        x2 = _layer_norm(alpha * x1_ref[rows, :d] + ybuf[slot, rows, :], g_ref[...], b_ref[...])
        x2_ref[rows, :] = x2
        acc = jnp.dot(x2.astype(BF16), w, preferred_element_type=F32)
        h_ref[rows, :] = acc.astype(BF16)
        kv_ref[rows, : 2 * WIDTH_A] = acc[:, COL_KA:COL_QB]
        kv_ref[rows, 2 * WIDTH_A:] = acc[:, COL_KB:]

    drain()


def _combine_inproj(pos3, ys, x1g, g, b, w_in, *, alpha, seq, n_prompt_tiles):
    t_pad, width = x1g.shape
    d = width - GATE_LANES
    n_tiles = t_pad // TM
    tiles_per_seq = seq // TM
    n_seq = n_prompt_tiles // tiles_per_seq
    n_kv_blocks = n_seq + (n_tiles - n_prompt_tiles)
    const = lambda i: (0, 0)

    def kv_map(i):
        return (jnp.where(i < n_prompt_tiles, i // tiles_per_seq, n_seq + i - n_prompt_tiles), 0)

    return pl.pallas_call(
        functools.partial(_combine_inproj_kernel, alpha=alpha, d=d, n_tiles=n_tiles),
        grid=(n_tiles,),
        in_specs=[pl.BlockSpec((1, 1, TM), lambda i: (i, 0, 0), memory_space=pltpu.SMEM),
                  pl.BlockSpec((1, 1, TM), lambda i: (jnp.minimum(i + 1, n_tiles - 1), 0, 0),
                               memory_space=pltpu.SMEM),
                  pl.BlockSpec(memory_space=pl.ANY),
                  pl.BlockSpec((TM, width), lambda i: (i, 0)),
                  pl.BlockSpec(g.shape, const), pl.BlockSpec(b.shape, const),
                  pl.BlockSpec(w_in.shape, const)],
        out_specs=[pl.BlockSpec((TM, d), lambda i: (i, 0)),
                   pl.BlockSpec((TM, D_IN), lambda i: (i, 0)),
                   pl.BlockSpec((TM, KV_OUT), kv_map)],
        out_shape=[jax.ShapeDtypeStruct((t_pad, d), F32),
                   jax.ShapeDtypeStruct((t_pad, D_IN), BF16),
                   jax.ShapeDtypeStruct((n_kv_blocks * TM, KV_OUT), F32)],
        scratch_shapes=[pltpu.VMEM((2, TM, d), F32), pltpu.SemaphoreType.DMA((2,))],
        compiler_params=_cparams(("arbitrary",)),
        name="moe_combine_ln_inproj",
    )(pos3, pos3, ys, x1g, g, b, w_in)


def _alibi_slopes():
    return 2.0 ** (-8.0 * jnp.arange(1, N_HEADS_B + 1, dtype=F32) / N_HEADS_B)


def _rel_bias_table(rel_bias, q0, n_q, n_k):
    n_diag = n_q + n_k - 1
    dist = q0 + (n_q - 1) - jnp.arange(n_diag, dtype=jnp.int32)
    diag = jnp.take(rel_bias.astype(F32), jnp.clip(dist, -REL_CLIP, REL_CLIP) + REL_CLIP, axis=1)
    n_h = diag.shape[0]
    ext = jnp.concatenate([diag, jnp.zeros((n_h, 1), F32)], axis=1)
    shifted = jnp.tile(ext, (1, n_q))[:, :n_q * n_diag].reshape(n_h, n_q, n_diag)
    return shifted[:, :, n_q - 1:n_q - 1 + n_k]


def _alibi_table(q0, n_q, n_k):
    dist = q0 + jnp.arange(n_q, dtype=jnp.int32)[:, None] - jnp.arange(n_k, dtype=jnp.int32)[None, :]
    return -_alibi_slopes()[:, None, None] * jnp.abs(dist).astype(F32)[None]


def kernel(x_prompt, x_sample, cache_a_k, cache_a_v, cache_b_k, cache_b_v, w_in, rel_bias, attn_sinks,
           gn_a, gn_b, w_out, ln1_g, ln1_b, w_router, b_router, w_gate, w_up, w_down, ln2_g, ln2_b):
    n_seq, seq, d = x_prompt.shape
    n_dec, dec_seq, _ = x_sample.shape
    depth = w_in.shape[0]
    rows_ca, rows_cb = cache_a_k.shape[2], cache_b_k.shape[2]
    alpha = (2.0 * depth) ** 0.25
    assert seq % TM == 0 and PAD_A == TM and n_dec * dec_seq <= TM and TM % dec_seq == 0
    assert dec_seq % 16 == 0 and seq >= PAD_A

    t_prompt = n_seq * seq
    t_real = t_prompt + n_dec * dec_seq
    t_pad = -(-t_real // (2 * TM)) * (2 * TM)
    n_prompt_tiles = t_prompt // TM
    n_slot_tiles = t_pad // TM + N_CLASSES

    x = x_prompt.reshape(t_prompt, d)
    x_tail = jnp.concatenate([x_sample.reshape(n_dec * dec_seq, d), jnp.zeros((t_pad - t_real, d), F32)], axis=0)

    w_in_b = w_in.astype(BF16)
    w_out_b = w_out.astype(BF16)
    w_gate_b, w_up_b, w_down_b = w_gate.astype(BF16), w_up.astype(BF16), w_down.astype(BF16)
    wr_t = w_router.astype(F32).T
    wr_hi = wr_t.astype(BF16)
    wr_lo = (wr_t - wr_hi.astype(F32)).astype(BF16)
    br = b_router.astype(F32).reshape(N_EXPERTS, 1)

    alibi_p = _alibi_table(PAD_B, CHUNK, BAND_B * CHUNK)
    alibi_s = _alibi_table(rows_cb, dec_seq, rows_cb + dec_seq)

    pa_k, pa_v, pb_k, pb_v, sa_k, sa_v, sb_k, sb_v = ([] for _ in range(8))
    h, kv = _inproj(x, x_tail, w_in_b[0], seq=seq, n_prompt_tiles=n_prompt_tiles)
    for l in range(depth):
        sinks = attn_sinks[l].astype(F32)
        gna = gn_a[l].astype(F32).reshape(1, WIDTH_A)
        gnb = gn_b[l].astype(F32).reshape(1, WIDTH_B)
        bias_pa = _rel_bias_table(rel_bias[l], PAD_A, CHUNK, BAND_A * CHUNK)
        a_prompt = _attn_prompt(h, sinks, bias_pa, alibi_p, gna, gnb, n_seq=n_seq, seq=seq)
        bias_sa = _rel_bias_table(rel_bias[l], rows_ca, dec_seq, rows_ca + dec_seq)
        bias_s = (bias_sa[:, :, :rows_ca], bias_sa[:, :, rows_ca:],
                  alibi_s[:, :, :rows_cb], alibi_s[:, :, rows_cb:])
        a_sample = _attn_sample(
            h, sinks,
            cache_a_k[l].reshape(n_dec, rows_ca, WIDTH_A), cache_a_v[l].reshape(n_dec, rows_ca, WIDTH_A),
            cache_b_k[l].reshape(n_dec, rows_cb, KV_WIDTH_B), cache_b_v[l].reshape(n_dec, rows_cb, KV_WIDTH_B),
            bias_s, gna, gnb, first_row=t_prompt, n_rows_out=t_pad - t_prompt, n_dec=n_dec, dec_seq=dec_seq)

        x1g, cls = _outproj(a_prompt, a_sample, x, x_tail, w_out_b[l],
                            ln1_g[l].astype(F32).reshape(1, d), ln1_b[l].astype(F32).reshape(1, d),
                            wr_hi, wr_lo, br, n_tiles=t_pad // TM, n_prompt_tiles=n_prompt_tiles, alpha=alpha)
        pos, tile_ea, tile_eb, n_used, pad_lo, pad_hi = _slots(
            cls.reshape(t_pad // V7X_LANES, V7X_LANES), n_slot_tiles=n_slot_tiles)
        pos3 = pos.reshape(t_pad // TM, 1, TM)
        xs = _dispatch(pos3, pad_lo[0], pad_hi[0], n_used[0], x1g, n_slot_tiles=n_slot_tiles)
        ys = _moe(tile_ea[0, :n_slot_tiles], tile_eb[0, :n_slot_tiles], n_used[0, :1],
                  xs, w_gate_b[l], w_up_b[l], w_down_b[l])
        kv_l = kv
        g2, b2 = ln2_g[l].astype(F32).reshape(1, d), ln2_b[l].astype(F32).reshape(1, d)
        if l + 1 < depth:
            x, h, kv = _combine_inproj(pos3, ys, x1g, g2, b2, w_in_b[l + 1], alpha=alpha, seq=seq,
                                       n_prompt_tiles=n_prompt_tiles)
        else:
            y_p, y_t = _combine_final(pos3, ys, x1g, g2, b2, alpha=alpha, n_prompt_tiles=n_prompt_tiles)

        kv_p = kv_l[:n_seq * TM].reshape(n_seq, TM, KV_OUT)
        kv_s = kv_l[n_seq * TM:n_seq * TM + n_dec * dec_seq].reshape(n_dec, dec_seq, KV_OUT)
        ra, rb = min(PAD_A, seq), min(WINDOW_B, seq)
        pa_k.append(kv_p[:, TM - ra:, :WIDTH_A].reshape(n_seq, ra, N_HEADS_A, HEAD_DIM))
        pa_v.append(kv_p[:, TM - ra:, WIDTH_A:2 * WIDTH_A].reshape(n_seq, ra, N_HEADS_A, HEAD_DIM))
        pb_k.append(kv_p[:, TM - rb:, 2 * WIDTH_A:2 * WIDTH_A + KV_WIDTH_B].reshape(n_seq, rb, N_KV_B, HEAD_DIM))
        pb_v.append(kv_p[:, TM - rb:, 2 * WIDTH_A + KV_WIDTH_B:].reshape(n_seq, rb, N_KV_B, HEAD_DIM))
        sa_k.append(kv_s[:, :, :WIDTH_A].reshape(n_dec, dec_seq, N_HEADS_A, HEAD_DIM))
        sa_v.append(kv_s[:, :, WIDTH_A:2 * WIDTH_A].reshape(n_dec, dec_seq, N_HEADS_A, HEAD_DIM))
        sb_k.append(kv_s[:, :, 2 * WIDTH_A:2 * WIDTH_A + KV_WIDTH_B].reshape(n_dec, dec_seq, N_KV_B, HEAD_DIM))
        sb_v.append(kv_s[:, :, 2 * WIDTH_A + KV_WIDTH_B:].reshape(n_dec, dec_seq, N_KV_B, HEAD_DIM))

    y_prompt = y_p.reshape(n_seq, seq, d)
    y_sample = y_t[:n_dec * dec_seq].reshape(n_dec, dec_seq, d)
    return (y_prompt, y_sample,
            jnp.stack(pa_k), jnp.stack(pa_v), jnp.stack(pb_k), jnp.stack(pb_v),
            jnp.stack(sa_k), jnp.stack(sa_v), jnp.stack(sb_k), jnp.stack(sb_v))
```

```python
import functools

import jax
import jax.numpy as jnp
import numpy as np
from jax import lax
from jax.experimental import pallas as pl
from jax.experimental.pallas import tpu as pltpu

CHUNK = 64
HEAD_DIM = 64
N_HEADS_A = 8
BAND_A = 9
REL_CLIP = 128
N_HEADS_B = 8
N_KV_B = 2
WINDOW_B = 128
BAND_B = 1 + WINDOW_B // CHUNK
N_EXPERTS = 16
N_GROUPS = 4
EXPERTS_PER_GROUP = N_EXPERTS // N_GROUPS
PAIRS_PER_GROUP = 6
N_CLASSES = N_GROUPS * PAIRS_PER_GROUP
LN_EPS = 1e-5
RMS_EPS = 1e-6

WIDTH_A = N_HEADS_A * HEAD_DIM
WIDTH_B = N_HEADS_B * HEAD_DIM
KV_WIDTH_B = N_KV_B * HEAD_DIM
D_IN = 3 * WIDTH_A + WIDTH_B + 2 * KV_WIDTH_B
COL_QA, COL_KA, COL_VA = 0, WIDTH_A, 2 * WIDTH_A
COL_QB = 3 * WIDTH_A
COL_KB = COL_QB + WIDTH_B
COL_VB = COL_KB + KV_WIDTH_B
KV_OUT = 2 * WIDTH_A + 2 * KV_WIDTH_B
PAD_A = (BAND_A - 1) * CHUNK
PAD_B = WINDOW_B

V7X_LANES = 128
TM = 512
GATE_LANES = V7X_LANES
V7X_VMEM_LIMIT = 56 * 1024 * 1024

F32 = jnp.float32
BF16 = jnp.bfloat16
NEG_INF = float("-inf")


def _cparams(sem):
    return pltpu.CompilerParams(dimension_semantics=sem, vmem_limit_bytes=V7X_VMEM_LIMIT)


def _inproj_kernel(xp_ref, xt_ref, w_ref, h_ref, kv_ref, *, tiles_per_seq, n_prompt_tiles):
    i = pl.program_id(0)
    x = jnp.where(i < n_prompt_tiles, xp_ref[...], xt_ref[...])
    acc = jnp.dot(x.astype(BF16), w_ref[...], preferred_element_type=F32)
    h_ref[...] = acc.astype(BF16)
    keeps_kv = jnp.logical_or(i % tiles_per_seq == tiles_per_seq - 1, i >= n_prompt_tiles)

    @pl.when(keeps_kv)
    def _():
        kv_ref[:, : 2 * WIDTH_A] = acc[:, COL_KA:COL_QB]
        kv_ref[:, 2 * WIDTH_A:] = acc[:, COL_KB:]


def _inproj(x_prompt, x_tail, w_in, *, seq, n_prompt_tiles):
    d = x_prompt.shape[1]
    t_pad = x_prompt.shape[0] + x_tail.shape[0]
    n_tiles = t_pad // TM
    last_p = n_prompt_tiles - 1
    tiles_per_seq = seq // TM
    n_seq = n_prompt_tiles // tiles_per_seq
    n_kv_blocks = n_seq + (n_tiles - n_prompt_tiles)

    def kv_map(i):
        return (jnp.where(i < n_prompt_tiles, i // tiles_per_seq, n_seq + i - n_prompt_tiles), 0)

    return pl.pallas_call(
        functools.partial(_inproj_kernel, tiles_per_seq=tiles_per_seq, n_prompt_tiles=n_prompt_tiles),
        grid=(n_tiles,),
        in_specs=[pl.BlockSpec((TM, d), lambda i: (jnp.minimum(i, last_p), 0)),
                  pl.BlockSpec((TM, d), lambda i: (jnp.maximum(i - n_prompt_tiles, 0), 0)),
                  pl.BlockSpec((d, D_IN), lambda i: (0, 0))],
        out_specs=[pl.BlockSpec((TM, D_IN), lambda i: (i, 0)),
                   pl.BlockSpec((TM, KV_OUT), kv_map)],
        out_shape=[jax.ShapeDtypeStruct((t_pad, D_IN), BF16),
                   jax.ShapeDtypeStruct((n_kv_blocks * TM, KV_OUT), F32)],
        compiler_params=_cparams(("arbitrary",)),
        name="inproj",
    )(x_prompt, x_tail, w_in)


def _softmax_pv(s_parts, v_parts, sink=None):
    m = s_parts[0].max(axis=-1, keepdims=True)
    for s in s_parts[1:]:
        m = jnp.maximum(m, s.max(axis=-1, keepdims=True))
    if sink is not None:
        m = jnp.maximum(m, sink)
    denom = None
    acc = None
    for s, v in zip(s_parts, v_parts):
        e = jnp.exp(s - m)
        d = e.sum(axis=-1, keepdims=True)
        pv = jnp.dot(e.astype(BF16), v, preferred_element_type=F32)
        denom = d if denom is None else denom + d
        acc = pv if acc is None else acc + pv
    if sink is not None:
        denom = denom + jnp.exp(sink - m)
    return acc / denom


def _nt_dot(a, b):
    return lax.dot_general(a, b, (((1,), (1,)), ((), ())), preferred_element_type=F32)


def _head_pair_lane_mask(rows):
    return lax.broadcasted_iota(jnp.int32, (rows, 2 * HEAD_DIM), 1) < HEAD_DIM


def _rms_store(o_ref, row_slice, col0, pairs, gn_ref):
    ssq = None
    for o in pairs:
        s = (o * o).sum(axis=-1, keepdims=True)
        ssq = s if ssq is None else ssq + s
    width = len(pairs) * 2 * HEAD_DIM
    inv = lax.rsqrt(ssq / width + RMS_EPS)
    for p, o in enumerate(pairs):
        c0 = p * 2 * HEAD_DIM
        g = gn_ref[:, c0:c0 + 2 * HEAD_DIM]
        o_ref[row_slice, col0 + c0:col0 + c0 + 2 * HEAD_DIM] = (o * inv * g).astype(o_ref.dtype)


KEY_BLOCK = 2 * V7X_LANES


def _swap_halves(x):
    return jnp.concatenate([x[:, HEAD_DIM:], x[:, :HEAD_DIM]], axis=1)


def _pair_rows(q2, lane_lo, kv_half=None):
    hi_lanes = jnp.logical_not(lane_lo)
    zero = jnp.zeros_like(q2)
    if kv_half is None:
        lo, hi = jnp.where(lane_lo, q2, zero), jnp.where(hi_lanes, q2, zero)
    elif kv_half == 0:
        lo, hi = jnp.where(lane_lo, q2, zero), jnp.where(lane_lo, _swap_halves(q2), zero)
    else:
        lo, hi = jnp.where(hi_lanes, _swap_halves(q2), zero), jnp.where(hi_lanes, q2, zero)
    return jnp.concatenate([lo, hi], axis=0)


def _tn_dot(a, b):
    return lax.dot_general(a, b, (((0,), (0,)), ((), ())), preferred_element_type=F32)


PV_KEYS = 256


def _score_blocks(k_ref, lanes, row0, n_keys, w):
    out = []
    for off in range(0, n_keys, KEY_BLOCK):
        n = min(KEY_BLOCK, n_keys - off)
        start = pl.multiple_of(row0 + off, CHUNK)
        out.append((_nt_dot(k_ref[pl.ds(start, n), lanes], w), start, n, off))
    return out


def _pair_probs(score_blocks, lane_tile, bias_ref, pair, pad, masked, sink_block=None):
    lanes = slice(lane_tile * V7X_LANES, (lane_tile + 1) * V7X_LANES)
    blocks = []
    for s_all, start, n, off in score_blocks:
        s = s_all[:, lanes] + bias_ref[pair, off:off + n, :]
        if masked:
            key_row = lax.broadcasted_iota(jnp.int32, s.shape, 0) + start
            s = jnp.where(key_row >= pad, s, NEG_INF)
        blocks.append((s, start, n))
    m = None
    for s, _, _ in blocks:
        bm = s.max(axis=0, keepdims=True)
        m = bm if m is None else jnp.maximum(m, bm)
    if sink_block is not None:
        m = jnp.maximum(m, sink_block.max(axis=0, keepdims=True))
    probs = [(jnp.exp(s - m).astype(BF16), start, n) for s, start, n in blocks]
    sink_p = None if sink_block is None else jnp.exp(sink_block - m).astype(BF16)
    return probs, sink_p


def _pair_values(probs, sink_p, v_ref, lanes):
    groups, cur, cur_rows = [], [], 0
    for blk in probs:
        if cur and cur_rows + blk[2] > PV_KEYS:
            groups.append(cur)
            cur, cur_rows = [], 0
        cur.append(blk)
        cur_rows += blk[2]
    groups.append(cur)
    acc = None
    for gi, grp in enumerate(groups):
        rows = sum(n for _, _, n in grp)
        p = grp[0][0] if len(grp) == 1 else jnp.concatenate([b[0] for b in grp], axis=0)
        v1 = jnp.concatenate([v_ref[pl.ds(grp[0][1], rows), lanes], jnp.ones((rows, V7X_LANES), BF16)], axis=1)
        if sink_p is not None and gi == len(groups) - 1:
            n = sink_p.shape[0]
            p = jnp.concatenate([p, sink_p], axis=0)
            sink_v = jnp.concatenate([jnp.zeros((n, V7X_LANES), BF16), jnp.ones((n, V7X_LANES), BF16)], axis=1)
            v1 = jnp.concatenate([v1, sink_v], axis=0)
        c = _tn_dot(p, v1)
        acc = c if acc is None else acc + c
    return acc[:, :V7X_LANES] / acc[:, V7X_LANES:]


def _quad_rows(q4):
    head_of_lane = lax.broadcasted_iota(jnp.int32, q4.shape, 1) // HEAD_DIM
    zero = jnp.zeros_like(q4)
    return jnp.concatenate([jnp.where(head_of_lane == h, q4, zero) for h in range(4)], axis=0)


def _attn_prompt_kernel(qa_ref, ka_ref, va_ref, qb_ref, kb_ref, vb_ref,
                        ba_ref, bb_ref, sink_ref, gna_ref, gnb_ref, o_ref,
                        kpa, vpa, kpb, vpb, pa_even, pb_even, pa_odd, pb_odd, *, seq, tq):
    j = pl.program_id(1)

    @pl.when(j == 0)
    def _():
        kpa[0:PAD_A, :] = jnp.zeros((PAD_A, WIDTH_A), BF16)
        vpa[0:PAD_A, :] = jnp.zeros((PAD_A, WIDTH_A), BF16)
        kpb[0:PAD_B, :] = jnp.zeros((PAD_B, KV_WIDTH_B), BF16)
        vpb[0:PAD_B, :] = jnp.zeros((PAD_B, KV_WIDTH_B), BF16)
        kpa[PAD_A:PAD_A + seq, :] = ka_ref[...]
        vpa[PAD_A:PAD_A + seq, :] = va_ref[...]
        kpb[PAD_B:PAD_B + seq, :] = kb_ref[...]
        vpb[PAD_B:PAD_B + seq, :] = vb_ref[...]

    lane_lo = _head_pair_lane_mask(CHUNK)
    scale = HEAD_DIM ** -0.5
    all_lanes = slice(0, KV_WIDTH_B)

    n_pairs_a, n_pairs_b = N_HEADS_A // 2, N_HEADS_B // 2
    band_a, band_b = BAND_A * CHUNK, BAND_B * CHUNK
    n_sink = sink_ref.shape[1]
    even_bufs, odd_bufs = (pa_even, pb_even), (pa_odd, pb_odd)

    def band_start(c):
        return pl.multiple_of(j * tq + c * CHUNK, CHUNK)

    def probs_phase(c, masked, bufs):
        pa_buf, pb_buf = bufs
        rows = pl.ds(pl.multiple_of(c * CHUNK, CHUNK), CHUNK)
        cs = band_start(c)
        scores_a = []
        for quad in range(N_HEADS_A // 4):
            lanes = slice(quad * 4 * HEAD_DIM, (quad + 1) * 4 * HEAD_DIM)
            w = _quad_rows(qa_ref[rows, lanes] * scale)
            scores_a.append(_score_blocks(kpa, lanes, cs, band_a, w))
        w = jnp.concatenate(
            [_pair_rows(qb_ref[rows, p * 2 * HEAD_DIM:(p + 1) * 2 * HEAD_DIM] * scale, lane_lo,
                        kv_half=(2 * p) // (N_HEADS_B // N_KV_B))
             for p in range(n_pairs_b)], axis=0)
        scores_b = _score_blocks(kpb, all_lanes, cs, band_b, w)
        for p in range(n_pairs_a):
            probs, _ = _pair_probs(scores_a[p // 2], p % 2, ba_ref, p, PAD_A, masked)
            for blk, (_, _, n, off) in zip(probs, scores_a[p // 2]):
                pa_buf[p, off:off + n, :] = blk[0]
        for p in range(n_pairs_b):
            probs, sink_p = _pair_probs(scores_b, p, bb_ref, p, PAD_B, masked, sink_block=sink_ref[p])
            for blk, (_, _, n, off) in zip(probs, scores_b):
                pb_buf[p, off:off + n, :] = blk[0]
            pb_buf[p, band_b:band_b + n_sink, :] = sink_p

    def values_phase(c, bufs):
        pa_buf, pb_buf = bufs
        rows = pl.ds(pl.multiple_of(c * CHUNK, CHUNK), CHUNK)
        cs = band_start(c)

        def groups(buf, p, n_keys):
            return [(buf[p, off:off + min(PV_KEYS, n_keys - off), :], pl.multiple_of(cs + off, CHUNK),
                     min(PV_KEYS, n_keys - off)) for off in range(0, n_keys, PV_KEYS)]

        pairs_a = []
        for p in range(n_pairs_a):
            lanes = slice(p * 2 * HEAD_DIM, (p + 1) * 2 * HEAD_DIM)
            o = _pair_values(groups(pa_buf, p, band_a), None, vpa, lanes)
            pairs_a.append(jnp.where(lane_lo, o[:CHUNK], o[CHUNK:]))
        _rms_store(o_ref, rows, 0, pairs_a, gna_ref)

        pairs_b = []
        for p in range(n_pairs_b):
            kv_half = (2 * p) // (N_HEADS_B // N_KV_B)
            o = _pair_values(groups(pb_buf, p, band_b), pb_buf[p, band_b:band_b + n_sink, :], vpb, all_lanes)
            lo, hi = o[:CHUNK], o[CHUNK:]
            if kv_half == 0:
                hi = _swap_halves(hi)
            else:
                lo = _swap_halves(lo)
            pairs_b.append(jnp.where(lane_lo, lo, hi))
        _rms_store(o_ref, rows, WIDTH_A, pairs_b, gnb_ref)

    def run(masked):
        n_chunks = tq // CHUNK
        bufs = (even_bufs, odd_bufs)
        probs_phase(0, masked, even_bufs)
        for c in range(1, n_chunks):
            probs_phase(c, masked, bufs[c % 2])
            values_phase(c - 1, bufs[(c - 1) % 2])
        values_phase(n_chunks - 1, bufs[(n_chunks - 1) % 2])

    @pl.when(j == 0)
    def _():
        run(True)

    @pl.when(j > 0)
    def _():
        run(False)


def _pair_transposed(table):
    n_h, n_q, n_k = table.shape
    return table.reshape(n_h // 2, 2, n_q, n_k).transpose(0, 3, 1, 2).reshape(n_h // 2, n_k, 2 * n_q)


def _sink_blocks(sinks):
    row = jnp.repeat(sinks.astype(F32).reshape(N_HEADS_B // 2, 2), HEAD_DIM, axis=1)
    rest = jnp.full((N_HEADS_B // 2, 15, 2 * HEAD_DIM), NEG_INF, F32)
    return jnp.concatenate([row[:, None, :], rest], axis=1)


def _attn_prompt(h, sinks, bias_a, bias_b, gn_a, gn_b, *, n_seq, seq):
    tq = TM
    assert tq >= PAD_A and tq >= PAD_B and tq % CHUNK == 0
    n_q = seq // tq
    vmem = pltpu.VMEM
    bias_a, bias_b = _pair_transposed(bias_a), _pair_transposed(bias_b)
    sinks = _sink_blocks(sinks)
    return pl.pallas_call(
        functools.partial(_attn_prompt_kernel, seq=seq, tq=tq),
        grid=(n_seq, n_q),
        in_specs=[
            pl.BlockSpec((tq, WIDTH_A), lambda b, j: (b * n_q + j, COL_QA // WIDTH_A)),
            pl.BlockSpec((seq, WIDTH_A), lambda b, j: (b, COL_KA // WIDTH_A)),
            pl.BlockSpec((seq, WIDTH_A), lambda b, j: (b, COL_VA // WIDTH_A)),
            pl.BlockSpec((tq, WIDTH_B), lambda b, j: (b * n_q + j, COL_QB // WIDTH_B)),
            pl.BlockSpec((seq, KV_WIDTH_B), lambda b, j: (b, COL_KB // KV_WIDTH_B)),
            pl.BlockSpec((seq, KV_WIDTH_B), lambda b, j: (b, COL_VB // KV_WIDTH_B)),
            pl.BlockSpec(bias_a.shape, lambda b, j: (0, 0, 0)),
            pl.BlockSpec(bias_b.shape, lambda b, j: (0, 0, 0)),
            pl.BlockSpec(sinks.shape, lambda b, j: (0, 0, 0)),
            pl.BlockSpec(gn_a.shape, lambda b, j: (0, 0)),
            pl.BlockSpec(gn_b.shape, lambda b, j: (0, 0)),
        ],
        out_specs=pl.BlockSpec((tq, WIDTH_A + WIDTH_B), lambda b, j: (b * n_q + j, 0)),
        out_shape=jax.ShapeDtypeStruct((n_seq * seq, WIDTH_A + WIDTH_B), BF16),
        scratch_shapes=[vmem((PAD_A + seq, WIDTH_A), BF16), vmem((PAD_A + seq, WIDTH_A), BF16),
                        vmem((PAD_B + seq, KV_WIDTH_B), BF16), vmem((PAD_B + seq, KV_WIDTH_B), BF16),
                        vmem((N_HEADS_A // 2, BAND_A * CHUNK, V7X_LANES), BF16),
                        vmem((N_HEADS_B // 2, BAND_B * CHUNK + sinks.shape[1], V7X_LANES), BF16),
                        vmem((N_HEADS_A // 2, BAND_A * CHUNK, V7X_LANES), BF16),
                        vmem((N_HEADS_B // 2, BAND_B * CHUNK + sinks.shape[1], V7X_LANES), BF16)],
        compiler_params=_cparams(("arbitrary", "arbitrary")),
        name="attn_prompt",
    )(h, h, h, h, h, h, bias_a, bias_b, sinks, gn_a, gn_b)


def _attn_sample_kernel(sink_ref, qa_ref, ka_ref, va_ref, qb_ref, kb_ref, vb_ref,
                        cka_ref, cva_ref, ckb_ref, cvb_ref,
                        bac_ref, ban_ref, bbc_ref, bbn_ref, gna_ref, gnb_ref, o_ref, *, n_dec, dec_seq):
    b = pl.program_id(0)

    @pl.when(b >= n_dec)
    def _():
        o_ref[...] = jnp.zeros(o_ref.shape, o_ref.dtype)

    @pl.when(b < n_dec)
    def _():
        lane_lo = _head_pair_lane_mask(dec_seq)
        scale = HEAD_DIM ** -0.5
        rows = slice(0, dec_seq)

        pairs_a = []
        for p in range(N_HEADS_A // 2):
            lanes = slice(p * 2 * HEAD_DIM, (p + 1) * 2 * HEAD_DIM)
            q2 = qa_ref[:, lanes] * scale
            kc = cka_ref[0, :, lanes].astype(BF16)
            vc = cva_ref[0, :, lanes].astype(BF16)
            kn = ka_ref[:, lanes]
            vn = va_ref[:, lanes]
            halves = []
            for half in range(2):
                h = 2 * p + half
                keep = lane_lo if half == 0 else jnp.logical_not(lane_lo)
                qm = jnp.where(keep, q2, jnp.zeros_like(q2))
                s_c = _nt_dot(qm, kc) + bac_ref[h]
                s_n = _nt_dot(qm, kn) + ban_ref[h]
                halves.append(_softmax_pv([s_c, s_n], [vc, vn]))
            pairs_a.append(jnp.where(lane_lo, halves[0], halves[1]))
        _rms_store(o_ref, rows, 0, pairs_a, gna_ref)

        kc = ckb_ref[0].astype(BF16)
        vc = cvb_ref[0].astype(BF16)
        kn = kb_ref[...]
        vn = vb_ref[...]
        pairs_b = []
        for p in range(N_HEADS_B // 2):
            lanes = slice(p * 2 * HEAD_DIM, (p + 1) * 2 * HEAD_DIM)
            q2 = qb_ref[:, lanes] * scale
            halves = []
            for half in range(2):
                h = 2 * p + half
                kv_head = h // (N_HEADS_B // N_KV_B)
                keep = lane_lo if half == 0 else jnp.logical_not(lane_lo)
                qm = jnp.where(keep, q2, jnp.zeros_like(q2))
                if kv_head != half:
                    qm = jnp.concatenate([qm[:, HEAD_DIM:], qm[:, :HEAD_DIM]], axis=1)
                s_c = _nt_dot(qm, kc) + bbc_ref[h]
                s_n = _nt_dot(qm, kn) + bbn_ref[h]
                o = _softmax_pv([s_c, s_n], [vc, vn], sink=sink_ref[h])
                if kv_head != half:
                    o = jnp.concatenate([o[:, HEAD_DIM:], o[:, :HEAD_DIM]], axis=1)
                halves.append(o)
            pairs_b.append(jnp.where(lane_lo, halves[0], halves[1]))
        _rms_store(o_ref, rows, WIDTH_A, pairs_b, gnb_ref)


def _attn_sample(h, sinks, cka, cva, ckb, cvb, bias, gn_a, gn_b, *, first_row, n_rows_out, n_dec, dec_seq):
    bac, ban, bbc, bbn = bias
    n_steps = n_rows_out // dec_seq
    rb = first_row // dec_seq
    last = n_dec - 1

    def hrow(b):
        return rb + jnp.minimum(b, last)

    def cache_spec(c):
        return pl.BlockSpec((1,) + c.shape[1:], lambda b: (jnp.minimum(b, last), 0, 0))

    def const_spec(a):
        return pl.BlockSpec(a.shape, lambda b: (0,) * a.ndim)

    return pl.pallas_call(
        functools.partial(_attn_sample_kernel, n_dec=n_dec, dec_seq=dec_seq),
        grid=(n_steps,),
        in_specs=[
            pl.BlockSpec(memory_space=pltpu.SMEM),
            pl.BlockSpec((dec_seq, WIDTH_A), lambda b: (hrow(b), COL_QA // WIDTH_A)),
            pl.BlockSpec((dec_seq, WIDTH_A), lambda b: (hrow(b), COL_KA // WIDTH_A)),
            pl.BlockSpec((dec_seq, WIDTH_A), lambda b: (hrow(b), COL_VA // WIDTH_A)),
            pl.BlockSpec((dec_seq, WIDTH_B), lambda b: (hrow(b), COL_QB // WIDTH_B)),
            pl.BlockSpec((dec_seq, KV_WIDTH_B), lambda b: (hrow(b), COL_KB // KV_WIDTH_B)),
            pl.BlockSpec((dec_seq, KV_WIDTH_B), lambda b: (hrow(b), COL_VB // KV_WIDTH_B)),
            cache_spec(cka), cache_spec(cva), cache_spec(ckb), cache_spec(cvb),
            const_spec(bac), const_spec(ban), const_spec(bbc), const_spec(bbn),
            const_spec(gn_a), const_spec(gn_b),
        ],
        out_specs=pl.BlockSpec((dec_seq, WIDTH_A + WIDTH_B), lambda b: (b, 0)),
        out_shape=jax.ShapeDtypeStruct((n_rows_out, WIDTH_A + WIDTH_B), BF16),
        compiler_params=_cparams(("arbitrary",)),
        name="attn_sample",
    )(sinks, h, h, h, h, h, h, cka, cva, ckb, cvb, bac, ban, bbc, bbn, gn_a, gn_b)


def _layer_norm(r, g, b):
    mu = r.mean(axis=-1, keepdims=True)
    c = r - mu
    var = (c * c).mean(axis=-1, keepdims=True)
    return c * lax.rsqrt(var + LN_EPS) * g + b


def _route_rows(logits):
    rows = [logits[e:e + 1, :] for e in range(N_EXPERTS)]

    def top2_sum(a, b, c, d):
        hi1, lo1 = jnp.maximum(a, b), jnp.minimum(a, b)
        hi2, lo2 = jnp.maximum(c, d), jnp.minimum(c, d)
        return jnp.maximum(hi1, hi2) + jnp.maximum(jnp.minimum(hi1, hi2), jnp.maximum(lo1, lo2))

    scores = [top2_sum(*rows[4 * g:4 * g + 4]) for g in range(N_GROUPS)]
    best = scores[0]
    g_sel = jnp.zeros(best.shape, jnp.int32)
    for g in range(1, N_GROUPS):
        upd = scores[g] > best
        best = jnp.where(upd, scores[g], best)
        g_sel = jnp.where(upd, g, g_sel)

    vals = []
    for k in range(EXPERTS_PER_GROUP):
        v = rows[k]
        for g in range(1, N_GROUPS):
            v = jnp.where(g_sel == g, rows[4 * g + k], v)
        vals.append(v)

    v1 = vals[0]
    i1 = jnp.zeros(v1.shape, jnp.int32)
    for k in range(1, EXPERTS_PER_GROUP):
        upd = vals[k] > v1
        v1 = jnp.where(upd, vals[k], v1)
        i1 = jnp.where(upd, k, i1)
    v2 = jnp.full(v1.shape, NEG_INF, F32)
    i2 = jnp.zeros(v1.shape, jnp.int32)
    for k in range(EXPERTS_PER_GROUP):
        upd = jnp.logical_and(i1 != k, vals[k] > v2)
        v2 = jnp.where(upd, vals[k], v2)
        i2 = jnp.where(upd, k, i2)

    e = jnp.exp(v2 - v1)
    den = 1.0 + e
    w1 = 1.0 / den
    w2 = e / den
    first_is_lo = i1 < i2
    lo = jnp.where(first_is_lo, i1, i2)
    hi = jnp.where(first_is_lo, i2, i1)
    gate_lo = jnp.where(first_is_lo, w1, w2)
    gate_hi = jnp.where(first_is_lo, w2, w1)
    pair_base = jnp.where(lo == 0, 0, jnp.where(lo == 1, 3, 5))
    cls = g_sel * PAIRS_PER_GROUP + pair_base + hi - lo - 1
    return cls, gate_lo, gate_hi


OUTPROJ_SUBTILES = 4


def _outproj_kernel(ap_ref, as_ref, xm_ref, xt_ref, w_ref, g_ref, b_ref, wrh_ref, wrl_ref, br_ref,
                    x1_ref, cls_ref, *, n_prompt_tiles, n_main_tiles, alpha, d):
    i = pl.program_id(0)
    sub = TM // OUTPROJ_SUBTILES
    w = w_ref[...]
    ys = []
    for s in range(OUTPROJ_SUBTILES):
        rows = slice(s * sub, (s + 1) * sub)
        a = jnp.where(i < n_prompt_tiles, ap_ref[rows, :], as_ref[rows, :])
        ys.append(jnp.dot(a, w, preferred_element_type=F32))
    for s in range(OUTPROJ_SUBTILES):
        rows = slice(s * sub, (s + 1) * sub)
        x = jnp.where(i < n_main_tiles, xm_ref[rows, :], xt_ref[rows, :])
        x1 = _layer_norm(alpha * x + ys[s], g_ref[...], b_ref[...])
        x1_ref[rows, :d] = x1
        x_hi = x1.astype(BF16)
        x_lo = (x1 - x_hi.astype(F32)).astype(BF16)
        logits = (_nt_dot(wrh_ref[...], x_hi) + _nt_dot(wrh_ref[...], x_lo) + _nt_dot(wrl_ref[...], x_hi)
                  + br_ref[...])
        cls, gate_lo, gate_hi = _route_rows(logits)
        cls_ref[:, rows] = cls
        gates = jnp.concatenate([gate_lo, gate_hi, jnp.zeros((GATE_LANES - 2, sub), F32)], axis=0)
        x1_ref[rows, d:] = gates.T


def _outproj(a_prompt, a_sample, x_main, x_tail, w_out, g, b, wr_hi, wr_lo, br, *, n_tiles, n_prompt_tiles, alpha):
    d = x_main.shape[1]
    t_pad = n_tiles * TM
    n_main_tiles = x_main.shape[0] // TM
    last_p = n_prompt_tiles - 1
    const = lambda i: (0, 0)
    return pl.pallas_call(
        functools.partial(_outproj_kernel, n_prompt_tiles=n_prompt_tiles, n_main_tiles=n_main_tiles,
                          alpha=alpha, d=d),
        grid=(n_tiles,),
        in_specs=[pl.BlockSpec((TM, d), lambda i: (jnp.minimum(i, last_p), 0)),
                  pl.BlockSpec((TM, d), lambda i: (jnp.maximum(i - n_prompt_tiles, 0), 0)),
                  pl.BlockSpec((TM, d), lambda i: (jnp.minimum(i, n_main_tiles - 1), 0)),
                  pl.BlockSpec((TM, d), lambda i: (jnp.maximum(i - n_main_tiles, 0), 0)),
                  pl.BlockSpec(w_out.shape, const),
                  pl.BlockSpec(g.shape, const), pl.BlockSpec(b.shape, const),
                  pl.BlockSpec(wr_hi.shape, const), pl.BlockSpec(wr_lo.shape, const),
                  pl.BlockSpec(br.shape, const)],
        out_specs=[pl.BlockSpec((TM, d + GATE_LANES), lambda i: (i, 0)),
                   pl.BlockSpec((1, TM), lambda i: (0, i))],
        out_shape=[jax.ShapeDtypeStruct((t_pad, d + GATE_LANES), F32),
                   jax.ShapeDtypeStruct((1, t_pad), jnp.int32)],
        compiler_params=_cparams(("arbitrary",)),
        name="outproj_ln_route",
    )(a_prompt, a_sample, x_main, x_tail, w_out, g, b, wr_hi, wr_lo, br)


def _slots_kernel(cls_ref, pos_ref, ea_ref, eb_ref, nu_ref, padlo_ref, padhi_ref, *, rows_pad):
    lane = lax.broadcasted_iota(jnp.int32, (1, V7X_LANES), 1)
    pad_lo = jnp.zeros((1, V7X_LANES), F32)
    pad_hi = jnp.zeros((1, V7X_LANES), F32)
    cls = cls_ref[...]
    rows = cls.shape[0]
    r_i = lax.broadcasted_iota(jnp.int32, (rows_pad, rows_pad), 0)
    c_i = lax.broadcasted_iota(jnp.int32, (rows_pad, rows_pad), 1)
    lower = jnp.where(c_i < r_i, 1.0, 0.0).astype(BF16)
    l_r = lax.broadcasted_iota(jnp.int32, (V7X_LANES, V7X_LANES), 0)
    l_c = lax.broadcasted_iota(jnp.int32, (V7X_LANES, V7X_LANES), 1)
    upper = jnp.where(l_r < l_c, 1.0, 0.0).astype(BF16)

    n_t = ea_ref.shape[1]
    tile_start = lax.broadcasted_iota(jnp.int32, (1, n_t), 1).astype(F32) * float(TM)
    tile_cls = jnp.zeros((1, n_t), jnp.int32)
    off = jnp.zeros((1, 1), F32)
    pos = jnp.zeros((rows, V7X_LANES), F32)
    for c in range(N_CLASSES):
        m = jnp.where(cls == c, 1.0, 0.0)
        mb = m.astype(BF16)
        if rows_pad > rows:
            mb_pad = jnp.concatenate([mb, jnp.zeros((rows_pad - rows, V7X_LANES), BF16)], axis=0)
        else:
            mb_pad = mb
        before_rows = jnp.dot(lower, mb_pad, preferred_element_type=F32)[:rows]
        before_rows = before_rows.sum(axis=-1, keepdims=True)
        before_lanes = jnp.dot(mb, upper, preferred_element_type=F32)
        pos = pos + m * (off + before_rows + before_lanes)
        if c > 0:
            tile_cls = tile_cls + jnp.where(tile_start >= off, 1, 0)
        count = m.sum(axis=-1, keepdims=True).sum(axis=0, keepdims=True)
        pad_lo = jnp.where(lane == c, off + count, pad_lo)
        off = off + jnp.floor((count + float(TM - 1)) * (1.0 / TM)) * float(TM)
        pad_hi = jnp.where(lane == c, off, pad_hi)
    pos_ref[...] = pos.astype(jnp.int32)
    padlo_ref[...] = pad_lo.astype(jnp.int32)
    padhi_ref[...] = pad_hi.astype(jnp.int32)
    grp = (jnp.where(tile_cls >= 6, 1, 0) + jnp.where(tile_cls >= 12, 1, 0)
           + jnp.where(tile_cls >= 18, 1, 0))
    pair = tile_cls - PAIRS_PER_GROUP * grp
    lo = jnp.where(pair >= 3, 1, 0) + jnp.where(pair >= 5, 1, 0)
    hi = pair + 1 - 2 * jnp.where(pair >= 3, 1, 0) - jnp.where(pair >= 5, 1, 0)
    ea_ref[...] = EXPERTS_PER_GROUP * grp + lo
    eb_ref[...] = EXPERTS_PER_GROUP * grp + hi
    nu_ref[...] = jnp.broadcast_to(off * (1.0 / TM), nu_ref.shape).astype(jnp.int32)


def _slots(cls2d, *, n_slot_tiles):
    rows = cls2d.shape[0]
    rows_pad = -(-rows // V7X_LANES) * V7X_LANES
    n_t = -(-n_slot_tiles // V7X_LANES) * V7X_LANES
    i32 = jnp.int32
    return pl.pallas_call(
        functools.partial(_slots_kernel, rows_pad=rows_pad),
        out_shape=[jax.ShapeDtypeStruct(cls2d.shape, i32),
                   jax.ShapeDtypeStruct((1, n_t), i32), jax.ShapeDtypeStruct((1, n_t), i32),
                   jax.ShapeDtypeStruct((1, V7X_LANES), i32), jax.ShapeDtypeStruct((1, V7X_LANES), i32),
                   jax.ShapeDtypeStruct((1, V7X_LANES), i32)],
        compiler_params=pltpu.CompilerParams(vmem_limit_bytes=V7X_VMEM_LIMIT),
        name="moe_slots",
    )(cls2d)


def _row_copy(src, dst, src_row, dst_row, sem):
    return pltpu.make_async_copy(src.at[pl.ds(src_row, 1)], dst.at[pl.ds(dst_row, 1)], sem)


def _start_tile_rows(make_copy):
    for t in range(TM):
        make_copy(t).start()


def _dispatch_kernel(pos_ref, padlo_ref, padhi_ref, nu_ref, x_ref, xs_hbm, zbuf, stage, sem, sems,
                     *, n_slot_tiles, n_tiles):
    @pl.when(pl.program_id(0) == 0)
    def _():
        zbuf[...] = jnp.zeros(zbuf.shape, zbuf.dtype)

        def zero_tile(t, carry):
            cp = pltpu.make_async_copy(zbuf, xs_hbm.at[pl.ds(pl.multiple_of(t * TM, TM), TM)], sem)
            cp.start()
            cp.wait()
            return carry

        lax.fori_loop(nu_ref[0], n_slot_tiles, zero_tile, 0)

        for c in range(N_CLASSES):
            lo, hi = padlo_ref[c], padhi_ref[c]

            def zero_row(s, carry):
                _row_copy(zbuf, xs_hbm, 0, s, sem).start()
                return carry

            def zero_wait(s, carry):
                _row_copy(zbuf, xs_hbm, 0, 0, sem).wait()
                return carry

            lax.fori_loop(lo, hi, zero_row, 0)
            lax.fori_loop(lo, hi, zero_wait, 0)

    i = pl.program_id(0)
    slot = i % 2

    def wait(s):
        pltpu.make_async_copy(stage.at[s], xs_hbm.at[pl.ds(0, TM)], sems.at[s]).wait()

    for s in range(2):
        @pl.when(slot == s)
        def _():
            stage[s] = x_ref[...]
            _start_tile_rows(lambda t: _row_copy(stage.at[s], xs_hbm, t, pos_ref[0, 0, t], sems.at[s]))

    @pl.when(i > 0)
    def _():
        wait(1 - slot)

    @pl.when(i == n_tiles - 1)
    def _():
        wait(slot)


def _dispatch(pos3, pad_lo, pad_hi, n_used, x1g, *, n_slot_tiles):
    n_tiles = pos3.shape[0]
    width = x1g.shape[1]
    smem = pl.BlockSpec(memory_space=pltpu.SMEM)
    return pl.pallas_call(
        functools.partial(_dispatch_kernel, n_slot_tiles=n_slot_tiles, n_tiles=n_tiles),
        grid=(n_tiles,),
        in_specs=[pl.BlockSpec((1, 1, TM), lambda i: (i, 0, 0), memory_space=pltpu.SMEM),
                  smem, smem, smem,
                  pl.BlockSpec((TM, width), lambda i: (i, 0))],
        out_specs=pl.BlockSpec(memory_space=pl.ANY),
        out_shape=jax.ShapeDtypeStruct((n_slot_tiles * TM, width), x1g.dtype),
        scratch_shapes=[pltpu.VMEM((TM, width), x1g.dtype), pltpu.VMEM((2, TM, width), x1g.dtype),
                        pltpu.SemaphoreType.DMA(()), pltpu.SemaphoreType.DMA((2,))],
        compiler_params=pltpu.CompilerParams(dimension_semantics=("arbitrary",), has_side_effects=True,
                                             vmem_limit_bytes=V7X_VMEM_LIMIT),
        name="moe_dispatch",
    )(pos3, pad_lo, pad_hi, n_used, x1g)


def _moe_kernel(ea_ref, eb_ref, nu_ref, xs_ref, wga, wua, wda, wgb, wub, wdb, ys_ref, *, d):
    del ea_ref, eb_ref
    used = pl.program_id(0) < nu_ref[0]

    @pl.when(jnp.logical_not(used))
    def _():
        ys_ref[...] = jnp.zeros(ys_ref.shape, ys_ref.dtype)

    @pl.when(used)
    def _():
        x = xs_ref[:, :d].astype(BF16)
        gate_a = xs_ref[:, d:d + 1]
        gate_b = xs_ref[:, d + 1:d + 2]

        def expert(wg, wu, wd):
            g = jnp.dot(x, wg[0], preferred_element_type=F32)
            u = jnp.dot(x, wu[0], preferred_element_type=F32)
            h = (g * jax.nn.sigmoid(g)) * u
            return jnp.dot(h.astype(BF16), wd[0], preferred_element_type=F32)

        ys_ref[...] = gate_a * expert(wga, wua, wda) + gate_b * expert(wgb, wub, wdb)


def _moe(tile_ea, tile_eb, n_used, xs, w_gate, w_up, w_down):
    n_slots, width = xs.shape
    d = width - GATE_LANES
    n_tiles = n_slots // TM
    d_e = w_gate.shape[-1]

    def row_map(i, ea, eb, nu):
        return (jnp.minimum(i, nu[0] - 1), 0)

    def w_a(i, ea, eb, nu):
        return (ea[i], 0, 0)

    def w_b(i, ea, eb, nu):
        return (eb[i], 0, 0)

    grid_spec = pltpu.PrefetchScalarGridSpec(
        num_scalar_prefetch=3,
        grid=(n_tiles,),
        in_specs=[pl.BlockSpec((TM, width), row_map),
                  pl.BlockSpec((1, d, d_e), w_a), pl.BlockSpec((1, d, d_e), w_a),
                  pl.BlockSpec((1, d_e, d), w_a),
                  pl.BlockSpec((1, d, d_e), w_b), pl.BlockSpec((1, d, d_e), w_b),
                  pl.BlockSpec((1, d_e, d), w_b)],
        out_specs=pl.BlockSpec((TM, d), lambda i, ea, eb, nu: (i, 0)),
    )
    return pl.pallas_call(
        functools.partial(_moe_kernel, d=d),
        grid_spec=grid_spec,
        out_shape=jax.ShapeDtypeStruct((n_slots, d), F32),
        compiler_params=_cparams(("arbitrary",)),
        name="moe_experts",
    )(tile_ea, tile_eb, n_used, xs, w_gate, w_up, w_down, w_gate, w_up, w_down)


def _gather_ahead(pos_ref, posn_ref, ys_hbm, ybuf, sems, n_tiles):
    i = pl.program_id(0)
    slot = i % 2

    def wait(s):
        pltpu.make_async_copy(ys_hbm.at[pl.ds(0, TM)], ybuf.at[s], sems.at[s]).wait()

    @pl.when(i == 0)
    def _():
        def start(t, carry):
            _row_copy(ys_hbm, ybuf.at[0], pos_ref[0, 0, t], t, sems.at[0]).start()
            return carry
        lax.fori_loop(0, TM, start, 0)

    for s in range(2):
        @pl.when(slot == 1 - s)
        def _():
            _start_tile_rows(lambda t: _row_copy(ys_hbm, ybuf.at[s], posn_ref[0, 0, t], t, sems.at[s]))

    wait(slot)

    def drain():
        @pl.when(i == n_tiles - 1)
        def _():
            wait(1 - slot)

    return slot, drain


def _combine_final_kernel(pos_ref, posn_ref, ys_hbm, x1_ref, g_ref, b_ref, yp_ref, yt_ref, ybuf, sems,
                          *, alpha, d, n_tiles, n_prompt_tiles):
    slot, drain = _gather_ahead(pos_ref, posn_ref, ys_hbm, ybuf, sems, n_tiles)
    x2 = _layer_norm(alpha * x1_ref[:, :d] + ybuf[slot], g_ref[...], b_ref[...])
    is_prompt = pl.program_id(0) < n_prompt_tiles

    @pl.when(is_prompt)
    def _():
        yp_ref[...] = x2

    @pl.when(jnp.logical_not(is_prompt))
    def _():
        yt_ref[...] = x2

    drain()


def _combine_final(pos3, ys, x1g, g, b, *, alpha, n_prompt_tiles):
    t_pad, width = x1g.shape
    d = width - GATE_LANES
    n_tiles = t_pad // TM
    last_p = n_prompt_tiles - 1
    const = lambda i: (0, 0)
    return pl.pallas_call(
        functools.partial(_combine_final_kernel, alpha=alpha, d=d, n_tiles=n_tiles,
                          n_prompt_tiles=n_prompt_tiles),
        grid=(n_tiles,),
        in_specs=[pl.BlockSpec((1, 1, TM), lambda i: (i, 0, 0), memory_space=pltpu.SMEM),
                  pl.BlockSpec((1, 1, TM), lambda i: (jnp.minimum(i + 1, n_tiles - 1), 0, 0),
                               memory_space=pltpu.SMEM),
                  pl.BlockSpec(memory_space=pl.ANY),
                  pl.BlockSpec((TM, width), lambda i: (i, 0)),
                  pl.BlockSpec(g.shape, const), pl.BlockSpec(b.shape, const)],
        out_specs=[pl.BlockSpec((TM, d), lambda i: (jnp.minimum(i, last_p), 0)),
                   pl.BlockSpec((TM, d), lambda i: (jnp.maximum(i - n_prompt_tiles, 0), 0))],
        out_shape=[jax.ShapeDtypeStruct((n_prompt_tiles * TM, d), F32),
                   jax.ShapeDtypeStruct(((n_tiles - n_prompt_tiles) * TM, d), F32)],
        scratch_shapes=[pltpu.VMEM((2, TM, d), F32), pltpu.SemaphoreType.DMA((2,))],
        compiler_params=_cparams(("arbitrary",)),
        name="moe_combine_ln_final",
    )(pos3, pos3, ys, x1g, g, b)


def _combine_inproj_kernel(pos_ref, posn_ref, ys_hbm, x1_ref, g_ref, b_ref, w_ref,
                           x2_ref, h_ref, kv_ref, ybuf, sems,
                           *, alpha, d, n_tiles):
    slot, drain = _gather_ahead(pos_ref, posn_ref, ys_hbm, ybuf, sems, n_tiles)
    sub = TM // OUTPROJ_SUBTILES
    w = w_ref[...]
    for s in range(OUTPROJ_SUBTILES):
        rows = slice(s * sub, (s + 1) * sub)
        x2 = _layer_norm(alpha * x1_ref[rows, :d] + ybuf[slot, rows, :], g_ref[...], b_ref[...])
        x2_ref[rows, :] = x2
        acc = jnp.dot(x2.astype(BF16), w, preferred_element_type=F32)
        h_ref[rows, :] = acc.astype(BF16)
        kv_ref[rows, : 2 * WIDTH_A] = acc[:, COL_KA:COL_QB]
        kv_ref[rows, 2 * WIDTH_A:] = acc[:, COL_KB:]

    drain()


def _combine_inproj(pos3, ys, x1g, g, b, w_in, *, alpha, seq, n_prompt_tiles):
    t_pad, width = x1g.shape
    d = width - GATE_LANES
    n_tiles = t_pad // TM
    tiles_per_seq = seq // TM
    n_seq = n_prompt_tiles // tiles_per_seq
    n_kv_blocks = n_seq + (n_tiles - n_prompt_tiles)
    const = lambda i: (0, 0)

    def kv_map(i):
        return (jnp.where(i < n_prompt_tiles, i // tiles_per_seq, n_seq + i - n_prompt_tiles), 0)

    return pl.pallas_call(
        functools.partial(_combine_inproj_kernel, alpha=alpha, d=d, n_tiles=n_tiles),
        grid=(n_tiles,),
        in_specs=[pl.BlockSpec((1, 1, TM), lambda i: (i, 0, 0), memory_space=pltpu.SMEM),
                  pl.BlockSpec((1, 1, TM), lambda i: (jnp.minimum(i + 1, n_tiles - 1), 0, 0),
                               memory_space=pltpu.SMEM),
                  pl.BlockSpec(memory_space=pl.ANY),
                  pl.BlockSpec((TM, width), lambda i: (i, 0)),
                  pl.BlockSpec(g.shape, const), pl.BlockSpec(b.shape, const),
                  pl.BlockSpec(w_in.shape, const)],
        out_specs=[pl.BlockSpec((TM, d), lambda i: (i, 0)),
                   pl.BlockSpec((TM, D_IN), lambda i: (i, 0)),
                   pl.BlockSpec((TM, KV_OUT), kv_map)],
        out_shape=[jax.ShapeDtypeStruct((t_pad, d), F32),
                   jax.ShapeDtypeStruct((t_pad, D_IN), BF16),
                   jax.ShapeDtypeStruct((n_kv_blocks * TM, KV_OUT), F32)],
        scratch_shapes=[pltpu.VMEM((2, TM, d), F32), pltpu.SemaphoreType.DMA((2,))],
        compiler_params=_cparams(("arbitrary",)),
        name="moe_combine_ln_inproj",
    )(pos3, pos3, ys, x1g, g, b, w_in)


def _alibi_slopes():
    return 2.0 ** (-8.0 * jnp.arange(1, N_HEADS_B + 1, dtype=F32) / N_HEADS_B)


def _rel_bias_table(rel_bias, q0, n_q, n_k):
    n_diag = n_q + n_k - 1
    dist = q0 + (n_q - 1) - jnp.arange(n_diag, dtype=jnp.int32)
    diag = jnp.take(rel_bias.astype(F32), jnp.clip(dist, -REL_CLIP, REL_CLIP) + REL_CLIP, axis=1)
    n_h = diag.shape[0]
    ext = jnp.concatenate([diag, jnp.zeros((n_h, 1), F32)], axis=1)
    shifted = jnp.tile(ext, (1, n_q))[:, :n_q * n_diag].reshape(n_h, n_q, n_diag)
    return shifted[:, :, n_q - 1:n_q - 1 + n_k]


def _alibi_table(q0, n_q, n_k):
    dist = q0 + jnp.arange(n_q, dtype=jnp.int32)[:, None] - jnp.arange(n_k, dtype=jnp.int32)[None, :]
    return -_alibi_slopes()[:, None, None] * jnp.abs(dist).astype(F32)[None]


def kernel(x_prompt, x_sample, cache_a_k, cache_a_v, cache_b_k, cache_b_v, w_in, rel_bias, attn_sinks,
           gn_a, gn_b, w_out, ln1_g, ln1_b, w_router, b_router, w_gate, w_up, w_down, ln2_g, ln2_b):
    n_seq, seq, d = x_prompt.shape
    n_dec, dec_seq, _ = x_sample.shape
    depth = w_in.shape[0]
    rows_ca, rows_cb = cache_a_k.shape[2], cache_b_k.shape[2]
    alpha = (2.0 * depth) ** 0.25
    assert seq % TM == 0 and PAD_A == TM and n_dec * dec_seq <= TM and TM % dec_seq == 0
    assert dec_seq % 16 == 0 and seq >= PAD_A

    t_prompt = n_seq * seq
    t_real = t_prompt + n_dec * dec_seq
    t_pad = -(-t_real // (2 * TM)) * (2 * TM)
    n_prompt_tiles = t_prompt // TM
    n_slot_tiles = t_pad // TM + N_CLASSES

    x = x_prompt.reshape(t_prompt, d)
    x_tail = jnp.concatenate([x_sample.reshape(n_dec * dec_seq, d), jnp.zeros((t_pad - t_real, d), F32)], axis=0)

    w_in_b = w_in.astype(BF16)
    w_out_b = w_out.astype(BF16)
    w_gate_b, w_up_b, w_down_b = w_gate.astype(BF16), w_up.astype(BF16), w_down.astype(BF16)
    wr_t = w_router.astype(F32).T
    wr_hi = wr_t.astype(BF16)
    wr_lo = (wr_t - wr_hi.astype(F32)).astype(BF16)
    br = b_router.astype(F32).reshape(N_EXPERTS, 1)

    alibi_p = _alibi_table(PAD_B, CHUNK, BAND_B * CHUNK)
    alibi_s = _alibi_table(rows_cb, dec_seq, rows_cb + dec_seq)

    pa_k, pa_v, pb_k, pb_v, sa_k, sa_v, sb_k, sb_v = ([] for _ in range(8))
    h, kv = _inproj(x, x_tail, w_in_b[0], seq=seq, n_prompt_tiles=n_prompt_tiles)
    for l in range(depth):
        sinks = attn_sinks[l].astype(F32)
        gna = gn_a[l].astype(F32).reshape(1, WIDTH_A)
        gnb = gn_b[l].astype(F32).reshape(1, WIDTH_B)
        bias_pa = _rel_bias_table(rel_bias[l], PAD_A, CHUNK, BAND_A * CHUNK)
        a_prompt = _attn_prompt(h, sinks, bias_pa, alibi_p, gna, gnb, n_seq=n_seq, seq=seq)
        bias_sa = _rel_bias_table(rel_bias[l], rows_ca, dec_seq, rows_ca + dec_seq)
        bias_s = (bias_sa[:, :, :rows_ca], bias_sa[:, :, rows_ca:],
                  alibi_s[:, :, :rows_cb], alibi_s[:, :, rows_cb:])
        a_sample = _attn_sample(
            h, sinks,
            cache_a_k[l].reshape(n_dec, rows_ca, WIDTH_A), cache_a_v[l].reshape(n_dec, rows_ca, WIDTH_A),
            cache_b_k[l].reshape(n_dec, rows_cb, KV_WIDTH_B), cache_b_v[l].reshape(n_dec, rows_cb, KV_WIDTH_B),
            bias_s, gna, gnb, first_row=t_prompt, n_rows_out=t_pad - t_prompt, n_dec=n_dec, dec_seq=dec_seq)

        x1g, cls = _outproj(a_prompt, a_sample, x, x_tail, w_out_b[l],
                            ln1_g[l].astype(F32).reshape(1, d), ln1_b[l].astype(F32).reshape(1, d),
                            wr_hi, wr_lo, br, n_tiles=t_pad // TM, n_prompt_tiles=n_prompt_tiles, alpha=alpha)
        pos, tile_ea, tile_eb, n_used, pad_lo, pad_hi = _slots(
            cls.reshape(t_pad // V7X_LANES, V7X_LANES), n_slot_tiles=n_slot_tiles)
        pos3 = pos.reshape(t_pad // TM, 1, TM)
        xs = _dispatch(pos3, pad_lo[0], pad_hi[0], n_used[0], x1g, n_slot_tiles=n_slot_tiles)
        ys = _moe(tile_ea[0, :n_slot_tiles], tile_eb[0, :n_slot_tiles], n_used[0, :1],
                  xs, w_gate_b[l], w_up_b[l], w_down_b[l])
        kv_l = kv
        g2, b2 = ln2_g[l].astype(F32).reshape(1, d), ln2_b[l].astype(F32).reshape(1, d)
        if l + 1 < depth:
            x, h, kv = _combine_inproj(pos3, ys, x1g, g2, b2, w_in_b[l + 1], alpha=alpha, seq=seq,
                                       n_prompt_tiles=n_prompt_tiles)
        else:
            y_p, y_t = _combine_final(pos3, ys, x1g, g2, b2, alpha=alpha, n_prompt_tiles=n_prompt_tiles)

        kv_p = kv_l[:n_seq * TM].reshape(n_seq, TM, KV_OUT)
        kv_s = kv_l[n_seq * TM:n_seq * TM + n_dec * dec_seq].reshape(n_dec, dec_seq, KV_OUT)
        ra, rb = min(PAD_A, seq), min(WINDOW_B, seq)
        pa_k.append(kv_p[:, TM - ra:, :WIDTH_A].reshape(n_seq, ra, N_HEADS_A, HEAD_DIM))
        pa_v.append(kv_p[:, TM - ra:, WIDTH_A:2 * WIDTH_A].reshape(n_seq, ra, N_HEADS_A, HEAD_DIM))
        pb_k.append(kv_p[:, TM - rb:, 2 * WIDTH_A:2 * WIDTH_A + KV_WIDTH_B].reshape(n_seq, rb, N_KV_B, HEAD_DIM))
        pb_v.append(kv_p[:, TM - rb:, 2 * WIDTH_A + KV_WIDTH_B:].reshape(n_seq, rb, N_KV_B, HEAD_DIM))
        sa_k.append(kv_s[:, :, :WIDTH_A].reshape(n_dec, dec_seq, N_HEADS_A, HEAD_DIM))
        sa_v.append(kv_s[:, :, WIDTH_A:2 * WIDTH_A].reshape(n_dec, dec_seq, N_HEADS_A, HEAD_DIM))
        sb_k.append(kv_s[:, :, 2 * WIDTH_A:2 * WIDTH_A + KV_WIDTH_B].reshape(n_dec, dec_seq, N_KV_B, HEAD_DIM))
        sb_v.append(kv_s[:, :, 2 * WIDTH_A + KV_WIDTH_B:].reshape(n_dec, dec_seq, N_KV_B, HEAD_DIM))

    y_prompt = y_p.reshape(n_seq, seq, d)
    y_sample = y_t[:n_dec * dec_seq].reshape(n_dec, dec_seq, d)
    return (y_prompt, y_sample,
            jnp.stack(pa_k), jnp.stack(pa_v), jnp.stack(pb_k), jnp.stack(pb_v),
            jnp.stack(sa_k), jnp.stack(sa_v), jnp.stack(sb_k), jnp.stack(sb_v))
```
